```python
import functools
import jax, jax.numpy as jnp
from jax import lax
import numpy as np

D_MODEL = 4096
BATCH = 32
SEQ = 256
DEPTH = 2
DEC_BATCH = 2
DEC_SEQ = 4096
PAST_LEN = 256

GRID_W = 64
HEAD_DIM = 128
Q_BLOCK = 128
ROPE_THETA = 10000.0
EPS = 1e-6
MLA_HEADS = 3 * D_MODEL // (8 * HEAD_DIM)
MLA_Q_LORA = 3 * D_MODEL // 16
MLA_KV_LORA = D_MODEL // 8
MLA_NOPE = 128
MLA_ROPE = 64
MLA_QK = MLA_NOPE + MLA_ROPE
MLA_V = 128
GQA_HEADS = 3 * D_MODEL // (8 * HEAD_DIM)
GQA_KV_HEADS = GQA_HEADS // 3
GQA_GROUP = GQA_HEADS // GQA_KV_HEADS
GDN_HEADS = D_MODEL // (4 * HEAD_DIM)
GDN_DK = 128
GDN_DV = 128
GDN_WIDTH = GDN_HEADS * GDN_DV
GDN_CHUNK = 64
CONV_W = 3
CONV_PAD = CONV_W // 2
D_MIX = MLA_HEADS * MLA_V + GQA_HEADS * HEAD_DIM + GDN_WIDTH
IN_SIZES = (MLA_Q_LORA, MLA_KV_LORA, MLA_ROPE,
            GQA_HEADS * HEAD_DIM, GQA_KV_HEADS * HEAD_DIM, GQA_KV_HEADS * HEAD_DIM,
            GDN_HEADS * (2 * GDN_DK + GDN_DV), GDN_WIDTH, 2 * GDN_HEADS, 2 * GDN_HEADS)
N_IN = sum(IN_SIZES)
IN_SPLITS = tuple(int(s) for s in np.cumsum(IN_SIZES)[:-1])
N_GROUPS = 8
EXPERTS_PER_GROUP = 4
N_EXPERTS = N_GROUPS * EXPERTS_PER_GROUP
TOP_K = 2
D_EXPERT = D_MODEL // 4
MOE_BLOCK = 128

kernel_name = "hybrid_mla_gqa_gdn_hmoe_diffusion_step"


def rms_norm(x, g):
    xf = x.astype(jnp.float32)
    y = xf * lax.rsqrt(jnp.mean(xf * xf, axis=-1, keepdims=True) + EPS)
    return (y * g.astype(jnp.float32)).astype(x.dtype)


def l2_norm(x):
    return x * lax.rsqrt(jnp.sum(x * x, axis=-1, keepdims=True) + EPS)


def rope_1d(x, pos):
    half = x.shape[-1] // 2
    freqs = ROPE_THETA ** (-jnp.arange(half, dtype=jnp.float32) / half)
    ang = pos.astype(jnp.float32)[:, None] * freqs[None, :]
    cos = jnp.cos(ang)[:, None, :]
    sin = jnp.sin(ang)[:, None, :]
    xf = x.astype(jnp.float32)
    x1, x2 = xf[..., :half], xf[..., half:]
    return jnp.concatenate([x1 * cos - x2 * sin, x1 * sin + x2 * cos], axis=-1).astype(x.dtype)


def rope_2d(x):
    n = x.shape[1]
    rows = n // GRID_W
    row = jnp.repeat(jnp.arange(rows, dtype=jnp.int32), GRID_W)
    col = jnp.tile(jnp.arange(GRID_W, dtype=jnp.int32), rows)
    a = x.shape[-1] // 2
    return jnp.concatenate([rope_1d(x[..., :a], row), rope_1d(x[..., a:], col)], axis=-1)


def block_attention(q, k, v, scale):
    B, Lq, Hk, G, _ = q.shape
    nb = Lq // Q_BLOCK
    qb = q.reshape(B, nb, Q_BLOCK, Hk, G, q.shape[-1]).swapaxes(0, 1)

    def one_block(qi):
        s = jnp.einsum('bqhgd,bshd->bhgqs', qi, k, preferred_element_type=jnp.float32) * scale
        p = jax.nn.softmax(s, axis=-1).astype(v.dtype)
        return jnp.einsum('bhgqs,bshd->bqhgd', p, v)

    ob = lax.map(one_block, qb)
    return ob.swapaxes(0, 1).reshape(B, Lq, Hk * G * v.shape[-1])


def mla_queries(cq, g_cq, w_q_up, g_q):
    B, L, _ = cq.shape
    q = (rms_norm(cq, g_cq) @ w_q_up).reshape(B, L, MLA_HEADS, MLA_QK)
    return rms_norm(q, g_q)


def mla_keys_values(ckv, krope, w_kv_up, g_k):
    B, L, _ = ckv.shape
    kv = (ckv @ w_kv_up).reshape(B, L, MLA_HEADS, MLA_NOPE + MLA_V)
    k_rope = jnp.broadcast_to(krope[:, :, None, :], (B, L, MLA_HEADS, MLA_ROPE))
    k = rms_norm(jnp.concatenate([kv[..., :MLA_NOPE], k_rope], axis=-1), g_k)
    return k, kv[..., MLA_NOPE:]


def rope_tail(x):
    return jnp.concatenate([x[..., :MLA_NOPE], rope_2d(x[..., MLA_NOPE:])], axis=-1)


def gqa_qkv(gq, gk, gv, g_q, g_k):
    B, L, _ = gq.shape
    q = rms_norm(gq.reshape(B, L, GQA_HEADS, HEAD_DIM), g_q)
    k = rms_norm(gk.reshape(B, L, GQA_KV_HEADS, HEAD_DIM), g_k)
    return q, k, gv.reshape(B, L, GQA_KV_HEADS, HEAD_DIM)


def short_conv(x, w):
    C = x.shape[-1]
    return lax.conv_general_dilated(x, w[:, None, :].astype(x.dtype), window_strides=(1,),
                                    padding=[(CONV_PAD, CONV_PAD)],
                                    dimension_numbers=('NWC', 'WIO', 'NWC'), feature_group_count=C)


def gdn_inputs(qkv, a, b, w_conv, a_log, dt_bias):
    B, L, _ = qkv.shape
    qkv = jax.nn.silu(short_conv(qkv, w_conv)).astype(jnp.float32)
    nk = GDN_HEADS * GDN_DK
    q = l2_norm(qkv[..., :nk].reshape(B, L, GDN_HEADS, GDN_DK)) * GDN_DK ** -0.5
    k = l2_norm(qkv[..., nk:2 * nk].reshape(B, L, GDN_HEADS, GDN_DK))
    v = qkv[..., 2 * nk:].reshape(B, L, GDN_HEADS, GDN_DV)
    a = a.astype(jnp.float32).reshape(B, L, 2, GDN_HEADS)
    b = b.astype(jnp.float32).reshape(B, L, 2, GDN_HEADS)
    g = -jnp.exp(a_log.astype(jnp.float32)) * jax.nn.softplus(a + dt_bias.astype(jnp.float32))
    beta = jax.nn.sigmoid(b)
    return q, k, v, g, beta


def gated_delta_chunked(q, k, v, g, beta, s0):
    B, L, H, _ = k.shape
    dv = v.shape[-1]
    n = L // GDN_CHUNK

    def to_chunks(t):
        t = t.reshape(B, n, GDN_CHUNK, H, *t.shape[3:])
        return jnp.moveaxis(t, (1, 3), (0, 2))

    qc, kc, vc, gc, bc = to_chunks(q), to_chunks(k), to_chunks(v), to_chunks(g), to_chunks(beta)
    decay = jnp.cumsum(gc, axis=-1)
    idx = jnp.arange(GDN_CHUNK)
    incl = idx[:, None] >= idx[None, :]
    strict = idx[:, None] > idx[None, :]
    diff = decay[..., :, None] - decay[..., None, :]
    dmask = jnp.where(incl, jnp.exp(jnp.where(incl, diff, 0.0)), 0.0)
    kb = kc * bc[..., None]
    a_mat = jnp.where(strict, jnp.einsum('nbhid,nbhjd->nbhij', kb, kc) * dmask, 0.0)
    t_mat = a_mat + jnp.eye(GDN_CHUNK, dtype=jnp.float32)
    u = lax.linalg.triangular_solve(t_mat, vc * bc[..., None], left_side=True, lower=True, unit_diagonal=True)
    w = lax.linalg.triangular_solve(t_mat, kb * jnp.exp(decay)[..., None], left_side=True, lower=True,
                                    unit_diagonal=True)
    attn = jnp.einsum('nbhid,nbhjd->nbhij', qc, kc) * dmask
    q_dec = qc * jnp.exp(decay)[..., None]
    k_dec = kc * jnp.exp(decay[..., -1:] - decay)[..., None]
    last = jnp.exp(decay[..., -1])

    def step(s, xs):
        u_i, w_i, attn_i, qd_i, kd_i, last_i = xs
        v_new = u_i - jnp.einsum('bhcd,bhde->bhce', w_i, s)
        o_i = jnp.einsum('bhcd,bhde->bhce', qd_i, s) + jnp.einsum('bhij,bhje->bhie', attn_i, v_new)
        s = s * last_i[..., None, None] + jnp.einsum('bhcd,bhce->bhde', kd_i, v_new)
        return s, o_i

    s_fin, o = lax.scan(step, s0, (u, w, attn, q_dec, k_dec, last))
    o = jnp.moveaxis(o, (0, 2), (1, 3)).reshape(B, L, H, dv)
    return o, s_fin


def gdn_bidirectional(q, k, v, g, beta, s0):
    o_f, s_f = gated_delta_chunked(q, k, v, g[:, :, 0], beta[:, :, 0], s0[:, 0])
    rev = functools.partial(jnp.flip, axis=1)
    o_b, s_b = gated_delta_chunked(rev(q), rev(k), rev(v), rev(g[:, :, 1]), rev(beta[:, :, 1]), s0[:, 1])
    return o_f + rev(o_b), jnp.stack([s_f, s_b], axis=1)


def gdn_output(o, z, g_out):
    B, L = z.shape[:2]
    zf = z.astype(jnp.float32).reshape(B, L, GDN_HEADS, GDN_DV)
    y = rms_norm(o, g_out) * jax.nn.silu(zf)
    return y.reshape(B, L, GDN_WIDTH).astype(z.dtype)


def mix_context(h, p):
    B, L, _ = h.shape
    cq, ckv, krope, gq, gk, gv, qkv, z, a, b = jnp.split(h @ p['w_in'], IN_SPLITS, axis=-1)
    ckv = rms_norm(ckv, p['g_ckv'])
    q_m = mla_queries(cq, p['g_cq'], p['w_q_up'], p['g_q_mla'])
    k_m, v_m = mla_keys_values(ckv, krope, p['w_kv_up'], p['g_k_mla'])
    o_m = block_attention(q_m[:, :, :, None], k_m, v_m, MLA_QK ** -0.5)
    q_g, k_g, v_g = gqa_qkv(gq, gk, gv, p['g_q_gqa'], p['g_k_gqa'])
    o_g = block_attention(q_g.reshape(B, L, GQA_KV_HEADS, GQA_GROUP, HEAD_DIM), k_g, v_g, HEAD_DIM ** -0.5)
    qd, kd, vd, gd, bd = gdn_inputs(qkv, a, b, p['w_conv'], p['a_log'], p['dt_bias'])
    s0 = jnp.zeros((B, 2, GDN_HEADS, GDN_DK, GDN_DV), jnp.float32)
    od, s_ctx = gdn_bidirectional(qd, kd, vd, gd, bd, s0)
    o_d = gdn_output(od, z, p['g_gdn_out'])
    out = jnp.concatenate([o_m, o_g, o_d], axis=-1) @ p['w_out']
    return out, (ckv, krope, k_g, v_g, s_ctx.astype(h.dtype))


def mix_latent(h, p, ctx_ckv, ctx_krope, ctx_k, ctx_v, ctx_state):
    B, L, _ = h.shape
    cq, ckv, krope, gq, gk, gv, qkv, z, a, b = jnp.split(h @ p['w_in'], IN_SPLITS, axis=-1)
    q_m = rope_tail(mla_queries(cq, p['g_cq'], p['w_q_up'], p['g_q_mla']))
    k_lat, v_lat = mla_keys_values(rms_norm(ckv, p['g_ckv']), krope, p['w_kv_up'], p['g_k_mla'])
    k_ctx, v_ctx = mla_keys_values(ctx_ckv, ctx_krope, p['w_kv_up'], p['g_k_mla'])
    k_m = jnp.concatenate([k_ctx, rope_tail(k_lat)], axis=1)
    v_m = jnp.concatenate([v_ctx, v_lat], axis=1)
    o_m = block_attention(q_m[:, :, :, None], k_m, v_m, MLA_QK ** -0.5)
    q_g, k_g, v_g = gqa_qkv(gq, gk, gv, p['g_q_gqa'], p['g_k_gqa'])
    k_all = jnp.concatenate([ctx_k, rope_2d(k_g)], axis=1)
    v_all = jnp.concatenate([ctx_v, v_g], axis=1)
    o_g = block_attention(rope_2d(q_g).reshape(B, L, GQA_KV_HEADS, GQA_GROUP, HEAD_DIM), k_all, v_all,
                          HEAD_DIM ** -0.5)
    qd, kd, vd, gd, bd = gdn_inputs(qkv, a, b, p['w_conv'], p['a_log'], p['dt_bias'])
    od, _ = gdn_bidirectional(qd, kd, vd, gd, bd, ctx_state.astype(jnp.float32))
    o_d = gdn_output(od, z, p['g_gdn_out'])
    out = jnp.concatenate([o_m, o_g, o_d], axis=-1) @ p['w_out']
    return out, ()


def routed_experts(x, expert_idx, gates, w_gate, w_up, w_down):
    T, D = x.shape
    K = expert_idx.shape[1]
    A = T * K
    n_blocks = -(-A // MOE_BLOCK) + N_EXPERTS
    P = n_blocks * MOE_BLOCK
    flat_e = expert_idx.reshape(-1)
    flat_tok = jnp.arange(A, dtype=jnp.int32) // K
    flat_gate = gates.reshape(-1)
    order = jnp.argsort(flat_e)
    sorted_e = flat_e[order]
    counts = jnp.zeros((N_EXPERTS,), jnp.int32).at[flat_e].add(1)
    padded = (counts + MOE_BLOCK - 1) // MOE_BLOCK * MOE_BLOCK
    start = jnp.cumsum(counts) - counts
    pend = jnp.cumsum(padded)
    pstart = pend - padded
    dest = pstart[sorted_e] + jnp.arange(A, dtype=jnp.int32) - start[sorted_e]
    slot_tok = jnp.full((P,), T, jnp.int32).at[dest].set(flat_tok[order])
    slot_gate = jnp.zeros((P,), flat_gate.dtype).at[dest].set(flat_gate[order])
    block_e = jnp.minimum(jnp.searchsorted(pend, jnp.arange(n_blocks, dtype=jnp.int32) * MOE_BLOCK, side='right'),
                          N_EXPERTS - 1).astype(jnp.int32)
    x_pad = jnp.concatenate([x, jnp.zeros((1, D), x.dtype)], axis=0)
    xb = x_pad[slot_tok].reshape(n_blocks, MOE_BLOCK, D)

    def expert_block(args):
        xi, e = args
        hmid = jax.nn.silu(xi @ w_gate[e]) * (xi @ w_up[e])
        return hmid @ w_down[e]

    yb = lax.map(expert_block, (xb, block_e)).reshape(P, D)
    y = jnp.zeros((T + 1, D), x.dtype).at[slot_tok].add(yb * slot_gate[:, None].astype(x.dtype))
    return y[:T]


def hier_moe(h, p):
    B, L, D = h.shape
    x = h.reshape(B * L, D)
    T = x.shape[0]
    gp = jax.nn.softmax((x @ p['w_rg']).astype(jnp.float32) + p['b_rg'].astype(jnp.float32), axis=-1)
    g_top, g_idx = lax.top_k(gp, 1)
    el = ((x @ p['w_re']).astype(jnp.float32) + p['b_re'].astype(jnp.float32)).reshape(T, N_GROUPS, EXPERTS_PER_GROUP)
    el = el[jnp.arange(T), g_idx[:, 0]]
    e_top, e_loc = lax.top_k(jax.nn.softmax(el, axis=-1), TOP_K)
    gates = g_top * e_top / jnp.sum(e_top, axis=-1, keepdims=True)
    experts = (g_idx * EXPERTS_PER_GROUP + e_loc).astype(jnp.int32)
    return routed_experts(x, experts, gates, p['w_gate'], p['w_up'], p['w_down']).reshape(B, L, D)


def trunk_layer(x, cond, p, mixer):
    m = jax.nn.silu(cond) @ p['w_mod'] + p['b_mod']
    sh1, sc1, g1, sh2, sc2, g2 = jnp.split(m[:, None, :], 6, axis=-1)
    mix, aux = mixer(rms_norm(x, p['g_norm_mix']) * (1 + sc1) + sh1, p)
    x = x + g1 * mix
    h = rms_norm(x, p['g_norm_ffn']) * (1 + sc2) + sh2
    x = x + g2 * hier_moe(h, p)
    return x, aux


def setup_inputs(seed: int = 0) -> dict:
    key = jax.random.key(seed)
    ks = iter(jax.random.split(key, 48))
    f32 = jnp.float32

    def nrm(shape, scale):
        return jax.random.normal(next(ks), shape, f32) * scale

    def gain(shape):
        return 1.0 + 0.05 * jax.random.normal(next(ks), shape, f32)

    a_log = jnp.log(jax.random.uniform(next(ks), (DEPTH, 2, GDN_HEADS), f32, 1.0, 16.0))
    dt = jnp.exp(jax.random.uniform(next(ks), (DEPTH, 2, GDN_HEADS), f32, np.log(1e-3), np.log(1e-1)))
    dt_bias = dt + jnp.log(-jnp.expm1(-dt))
    return {
        'x_prompt': nrm((BATCH, SEQ, D_MODEL), 1.0),
        'x_sample': nrm((DEC_BATCH, DEC_SEQ, D_MODEL), 1.0),
        'cache_mla_ckv': nrm((DEC_BATCH, DEPTH, PAST_LEN, MLA_KV_LORA), 1.0),
        'cache_mla_krope': nrm((DEC_BATCH, DEPTH, PAST_LEN, MLA_ROPE), 1.0),
        'cache_gqa_k': nrm((DEC_BATCH, DEPTH, PAST_LEN, GQA_KV_HEADS, HEAD_DIM), 1.0),
        'cache_gqa_v': nrm((DEC_BATCH, DEPTH, PAST_LEN, GQA_KV_HEADS, HEAD_DIM), 1.0),
        'state_gdn': nrm((DEC_BATCH, DEPTH, 2, GDN_HEADS, GDN_DK, GDN_DV), 0.1),
        'c': nrm((DEC_BATCH, D_MODEL), 1.0),
        'c_ctx': nrm((D_MODEL,), 1.0),
        'w_mod': nrm((DEPTH, D_MODEL, 6 * D_MODEL), D_MODEL ** -0.5),
        'b_mod': nrm((DEPTH, 6 * D_MODEL), 0.02),
        'g_norm_mix': gain((DEPTH, D_MODEL)),
        'g_norm_ffn': gain((DEPTH, D_MODEL)),
        'w_in': nrm((DEPTH, D_MODEL, N_IN), D_MODEL ** -0.5),
        'g_cq': gain((DEPTH, MLA_Q_LORA)),
        'w_q_up': nrm((DEPTH, MLA_Q_LORA, MLA_HEADS * MLA_QK), MLA_Q_LORA ** -0.5),
        'g_ckv': gain((DEPTH, MLA_KV_LORA)),
        'w_kv_up': nrm((DEPTH, MLA_KV_LORA, MLA_HEADS * (MLA_NOPE + MLA_V)), MLA_KV_LORA ** -0.5),
        'g_q_mla': gain((DEPTH, MLA_QK)),
        'g_k_mla': gain((DEPTH, MLA_QK)),
        'g_q_gqa': gain((DEPTH, HEAD_DIM)),
        'g_k_gqa': gain((DEPTH, HEAD_DIM)),
        'w_conv': nrm((DEPTH, CONV_W, GDN_HEADS * (2 * GDN_DK + GDN_DV)), CONV_W ** -0.5),
        'a_log': a_log,
        'dt_bias': dt_bias,
        'g_gdn_out': gain((DEPTH, GDN_DV)),
        'w_out': nrm((DEPTH, D_MIX, D_MODEL), D_MIX ** -0.5),
        'w_router_group': nrm((DEPTH, D_MODEL, N_GROUPS), D_MODEL ** -0.5),
        'b_router_group': nrm((DEPTH, N_GROUPS), 0.01),
        'w_router_expert': nrm((DEPTH, D_MODEL, N_EXPERTS), D_MODEL ** -0.5),
        'b_router_expert': nrm((DEPTH, N_EXPERTS), 0.01),
        'w_gate': nrm((DEPTH, N_EXPERTS, D_MODEL, D_EXPERT), D_MODEL ** -0.5),
        'w_up': nrm((DEPTH, N_EXPERTS, D_MODEL, D_EXPERT), D_MODEL ** -0.5),
        'w_down': nrm((DEPTH, N_EXPERTS, D_EXPERT, D_MODEL), D_EXPERT ** -0.5),
    }


def reference(x_prompt, x_sample, cache_mla_ckv, cache_mla_krope, cache_gqa_k, cache_gqa_v, state_gdn, c, c_ctx,
              w_mod, b_mod, g_norm_mix, g_norm_ffn, w_in, g_cq, w_q_up, g_ckv, w_kv_up, g_q_mla, g_k_mla,
              g_q_gqa, g_k_gqa, w_conv, a_log, dt_bias, g_gdn_out, w_out, w_router_group, b_router_group,
              w_router_expert, b_router_expert, w_gate, w_up, w_down):
    xp, xs = x_prompt, x_sample
    new_ckv, new_krope, new_k, new_v, new_s = [], [], [], [], []
    for l in range(DEPTH):
        p = {'w_mod': w_mod[l], 'b_mod': b_mod[l], 'g_norm_mix': g_norm_mix[l], 'g_norm_ffn': g_norm_ffn[l],
             'w_in': w_in[l], 'g_cq': g_cq[l], 'w_q_up': w_q_up[l], 'g_ckv': g_ckv[l], 'w_kv_up': w_kv_up[l],
             'g_q_mla': g_q_mla[l], 'g_k_mla': g_k_mla[l], 'g_q_gqa': g_q_gqa[l], 'g_k_gqa': g_k_gqa[l],
             'w_conv': w_conv[l], 'a_log': a_log[l], 'dt_bias': dt_bias[l], 'g_gdn_out': g_gdn_out[l],
             'w_out': w_out[l], 'w_rg': w_router_group[l], 'b_rg': b_router_group[l],
             'w_re': w_router_expert[l], 'b_re': b_router_expert[l],
             'w_gate': w_gate[l], 'w_up': w_up[l], 'w_down': w_down[l]}
        xp, (ckv, krope, k_g, v_g, s_ctx) = trunk_layer(xp, c_ctx[None, :], p, mix_context)
        new_ckv.append(ckv)
        new_krope.append(krope)
        new_k.append(k_g)
        new_v.append(v_g)
        new_s.append(s_ctx)
        latent_mixer = functools.partial(mix_latent, ctx_ckv=cache_mla_ckv[:, l], ctx_krope=cache_mla_krope[:, l],
                                         ctx_k=cache_gqa_k[:, l], ctx_v=cache_gqa_v[:, l], ctx_state=state_gdn[:, l])
        xs, _ = trunk_layer(xs, c, p, latent_mixer)
    return (xp, xs, jnp.stack(new_ckv, axis=1), jnp.stack(new_krope, axis=1), jnp.stack(new_k, axis=1),
            jnp.stack(new_v, axis=1), jnp.stack(new_s, axis=1))
```

```python
import functools
import math

import numpy as np
import jax
import jax.numpy as jnp
from jax import lax
from jax.experimental import pallas as pl
from jax.experimental.pallas import tpu as pltpu

F32 = jnp.float32
BF16 = jnp.bfloat16

EPS = 1e-6
ROPE_THETA = 10000.0
GRID_W = 64
HEAD_DIM = 128
MLA_NOPE = 128
MLA_ROPE = 64
MLA_V = 128
MLA_QK = MLA_NOPE + MLA_ROPE
MLA_PAD = 256
GDN_DK = 128
GDN_DV = 128
GDN_CHUNK = 64
TOP_K = 2
LANE = 128
VMEM_LIMIT_BYTES = 56 * 1024 * 1024
LOG2E = 1.4426950408889634


def _params(*sem):
    return pltpu.CompilerParams(dimension_semantics=sem, vmem_limit_bytes=VMEM_LIMIT_BYTES)


def _pick(n, prefs):
    for p in prefs:
        if n % p == 0:
            return p
    return n


def _mod_kernel(c_ref, w_ref, b_ref, o_ref):
    c = c_ref[...]
    s = (c * jax.nn.sigmoid(c)).astype(BF16)
    o_ref[...] = jnp.dot(s, w_ref[...].astype(BF16), preferred_element_type=F32) + b_ref[...]


def modulation(cond8, w_mod, b_mod):
    rows, d = cond8.shape
    n = w_mod.shape[1]
    tn = _pick(n, (512, 256, 128))
    return pl.pallas_call(
        _mod_kernel,
        grid=(n // tn,),
        in_specs=[pl.BlockSpec((rows, d), lambda j: (0, 0)),
                  pl.BlockSpec((d, tn), lambda j: (0, j)),
                  pl.BlockSpec((1, tn), lambda j: (0, j))],
        out_specs=pl.BlockSpec((rows, tn), lambda j: (0, j)),
        out_shape=jax.ShapeDtypeStruct((rows, n), F32),
        compiler_params=_params("parallel"),
        name="modulation",
    )(cond8, w_mod, b_mod.reshape(1, n))


def _mod_row_fn(tm, t_prompt, dec_seq):
    n_p = t_prompt // tm
    per_b = dec_seq // tm

    def row(i):
        return jnp.where(i < n_p, 0, 1 + (i - n_p) // per_b)
    return row


def _modulated_norm(x, g, sc, sh):
    ms = jnp.mean(x * x, axis=-1, keepdims=True)
    return (x * lax.rsqrt(ms + EPS) * g) * (1.0 + sc) + sh


def _norm_matmul_kernel(x_ref, g_ref, mod_ref, w_ref, o_ref, h_ref, *, sh_idx, sc_idx):
    @pl.when(pl.program_id(1) == 0)
    def _():
        h = _modulated_norm(x_ref[...], g_ref[...], mod_ref[sc_idx:sc_idx + 1, :],
                            mod_ref[sh_idx:sh_idx + 1, :])
        h_ref[...] = h.astype(BF16)
    o_ref[...] = jnp.dot(h_ref[...], w_ref[...], preferred_element_type=F32).astype(o_ref.dtype)


def norm_matmul(x, g, mod, w, *, sh_idx, sc_idx, tm, row_fn, out_dtype=F32):
    t, d = x.shape
    n = w.shape[1]
    tn = _pick(n, (512, 384, 256, 128))
    return pl.pallas_call(
        functools.partial(_norm_matmul_kernel, sh_idx=sh_idx, sc_idx=sc_idx),
        grid=(t // tm, n // tn),
        in_specs=[pl.BlockSpec((tm, d), lambda i, j: (i, 0)),
                  pl.BlockSpec((1, d), lambda i, j: (0, 0)),
                  pl.BlockSpec((None, mod.shape[1], d), lambda i, j: (row_fn(i), 0, 0)),
                  pl.BlockSpec((d, tn), lambda i, j: (0, j))],
        out_specs=pl.BlockSpec((tm, tn), lambda i, j: (i, j)),
        out_shape=jax.ShapeDtypeStruct((t, n), out_dtype),
        scratch_shapes=[pltpu.VMEM((tm, d), BF16)],
        compiler_params=_params("parallel", "arbitrary"),
        name="norm_in_proj",
    )(x, g.reshape(1, d), mod, w)


def _matmul_kernel(x_ref, w_ref, o_ref):
    o_ref[...] = jnp.dot(x_ref[...].astype(BF16), w_ref[...],
                         preferred_element_type=F32).astype(o_ref.dtype)


def matmul(x, w, *, out_dtype=F32, name="matmul"):
    t, k = x.shape
    n = w.shape[1]
    tm = _pick(t, (512, 256, 128))
    tn = _pick(n, (512, 384, 256, 128))
    return pl.pallas_call(
        _matmul_kernel,
        grid=(t // tm, n // tn),
        in_specs=[pl.BlockSpec((tm, k), lambda i, j: (i, 0)),
                  pl.BlockSpec((k, tn), lambda i, j: (0, j))],
        out_specs=pl.BlockSpec((tm, tn), lambda i, j: (i, j)),
        out_shape=jax.ShapeDtypeStruct((t, n), out_dtype),
        compiler_params=_params("parallel", "parallel"),
        name=name,
    )(x, w)


def _out_proj_kernel(om_ref, og_ref, od_ref, wm_ref, wg_ref, wd_ref, x_ref, mod_ref, o_ref, *, gate_idx):
    acc = jnp.dot(om_ref[...], wm_ref[...], preferred_element_type=F32)
    acc += jnp.dot(og_ref[...], wg_ref[...], preferred_element_type=F32)
    acc += jnp.dot(od_ref[...], wd_ref[...], preferred_element_type=F32)
    o_ref[...] = x_ref[...] + mod_ref[gate_idx:gate_idx + 1, :] * acc


def out_proj_residual(o_m, o_g, o_d, w_out, x, mod, *, gate_idx, tm, row_fn):
    t, d = x.shape
    km, kg, kd = o_m.shape[1], o_g.shape[1], o_d.shape[1]
    assert km == kg and (km + kg) % kd == 0
    tn = _pick(d, (512, 256, 128))
    return pl.pallas_call(
        functools.partial(_out_proj_kernel, gate_idx=gate_idx),
        grid=(t // tm, d // tn),
        in_specs=[pl.BlockSpec((tm, km), lambda i, j: (i, 0)),
                  pl.BlockSpec((tm, kg), lambda i, j: (i, 0)),
                  pl.BlockSpec((tm, kd), lambda i, j: (i, 0)),
                  pl.BlockSpec((km, tn), lambda i, j: (0, j)),
                  pl.BlockSpec((kg, tn), lambda i, j: (1, j)),
                  pl.BlockSpec((kd, tn), lambda i, j: ((km + kg) // kd, j)),
                  pl.BlockSpec((tm, tn), lambda i, j: (i, j)),
                  pl.BlockSpec((None, mod.shape[1], tn), lambda i, j: (row_fn(i), 0, j))],
        out_specs=pl.BlockSpec((tm, tn), lambda i, j: (i, j)),
        out_shape=jax.ShapeDtypeStruct((t, d), F32),
        compiler_params=_params("parallel", "parallel"),
        name="out_proj",
    )(o_m, o_g, o_d, w_out, w_out, w_out, x, mod)


def _ffn_norm_router_kernel(x_ref, g_ref, mod_ref, wr_ref, h_ref, lg_ref, *, sh_idx, sc_idx):
    h = _modulated_norm(x_ref[...], g_ref[...], mod_ref[sc_idx:sc_idx + 1, :],
                        mod_ref[sh_idx:sh_idx + 1, :])
    h_ref[...] = h.astype(BF16)
    lg_ref[...] = jnp.dot(h, wr_ref[...], preferred_element_type=F32, precision=lax.Precision.HIGHEST)


def ffn_norm_router(x, g, mod, w_router, *, sh_idx, sc_idx, tm, row_fn):
    t, d = x.shape
    nr = w_router.shape[1]
    return pl.pallas_call(
        functools.partial(_ffn_norm_router_kernel, sh_idx=sh_idx, sc_idx=sc_idx),
        grid=(t // tm,),
        in_specs=[pl.BlockSpec((tm, d), lambda i: (i, 0)),
                  pl.BlockSpec((1, d), lambda i: (0, 0)),
                  pl.BlockSpec((None, mod.shape[1], d), lambda i: (row_fn(i), 0, 0)),
                  pl.BlockSpec((d, nr), lambda i: (0, 0))],
        out_specs=[pl.BlockSpec((tm, d), lambda i: (i, 0)),
                   pl.BlockSpec((tm, nr), lambda i: (i, 0))],
        out_shape=[jax.ShapeDtypeStruct((t, d), BF16), jax.ShapeDtypeStruct((t, nr), F32)],
        compiler_params=_params("parallel"),
        name="ffn_norm_router",
    )(x, g.reshape(1, d), mod, w_router)


def _attn_kernel(q_ref, k_ref, v_ref, o_ref, *, group, dq, dv, tk):
    tq = q_ref.shape[0]
    lk = k_ref.shape[0]
    q = jnp.concatenate([q_ref[:, g * dq:(g + 1) * dq] for g in range(group)], axis=0)
    m_rows = group * tq

    def body(c, carry):
        m, l, acc = carry
        start = pl.multiple_of(c * tk, tk)
        kc = k_ref[pl.ds(start, tk), :]
        vc = v_ref[pl.ds(start, tk), :]
        s = lax.dot_general(q, kc, (((1,), (1,)), ((), ())), preferred_element_type=F32)
        m_new = jnp.maximum(m, jnp.max(s, axis=-1, keepdims=True))
        alpha = jnp.exp2(m - m_new)
        p = jnp.exp2(s - m_new)
        l = alpha * l + jnp.sum(p, axis=-1, keepdims=True)
        acc = alpha * acc + jnp.dot(p.astype(BF16), vc, preferred_element_type=F32)
        return m_new, l, acc

    init = (jnp.full((m_rows, 1), -jnp.inf, F32), jnp.zeros((m_rows, 1), F32), jnp.zeros((m_rows, dv), F32))
    _, l, acc = lax.fori_loop(0, lk // tk, body, init)
    o = acc / l
    for g in range(group):
        o_ref[:, g * dv:(g + 1) * dv] = o[g * tq:(g + 1) * tq].astype(o_ref.dtype)


def attention(q, k, v, *, batch, lq, lk, kv_heads, group, dq, dv, name):
    tq = _pick(lq, (256, 128))
    tk = _pick(lk, (512, 256, 128))
    nq = lq // tq
    return pl.pallas_call(
        functools.partial(_attn_kernel, group=group, dq=dq, dv=dv, tk=tk),
        grid=(batch, kv_heads, nq),
        in_specs=[pl.BlockSpec((tq, group * dq), lambda b, h, i: (b * nq + i, h)),
                  pl.BlockSpec((lk, dq), lambda b, h, i: (b, h)),
                  pl.BlockSpec((lk, dv), lambda b, h, i: (b, h))],
        out_specs=pl.BlockSpec((tq, group * dv), lambda b, h, i: (b * nq + i, h)),
        out_shape=jax.ShapeDtypeStruct((batch * lq, kv_heads * group * dv), BF16),
        compiler_params=_params("parallel", "parallel", "parallel"),
        name=name,
    )(q, k, v)


def _moe_kernel(be_ref, nu_ref, x_ref, wg_ref, wu_ref, wd_ref, o_ref, acc_ref):
    i = pl.program_id(0)
    f = pl.program_id(1)

    @pl.when(i < nu_ref[0])
    def _():
        @pl.when(f == 0)
        def _():
            acc_ref[...] = jnp.zeros_like(acc_ref)
        x = x_ref[...]
        gate = jnp.dot(x, wg_ref[...].astype(BF16), preferred_element_type=F32)
        up = jnp.dot(x, wu_ref[...].astype(BF16), preferred_element_type=F32)
        hmid = (gate * jax.nn.sigmoid(gate) * up).astype(BF16)
        acc_ref[...] += jnp.dot(hmid, wd_ref[...].astype(BF16), preferred_element_type=F32)

        @pl.when(f == pl.num_programs(1) - 1)
        def _():
            o_ref[...] = acc_ref[...].astype(o_ref.dtype)

    @pl.when(jnp.logical_and(i >= nu_ref[0], f == 0))
    def _():
        o_ref[...] = jnp.zeros_like(o_ref)


def moe_experts(xb, block_e, n_used, w_gate, w_up, w_down, *, tm):
    p, d = xb.shape
    f_dim = w_gate.shape[2]
    tf = _pick(f_dim, (256, 128))
    n_blocks = p // tm

    def blk(i, nu):
        return jnp.minimum(i, nu[0] - 1)

    grid_spec = pltpu.PrefetchScalarGridSpec(
        num_scalar_prefetch=2,
        grid=(n_blocks, f_dim // tf),
        in_specs=[pl.BlockSpec((tm, d), lambda i, f, be, nu: (blk(i, nu), 0)),
                  pl.BlockSpec((None, d, tf), lambda i, f, be, nu: (be[blk(i, nu)], 0, jnp.where(i < nu[0], f, f_dim // tf - 1))),
                  pl.BlockSpec((None, d, tf), lambda i, f, be, nu: (be[blk(i, nu)], 0, jnp.where(i < nu[0], f, f_dim // tf - 1))),
                  pl.BlockSpec((None, tf, d), lambda i, f, be, nu: (be[blk(i, nu)], jnp.where(i < nu[0], f, f_dim // tf - 1), 0))],
        out_specs=pl.BlockSpec((tm, d), lambda i, f, be, nu: (i, 0)),
        scratch_shapes=[pltpu.VMEM((tm, d), F32)],
    )
    return pl.pallas_call(
        _moe_kernel,
        grid_spec=grid_spec,
        out_shape=jax.ShapeDtypeStruct((p, d), BF16),
        compiler_params=_params("arbitrary", "arbitrary"),
        name="moe_experts",
    )(block_e, n_used, xb, w_gate, w_up, w_down)


def _rms(x, g):
    return x * lax.rsqrt(jnp.mean(x * x, axis=-1, keepdims=True) + EPS) * g


def _rope_tables(t_prompt, dec_batch, dec_seq, width):
    a = width // 2
    half = a // 2
    freqs = ROPE_THETA ** (-np.arange(half, dtype=np.float32) / half)
    t = np.arange(dec_seq)
    row, col = t // GRID_W, t % GRID_W
    ang_r = row[:, None].astype(np.float32) * freqs[None, :]
    ang_c = col[:, None].astype(np.float32) * freqs[None, :]
    cos = np.concatenate([np.cos(ang_r), np.cos(ang_r), np.cos(ang_c), np.cos(ang_c)], axis=-1)
    sin = np.concatenate([-np.sin(ang_r), np.sin(ang_r), -np.sin(ang_c), np.sin(ang_c)], axis=-1)
    cos = np.concatenate([np.ones((t_prompt, width), np.float32), np.tile(cos, (dec_batch, 1))], axis=0)
    sin = np.concatenate([np.zeros((t_prompt, width), np.float32), np.tile(sin, (dec_batch, 1))], axis=0)
    return jnp.asarray(cos), jnp.asarray(sin)


def _rope_apply(x, cos, sin):
    half = x.shape[-1] // 4
    xs = x.reshape(x.shape[:-1] + (2, 2, half))
    swapped = jnp.flip(xs, axis=-2).reshape(x.shape)
    return x * cos[:, None, :] + swapped * sin[:, None, :]


def _short_conv(x, w):
    xm = jnp.pad(x, ((0, 0), (1, 0), (0, 0)))[:, :-1]
    xp = jnp.pad(x, ((0, 0), (0, 1), (0, 0)))[:, 1:]
    return xm * w[0] + x * w[1] + xp * w[2]


def _l2(x):
    return x * lax.rsqrt(jnp.sum(x * x, axis=-1, keepdims=True) + EPS)


def _gdn_inputs(qkv, a, b, w_conv, a_log, dt_bias, heads):
    bsz, length, _ = qkv.shape
    qkv = jax.nn.silu(_short_conv(qkv, w_conv))
    nk = heads * GDN_DK
    q = _l2(qkv[..., :nk].reshape(bsz, length, heads, GDN_DK)) * GDN_DK ** -0.5
    k = _l2(qkv[..., nk:2 * nk].reshape(bsz, length, heads, GDN_DK))
    v = qkv[..., 2 * nk:].reshape(bsz, length, heads, GDN_DV)
    a = a.reshape(bsz, length, 2, heads)
    b = b.reshape(bsz, length, 2, heads)
    g = -jnp.exp(a_log) * jax.nn.softplus(a + dt_bias)
    return q, k, v, g, jax.nn.sigmoid(b)


def _gated_delta_chunked(q, k, v, g, beta, s0):
    bsz, length, heads, _ = k.shape
    dv = v.shape[-1]
    n = length // GDN_CHUNK
    hi = lax.Precision.HIGHEST

    def to_chunks(t):
        t = t.reshape(bsz, n, GDN_CHUNK, heads, *t.shape[3:])
        return jnp.moveaxis(t, (1, 3), (0, 2))

    qc, kc, vc, gc, bc = to_chunks(q), to_chunks(k), to_chunks(v), to_chunks(g), to_chunks(beta)
    decay = jnp.cumsum(gc, axis=-1)
    idx = jnp.arange(GDN_CHUNK)
    incl = idx[:, None] >= idx[None, :]
    strict = idx[:, None] > idx[None, :]
    diff = decay[..., :, None] - decay[..., None, :]
    dmask = jnp.where(incl, jnp.exp(jnp.where(incl, diff, 0.0)), 0.0)
    kb = kc * bc[..., None]
    a_mat = jnp.where(strict, jnp.einsum('nbhid,nbhjd->nbhij', kb, kc, precision=hi) * dmask, 0.0)
    t_mat = a_mat + jnp.eye(GDN_CHUNK, dtype=F32)
    u = lax.linalg.triangular_solve(t_mat, vc * bc[..., None], left_side=True, lower=True, unit_diagonal=True)
    w = lax.linalg.triangular_solve(t_mat, kb * jnp.exp(decay)[..., None], left_side=True, lower=True,
                                    unit_diagonal=True)
    attn = jnp.einsum('nbhid,nbhjd->nbhij', qc, kc, precision=hi) * dmask
    q_dec = qc * jnp.exp(decay)[..., None]
    k_dec = kc * jnp.exp(decay[..., -1:] - decay)[..., None]
    last = jnp.exp(decay[..., -1])

    def step(s, xs):
        u_i, w_i, attn_i, qd_i, kd_i, last_i = xs
        v_new = u_i - jnp.einsum('bhcd,bhde->bhce', w_i, s, precision=hi)
        o_i = (jnp.einsum('bhcd,bhde->bhce', qd_i, s, precision=hi)
               + jnp.einsum('bhij,bhje->bhie', attn_i, v_new, precision=hi))
        s = s * last_i[..., None, None] + jnp.einsum('bhcd,bhce->bhde', kd_i, v_new, precision=hi)
        return s, o_i

    s_fin, o = lax.scan(step, s0, (u, w, attn, q_dec, k_dec, last))
    o = jnp.moveaxis(o, (0, 2), (1, 3)).reshape(bsz, length, heads, dv)
    return o, s_fin


def _gdn_bidirectional(q, k, v, g, beta, s0):
    o_f, s_f = _gated_delta_chunked(q, k, v, g[:, :, 0], beta[:, :, 0], s0[:, 0])
    rev = functools.partial(jnp.flip, axis=1)
    o_b, s_b = _gated_delta_chunked(rev(q), rev(k), rev(v), rev(g[:, :, 1]), rev(beta[:, :, 1]), s0[:, 1])
    return o_f + rev(o_b), jnp.stack([s_f, s_b], axis=1)


def _gdn_stream(qkv, a, b, z, s0, w_conv, a_log, dt_bias, g_out, heads):
    qd, kd, vd, gd, bd = _gdn_inputs(qkv, a, b, w_conv, a_log, dt_bias, heads)
    od, s_fin = _gdn_bidirectional(qd, kd, vd, gd, bd, s0)
    bsz, length = z.shape[:2]
    zf = z.reshape(bsz, length, heads, GDN_DV)
    y = _rms(od, g_out) * jax.nn.silu(zf)
    return y.reshape(bsz * length, heads * GDN_DV).astype(BF16), s_fin


def _route(logits, b_rg, b_re, n_groups, n_experts):
    t = logits.shape[0]
    epg = n_experts // n_groups
    gp = jax.nn.softmax(logits[:, :n_groups] + b_rg, axis=-1)
    g_top, g_idx = lax.top_k(gp, 1)
    el = (logits[:, n_groups:n_groups + n_experts] + b_re).reshape(t, n_groups, epg)
    el = jnp.take_along_axis(el, g_idx[:, :, None], axis=1)[:, 0]
    e_top, e_loc = lax.top_k(jax.nn.softmax(el, axis=-1), TOP_K)
    gates = g_top * e_top / jnp.sum(e_top, axis=-1, keepdims=True)
    experts = (g_idx * epg + e_loc).astype(jnp.int32)
    return experts, gates


def _dispatch_plan(experts, n_experts, tm):
    t, k = experts.shape
    a = t * k
    n_blocks = a // tm + n_experts
    flat_e = experts.reshape(-1)
    onehot = (flat_e[:, None] == jnp.arange(n_experts, dtype=jnp.int32)[None, :]).astype(jnp.int32)
    csum = jnp.cumsum(onehot, axis=0)
    pos = jnp.sum((csum - onehot) * onehot, axis=1)
    counts = csum[-1]
    padded = (counts + tm - 1) // tm * tm
    pend = jnp.cumsum(padded)
    pstart = pend - padded
    dest = pstart[flat_e] + pos
    flat_tok = jnp.arange(a, dtype=jnp.int32) // k
    slot_tok = jnp.zeros((n_blocks * tm,), jnp.int32).at[dest].set(flat_tok)
    block_e = jnp.minimum(jnp.searchsorted(pend, jnp.arange(n_blocks, dtype=jnp.int32) * tm, side='right'),
                          n_experts - 1).astype(jnp.int32)
    n_used = (pend[-1] // tm).astype(jnp.int32).reshape(1)
    return dest.reshape(t, k), slot_tok, block_e, n_used


def kernel(x_prompt, x_sample, cache_mla_ckv, cache_mla_krope, cache_gqa_k, cache_gqa_v, state_gdn, c, c_ctx, w_mod, b_mod, g_norm_mix, g_norm_ffn, w_in, g_cq, w_q_up, g_ckv, w_kv_up, g_q_mla, g_k_mla, g_q_gqa, g_k_gqa, w_conv, a_log, dt_bias, g_gdn_out, w_out, w_router_group, b_router_group, w_router_expert, b_router_expert, w_gate, w_up, w_down):
    batch, seq, d = x_prompt.shape
    dec_batch, dec_seq, _ = x_sample.shape
    depth = w_in.shape[0]
    past = cache_mla_ckv.shape[2]
    q_lora = w_q_up.shape[1]
    kv_lora = w_kv_up.shape[1]
    mla_heads = w_q_up.shape[2] // MLA_QK
    kv_heads = cache_gqa_k.shape[3]
    gdn_heads = state_gdn.shape[3]
    gqa_heads = (w_out.shape[1] - mla_heads * MLA_V - gdn_heads * GDN_DV) // HEAD_DIM
    group = gqa_heads // kv_heads
    n_groups = w_router_group.shape[2]
    n_experts = w_gate.shape[1]
    gdn_qkv = gdn_heads * (2 * GDN_DK + GDN_DV)
    gdn_width = gdn_heads * GDN_DV

    t_p = batch * seq
    t_s = dec_batch * dec_seq
    t_all = t_p + t_s
    tm = _pick(math.gcd(t_p, dec_seq), (512, 256, 128))
    row_fn = _mod_row_fn(tm, t_p, dec_seq)
    moe_tm = _pick(t_all * TOP_K, (512, 256, 128))

    sizes = (q_lora, kv_lora, MLA_ROPE, gqa_heads * HEAD_DIM, kv_heads * HEAD_DIM, kv_heads * HEAD_DIM,
             gdn_qkv, gdn_width, 2 * gdn_heads, 2 * gdn_heads)
    off = np.concatenate([[0], np.cumsum(sizes)])
    order = (0, 1, 3, 4, 5, 6, 7, 2, 8, 9)
    new_sizes = [sizes[i] for i in order]
    n_in = int(off[-1])
    n_in_pad = -(-n_in // LANE) * LANE
    noff = np.concatenate([[0], np.cumsum(new_sizes)])
    seg = {name: (int(noff[j]), int(noff[j + 1])) for j, name in
           enumerate(('cq', 'ckv', 'gq', 'gk', 'gv', 'qkv', 'z', 'krope', 'a', 'b'))}
    for name in ('cq', 'ckv', 'gq', 'gk', 'gv', 'qkv', 'z', 'krope'):
        assert seg[name][0] % LANE == 0, name

    cos_g, sin_g = _rope_tables(t_p, dec_batch, dec_seq, HEAD_DIM)
    cos_m, sin_m = _rope_tables(t_p, dec_batch, dec_seq, MLA_ROPE)

    cond8 = jnp.zeros((8, d), F32).at[0].set(c_ctx).at[1:1 + dec_batch].set(c)

    x = jnp.concatenate([x_prompt.reshape(t_p, d), x_sample.reshape(t_s, d)], axis=0)
    new_ckv, new_krope, new_k, new_v, new_s = [], [], [], [], []
    q_scale_m = MLA_QK ** -0.5 * LOG2E
    q_scale_g = HEAD_DIM ** -0.5 * LOG2E

    for l in range(depth):
        mod = modulation(cond8, w_mod[l], b_mod[l]).reshape(8, 6, d)

        w_in_r = jnp.concatenate([w_in[l][:, off[i]:off[i + 1]] for i in order]
                                 + [jnp.zeros((d, n_in_pad - n_in), F32)], axis=1).astype(BF16)
        u = norm_matmul(x, g_norm_mix[l], mod, w_in_r, sh_idx=0, sc_idx=1, tm=tm, row_fn=row_fn)

        def take(name):
            return u[:, seg[name][0]:seg[name][1]]

        ckv_n = _rms(take('ckv'), g_ckv[l])
        krope = take('krope')
        cq_n = _rms(take('cq'), g_cq[l]).astype(BF16)
        wq = w_q_up[l].reshape(q_lora, mla_heads, MLA_QK)
        wq = jnp.pad(wq, ((0, 0), (0, 0), (0, MLA_PAD - MLA_QK))).reshape(q_lora, mla_heads * MLA_PAD).astype(BF16)
        q_m = matmul(cq_n, wq, name="mla_q_up").reshape(t_all, mla_heads, MLA_PAD)
        g_q_pad = jnp.pad(g_q_mla[l], (0, MLA_PAD - MLA_QK))
        q_m = q_m * lax.rsqrt(jnp.sum(q_m * q_m, axis=-1, keepdims=True) / MLA_QK + EPS) * g_q_pad
        q_rot = _rope_apply(q_m[..., MLA_NOPE:MLA_QK], cos_m, sin_m)
        q_m = jnp.concatenate([q_m[..., :MLA_NOPE], q_rot, q_m[..., MLA_QK:]], axis=-1)
        q_m = (q_m * q_scale_m).astype(BF16).reshape(t_all, mla_heads * MLA_PAD)

        wkv = w_kv_up[l].reshape(kv_lora, mla_heads, MLA_NOPE + MLA_V)
        wkv = jnp.concatenate([wkv[..., :MLA_NOPE].reshape(kv_lora, -1), wkv[..., MLA_NOPE:].reshape(kv_lora, -1)],
                              axis=1).astype(BF16)
        ckv_all = jnp.concatenate([ckv_n, cache_mla_ckv[:, l].reshape(dec_batch * past, kv_lora)], axis=0)
        krope_all = jnp.concatenate([krope, cache_mla_krope[:, l].reshape(dec_batch * past, MLA_ROPE)], axis=0)
        cos_all = jnp.concatenate([cos_m, jnp.ones((dec_batch * past, MLA_ROPE), F32)], axis=0)
        sin_all = jnp.concatenate([sin_m, jnp.zeros((dec_batch * past, MLA_ROPE), F32)], axis=0)
        kv = matmul(ckv_all, wkv, name="mla_kv_up")
        n_all = kv.shape[0]
        k_nope = kv[:, :mla_heads * MLA_NOPE].reshape(n_all, mla_heads, MLA_NOPE)
        v_m = kv[:, mla_heads * MLA_NOPE:].astype(BF16)
        ssq = jnp.sum(k_nope * k_nope, axis=-1, keepdims=True) + jnp.sum(krope_all * krope_all, axis=-1)[:, None, None]
        r_k = lax.rsqrt(ssq / MLA_QK + EPS)
        k_rope_h = _rope_apply(krope_all[:, None, :] * r_k * g_k_mla[l][MLA_NOPE:], cos_all, sin_all)
        k_m = jnp.concatenate([k_nope * r_k * g_k_mla[l][:MLA_NOPE], k_rope_h,
                               jnp.zeros((n_all, mla_heads, MLA_PAD - MLA_QK), F32)], axis=-1)
        k_m = k_m.astype(BF16).reshape(n_all, mla_heads * MLA_PAD)

        def with_ctx(lat, ctx):
            lat = lat[t_p:t_all].reshape(dec_batch, dec_seq, -1)
            ctx = ctx.reshape(dec_batch, past, -1)
            return jnp.concatenate([ctx, lat], axis=1).reshape(dec_batch * (past + dec_seq), -1)

        o_m_p = attention(q_m[:t_p], k_m[:t_p], v_m[:t_p], batch=batch, lq=seq, lk=seq, kv_heads=mla_heads,
                          group=1, dq=MLA_PAD, dv=MLA_V, name="mla_attn_ctx")
        o_m_s = attention(q_m[t_p:], with_ctx(k_m, k_m[t_all:]), with_ctx(v_m, v_m[t_all:]), batch=dec_batch,
                          lq=dec_seq, lk=past + dec_seq, kv_heads=mla_heads, group=1, dq=MLA_PAD, dv=MLA_V,
                          name="mla_attn_lat")
        o_m = jnp.concatenate([o_m_p, o_m_s], axis=0)

        q_g = _rms(take('gq').reshape(t_all, gqa_heads, HEAD_DIM), g_q_gqa[l])
        q_g = (_rope_apply(q_g, cos_g, sin_g) * q_scale_g).astype(BF16).reshape(t_all, gqa_heads * HEAD_DIM)
        k_g = _rms(take('gk').reshape(t_all, kv_heads, HEAD_DIM), g_k_gqa[l])
        v_g = take('gv')
        k_g_rot = _rope_apply(k_g, cos_g, sin_g).astype(BF16).reshape(t_all, kv_heads * HEAD_DIM)
        v_g_b = v_g.astype(BF16)
        ctx_k = cache_gqa_k[:, l].reshape(dec_batch * past, kv_heads * HEAD_DIM).astype(BF16)
        ctx_v = cache_gqa_v[:, l].reshape(dec_batch * past, kv_heads * HEAD_DIM).astype(BF16)
        o_g_p = attention(q_g[:t_p], k_g_rot[:t_p], v_g_b[:t_p], batch=batch, lq=seq, lk=seq, kv_heads=kv_heads,
                          group=group, dq=HEAD_DIM, dv=HEAD_DIM, name="gqa_attn_ctx")
        o_g_s = attention(q_g[t_p:], with_ctx(k_g_rot, ctx_k), with_ctx(v_g_b, ctx_v), batch=dec_batch, lq=dec_seq,
                          lk=past + dec_seq, kv_heads=kv_heads, group=group, dq=HEAD_DIM, dv=HEAD_DIM,
                          name="gqa_attn_lat")
        o_g = jnp.concatenate([o_g_p, o_g_s], axis=0)

        qkv, z, a_, b_ = take('qkv'), take('z'), take('a'), take('b')
        s0_p = jnp.zeros((batch, 2, gdn_heads, GDN_DK, GDN_DV), F32)
        o_d_p, s_ctx = _gdn_stream(qkv[:t_p].reshape(batch, seq, -1), a_[:t_p].reshape(batch, seq, -1),
                                   b_[:t_p].reshape(batch, seq, -1), z[:t_p].reshape(batch, seq, -1), s0_p,
                                   w_conv[l], a_log[l], dt_bias[l], g_gdn_out[l], gdn_heads)
        o_d_s, _ = _gdn_stream(qkv[t_p:].reshape(dec_batch, dec_seq, -1), a_[t_p:].reshape(dec_batch, dec_seq, -1),
                               b_[t_p:].reshape(dec_batch, dec_seq, -1), z[t_p:].reshape(dec_batch, dec_seq, -1),
                               state_gdn[:, l], w_conv[l], a_log[l], dt_bias[l], g_gdn_out[l], gdn_heads)
        o_d = jnp.concatenate([o_d_p, o_d_s], axis=0)

        new_ckv.append(ckv_n[:t_p].reshape(batch, seq, kv_lora))
        new_krope.append(krope[:t_p].reshape(batch, seq, MLA_ROPE))
        new_k.append(k_g[:t_p].reshape(batch, seq, kv_heads, HEAD_DIM))
        new_v.append(v_g[:t_p].reshape(batch, seq, kv_heads, HEAD_DIM))
        new_s.append(s_ctx)

        x = out_proj_residual(o_m, o_g, o_d, w_out[l].astype(BF16), x, mod, gate_idx=2, tm=tm, row_fn=row_fn)

        w_router = jnp.concatenate([w_router_group[l], w_router_expert[l],
                                    jnp.zeros((d, LANE - n_groups - n_experts), F32)], axis=1)
        h2, logits = ffn_norm_router(x, g_norm_ffn[l], mod, w_router, sh_idx=3, sc_idx=4, tm=tm, row_fn=row_fn)
        experts, gates = _route(logits, b_router_group[l], b_router_expert[l], n_groups, n_experts)
        dest, slot_tok, block_e, n_used = _dispatch_plan(experts, n_experts, moe_tm)
        yb = moe_experts(h2[slot_tok], block_e, n_used, w_gate[l], w_up[l], w_down[l], tm=moe_tm)
        y = jnp.sum(yb[dest].astype(F32) * gates[:, :, None], axis=1)
        gate2 = jnp.concatenate([jnp.broadcast_to(mod[0, 5], (t_p, d))]
                                + [jnp.broadcast_to(mod[1 + bi, 5], (dec_seq, d)) for bi in range(dec_batch)], axis=0)
        x = x + gate2 * y

    return (x[:t_p].reshape(batch, seq, d), x[t_p:].reshape(dec_batch, dec_seq, d),
            jnp.stack(new_ckv, axis=1), jnp.stack(new_krope, axis=1), jnp.stack(new_k, axis=1),
            jnp.stack(new_v, axis=1), jnp.stack(new_s, axis=1))
```

```python
import functools
import math

import numpy as np
import jax
import jax.numpy as jnp
from jax import lax
from jax.experimental import pallas as pl
from jax.experimental.pallas import tpu as pltpu

F32 = jnp.float32
BF16 = jnp.bfloat16

EPS = 1e-6
ROPE_THETA = 10000.0
GRID_W = 64
HEAD_DIM = 128
MLA_NOPE = 128
MLA_ROPE = 64
MLA_V = 128
MLA_QK = MLA_NOPE + MLA_ROPE
MLA_PAD = 256
GDN_DK = 128
GDN_DV = 128
GDN_CHUNK = 64
TOP_K = 2
LANE = 128
VMEM_LIMIT_BYTES = 56 * 1024 * 1024
LOG2E = 1.4426950408889634
ATTN_UNROLL = 4


def _params(*sem):
    return pltpu.CompilerParams(dimension_semantics=sem, vmem_limit_bytes=VMEM_LIMIT_BYTES)


def _pick(n, prefs):
    for p in prefs:
        if n % p == 0:
            return p
    return n


def _mod_kernel(c_ref, w_ref, b_ref, o_ref):
    c = c_ref[...]
    s = (c * jax.nn.sigmoid(c)).astype(BF16)
    o_ref[...] = jnp.dot(s, w_ref[...].astype(BF16), preferred_element_type=F32) + b_ref[...]


def modulation(cond8, w_mod, b_mod, layer):
    rows, d = cond8.shape
    depth, _, n = w_mod.shape
    tn = _pick(n, (512, 256, 128))
    return pl.pallas_call(
        _mod_kernel,
        grid=(n // tn,),
        in_specs=[pl.BlockSpec((rows, d), lambda j: (0, 0)),
                  pl.BlockSpec((None, d, tn), lambda j: (layer, 0, j)),
                  pl.BlockSpec((None, 1, tn), lambda j: (layer, 0, j))],
        out_specs=pl.BlockSpec((rows, tn), lambda j: (0, j)),
        out_shape=jax.ShapeDtypeStruct((rows, n), F32),
        compiler_params=_params("parallel"),
        name="modulation",
    )(cond8, w_mod, b_mod.reshape(depth, 1, n))


def _mod_row_fn(tm, t_prompt, dec_seq):
    n_p = t_prompt // tm
    per_b = dec_seq // tm

    def row(i):
        return jnp.where(i < n_p, 0, 1 + (i - n_p) // per_b)
    return row


def _modulated_norm(x, g, sc, sh):
    ms = jnp.mean(x * x, axis=-1, keepdims=True)
    return (x * lax.rsqrt(ms + EPS) * g) * (1.0 + sc) + sh


def _norm_matmul_kernel(x_ref, g_ref, mod_ref, w_ref, o_ref, h_ref, *, sh_idx, sc_idx):
    @pl.when(pl.program_id(1) == 0)
    def _():
        h = _modulated_norm(x_ref[...], g_ref[...], mod_ref[sc_idx:sc_idx + 1, :],
                            mod_ref[sh_idx:sh_idx + 1, :])
        h_ref[...] = h.astype(BF16)
    o_ref[...] = jnp.dot(h_ref[...], w_ref[...], preferred_element_type=F32).astype(o_ref.dtype)


def norm_matmul(x, g, mod, w, *, sh_idx, sc_idx, tm, row_fn, out_dtype=F32):
    t, d = x.shape
    n = w.shape[1]
    tn = _pick(n, (512, 384, 256, 128))
    return pl.pallas_call(
        functools.partial(_norm_matmul_kernel, sh_idx=sh_idx, sc_idx=sc_idx),
        grid=(t // tm, n // tn),
        in_specs=[pl.BlockSpec((tm, d), lambda i, j: (i, 0)),
                  pl.BlockSpec((1, d), lambda i, j: (0, 0)),
                  pl.BlockSpec((None, mod.shape[1], d), lambda i, j: (row_fn(i), 0, 0)),
                  pl.BlockSpec((d, tn), lambda i, j: (0, j))],
        out_specs=pl.BlockSpec((tm, tn), lambda i, j: (i, j)),
        out_shape=jax.ShapeDtypeStruct((t, n), out_dtype),
        scratch_shapes=[pltpu.VMEM((tm, d), BF16)],
        compiler_params=_params("parallel", "arbitrary"),
        name="norm_in_proj",
    )(x, g.reshape(1, d), mod, w)


def _matmul_kernel(x_ref, w_ref, o_ref):
    o_ref[...] = jnp.dot(x_ref[...].astype(BF16), w_ref[...],
                         preferred_element_type=F32).astype(o_ref.dtype)


def matmul(x, w, *, out_dtype=F32, name="matmul"):
    t, k = x.shape
    n = w.shape[1]
    tm = _pick(t, (512, 256, 128))
    tn = _pick(n, (512, 384, 256, 128))
    return pl.pallas_call(
        _matmul_kernel,
        grid=(t // tm, n // tn),
        in_specs=[pl.BlockSpec((tm, k), lambda i, j: (i, 0)),
                  pl.BlockSpec((k, tn), lambda i, j: (0, j))],
        out_specs=pl.BlockSpec((tm, tn), lambda i, j: (i, j)),
        out_shape=jax.ShapeDtypeStruct((t, n), out_dtype),
        compiler_params=_params("parallel", "parallel"),
        name=name,
    )(x, w)


def _out_proj_kernel(om_ref, og_ref, od_ref, wm_ref, wg_ref, wd_ref, x_ref, mod_ref, o_ref, *, gate_idx):
    acc = jnp.dot(om_ref[...], wm_ref[...], preferred_element_type=F32)
    acc += jnp.dot(og_ref[...], wg_ref[...], preferred_element_type=F32)
    acc += jnp.dot(od_ref[...], wd_ref[...], preferred_element_type=F32)
    o_ref[...] = x_ref[...] + mod_ref[gate_idx:gate_idx + 1, :] * acc


def out_proj_residual(o_m, o_g, o_d, w_out, x, mod, *, gate_idx, tm, row_fn):
    t, d = x.shape
    km, kg, kd = o_m.shape[1], o_g.shape[1], o_d.shape[1]
    assert km == kg and (km + kg) % kd == 0
    tn = _pick(d, (512, 256, 128))
    return pl.pallas_call(
        functools.partial(_out_proj_kernel, gate_idx=gate_idx),
        grid=(t // tm, d // tn),
        in_specs=[pl.BlockSpec((tm, km), lambda i, j: (i, 0)),
                  pl.BlockSpec((tm, kg), lambda i, j: (i, 0)),
                  pl.BlockSpec((tm, kd), lambda i, j: (i, 0)),
                  pl.BlockSpec((km, tn), lambda i, j: (0, j)),
                  pl.BlockSpec((kg, tn), lambda i, j: (1, j)),
                  pl.BlockSpec((kd, tn), lambda i, j: ((km + kg) // kd, j)),
                  pl.BlockSpec((tm, tn), lambda i, j: (i, j)),
                  pl.BlockSpec((None, mod.shape[1], tn), lambda i, j: (row_fn(i), 0, j))],
        out_specs=pl.BlockSpec((tm, tn), lambda i, j: (i, j)),
        out_shape=jax.ShapeDtypeStruct((t, d), F32),
        compiler_params=_params("parallel", "parallel"),
        name="out_proj",
    )(o_m, o_g, o_d, w_out, w_out, w_out, x, mod)


def _ffn_norm_router_kernel(x_ref, g_ref, mod_ref, wr_ref, br_ref, h_ref, gate_ref, exp_ref, *,
                            sh_idx, sc_idx, n_groups, epg):
    h = _modulated_norm(x_ref[...], g_ref[...], mod_ref[sc_idx:sc_idx + 1, :],
                        mod_ref[sh_idx:sh_idx + 1, :])
    h_ref[...] = h.astype(BF16)
    lg = jnp.dot(h, wr_ref[...], preferred_element_type=F32, precision=lax.Precision.HIGHEST) + br_ref[...]
    lane = lax.broadcasted_iota(jnp.int32, lg.shape, 1).astype(F32)
    neg = jnp.float32(-1e30)
    far = jnp.float32(2 * LANE)

    def first_argmax(v):
        top = jnp.max(v, axis=-1, keepdims=True)
        return top, jnp.min(jnp.where(v == top, lane, far), axis=-1, keepdims=True)

    is_group = lane < n_groups
    g_max, g_idx = first_argmax(jnp.where(is_group, lg, neg))
    g_top = 1.0 / jnp.sum(jnp.where(is_group, jnp.exp(lg - g_max), 0.0), axis=-1, keepdims=True)
    lo = n_groups + g_idx * epg
    el = jnp.where(jnp.logical_and(lane >= lo, lane < lo + epg), lg, neg)
    e1, i1 = first_argmax(el)
    e2, i2 = first_argmax(jnp.where(lane == i1, neg, el))
    r = jnp.exp(e2 - e1)
    gate1 = g_top / (1.0 + r)
    gate2 = g_top * r / (1.0 + r)
    gate_ref[...] = jnp.where(lane == 0, gate1, jnp.where(lane == 1, gate2, 0.0))
    exp_ref[...] = jnp.where(lane == 0, i1 - n_groups, jnp.where(lane == 1, i2 - n_groups, 0.0)).astype(jnp.int32)


def ffn_norm_router(x, g, mod, w_router, b_router, *, sh_idx, sc_idx, tm, row_fn, n_groups, epg):
    t, d = x.shape
    nr = w_router.shape[1]
    return pl.pallas_call(
        functools.partial(_ffn_norm_router_kernel, sh_idx=sh_idx, sc_idx=sc_idx, n_groups=n_groups, epg=epg),
        grid=(t // tm,),
        in_specs=[pl.BlockSpec((tm, d), lambda i: (i, 0)),
                  pl.BlockSpec((1, d), lambda i: (0, 0)),
                  pl.BlockSpec((None, mod.shape[1], d), lambda i: (row_fn(i), 0, 0)),
                  pl.BlockSpec((d, nr), lambda i: (0, 0)),
                  pl.BlockSpec((1, nr), lambda i: (0, 0))],
        out_specs=[pl.BlockSpec((tm, d), lambda i: (i, 0)),
                   pl.BlockSpec((tm, nr), lambda i: (i, 0)),
                   pl.BlockSpec((tm, nr), lambda i: (i, 0))],
        out_shape=[jax.ShapeDtypeStruct((t, d), BF16), jax.ShapeDtypeStruct((t, nr), F32),
                   jax.ShapeDtypeStruct((t, nr), jnp.int32)],
        compiler_params=_params("parallel"),
        name="ffn_norm_router",
    )(x, g.reshape(1, d), mod, w_router, b_router)


def _attn_kernel(q_ref, k_ref, v_ref, o_ref, s_ref, mx_ref, l_ref, acc_ref, *, group, dq, dv, tk):
    tq = q_ref.shape[0]
    nk = k_ref.shape[0] // tk
    q = jnp.concatenate([q_ref[:, g * dq:(g + 1) * dq] for g in range(group)], axis=0)
    slabs = tk // LANE

    def for_chunks(body):
        main = nk // ATTN_UNROLL

        def group_body(i, carry):
            for u in range(ATTN_UNROLL):
                body(i * ATTN_UNROLL + u)
            return carry
        lax.fori_loop(0, main, group_body, 0)
        for c in range(main * ATTN_UNROLL, nk):
            body(c)

    mx_ref[...] = jnp.full(mx_ref.shape, -jnp.inf, F32)

    def rows_of(c):
        return pl.ds(c * tk if isinstance(c, int) else pl.multiple_of(c * tk, tk), tk)

    def scores(c):
        s = lax.dot_general(q, k_ref[rows_of(c), :], (((1,), (1,)), ((), ())),
                            preferred_element_type=F32)
        s_ref[c] = s
        mx = mx_ref[...]
        for j in range(slabs):
            mx = jnp.maximum(mx, s[:, j * LANE:(j + 1) * LANE])
        mx_ref[...] = mx

    for_chunks(scores)
    m = jnp.broadcast_to(jnp.max(mx_ref[...], axis=-1, keepdims=True), mx_ref.shape)
    l_ref[...] = jnp.zeros_like(l_ref)
    acc_ref[...] = jnp.zeros_like(acc_ref)

    def weighted(c):
        s = s_ref[c]
        p = jnp.concatenate([jnp.exp2(s[:, j * LANE:(j + 1) * LANE] - m) for j in range(slabs)], axis=1)
        lsum = l_ref[...]
        for j in range(slabs):
            lsum = lsum + p[:, j * LANE:(j + 1) * LANE]
        l_ref[...] = lsum
        acc_ref[...] += jnp.dot(p.astype(BF16), v_ref[rows_of(c), :], preferred_element_type=F32)

    for_chunks(weighted)
    o = acc_ref[...] / jnp.sum(l_ref[...], axis=-1, keepdims=True)
    for g in range(group):
        o_ref[:, g * dv:(g + 1) * dv] = o[g * tq:(g + 1) * tq].astype(o_ref.dtype)


def attention(q, k, v, *, batch, lq, lk, kv_heads, group, dq, dv, name):
    tq = _pick(lq, (512, 256, 128)) // (2 if group > 1 else 1)
    tk = _pick(lk, (256, 128))
    nq = lq // tq
    m_rows = group * tq
    return pl.pallas_call(
        functools.partial(_attn_kernel, group=group, dq=dq, dv=dv, tk=tk),
        grid=(batch, kv_heads, nq),
        in_specs=[pl.BlockSpec((tq, group * dq), lambda b, h, i: (b * nq + i, h)),
                  pl.BlockSpec((lk, dq), lambda b, h, i: (b, h)),
                  pl.BlockSpec((lk, dv), lambda b, h, i: (b, h))],
        out_specs=pl.BlockSpec((tq, group * dv), lambda b, h, i: (b * nq + i, h)),
        out_shape=jax.ShapeDtypeStruct((batch * lq, kv_heads * group * dv), BF16),
        scratch_shapes=[pltpu.VMEM((lk // tk, m_rows, tk), F32), pltpu.VMEM((m_rows, LANE), F32),
                        pltpu.VMEM((m_rows, LANE), F32), pltpu.VMEM((m_rows, dv), F32)],
        compiler_params=_params("parallel", "parallel", "parallel"),
        name=name,
    )(q, k, v)


def _split_hi_lo(a):
    hi = a.astype(BF16).astype(F32)
    return hi, (a - hi).astype(BF16).astype(F32)


def _dot_split(a, b):
    ah, al = _split_hi_lo(a)
    bh, bl = _split_hi_lo(b)
    lhs = jnp.concatenate([ah, al, ah, al], axis=2).astype(BF16)
    rhs = jnp.concatenate([bh, bh, bl, bl], axis=1).astype(BF16)
    return lax.dot_general(lhs, rhs, (((2,), (1,)), ((0,), (0,))), preferred_element_type=F32)


def _dot_nt(a, b):
    return lax.dot_general(a.astype(BF16), b.astype(BF16), (((2,), (2,)), ((0,), (0,))), preferred_element_type=F32)


def _chunk_masks(backward):
    r = lax.broadcasted_iota(jnp.int32, (GDN_CHUNK, GDN_CHUNK), 0)
    c = lax.broadcasted_iota(jnp.int32, (GDN_CHUNK, GDN_CHUNK), 1)
    incl = (r <= c) if backward else (r >= c)
    strict = (r < c) if backward else (r > c)
    return r, c, incl, strict


def _decay_mask(dcol, r, c, incl):
    drow = jnp.sum(jnp.where(r == c, dcol, 0.0), axis=1, keepdims=True)
    return jnp.where(incl, jnp.exp(jnp.where(incl, dcol - drow, 0.0)), 0.0)


def _unit_tri_inverse(a_mat, r, c):
    eye = (r == c).astype(F32)
    same16 = jnp.right_shift(r, 4) == jnp.right_shift(c, 4)
    same32 = jnp.right_shift(r, 5) == jnp.right_shift(c, 5)
    x = jnp.where(same16, -a_mat, 0.0)
    inv = eye + x
    for _ in range(3):
        x = _dot_split(x, x)
        inv = inv + _dot_split(inv, x)
    for off_blocks in (jnp.logical_and(same32, jnp.logical_not(same16)), jnp.logical_not(same32)):
        off = jnp.where(off_blocks, a_mat, 0.0)
        inv = inv - _dot_split(inv, _dot_split(off, inv))
    return inv


def _head_column(ref, lane, index, n_c):
    col = jnp.sum(jnp.where(lane == index, ref[...], 0.0), axis=1, keepdims=True)
    return col.reshape(n_c, GDN_CHUNK, 1)


def _gdn_solve_kernel(k_ref, v_ref, d_ref, b_ref, uw_ref, *, head_lanes):
    h = pl.program_id(1)
    n_c = k_ref.shape[0] // GDN_CHUNK
    lane = lax.broadcasted_iota(jnp.int32, (1, d_ref.shape[1]), 1)
    kc = k_ref[...].reshape(n_c, GDN_CHUNK, GDN_DK)
    vc = v_ref[...].reshape(n_c, GDN_CHUNK, GDN_DV)
    for direction in range(2):
        r, c, incl, strict = _chunk_masks(direction == 1)
        dcol = _head_column(d_ref, lane, direction * head_lanes + h, n_c)
        bcol = _head_column(b_ref, lane, direction * head_lanes + h, n_c)
        kb = kc * bcol
        a_mat = jnp.where(strict, _dot_nt(kb, kc) * _decay_mask(dcol, r, c, incl), 0.0)
        rhs = jnp.concatenate([vc * bcol, kb * jnp.exp(dcol)], axis=2)
        uw = _dot_split(_unit_tri_inverse(a_mat, r, c), rhs)
        uw_ref[direction] = uw.reshape(n_c * GDN_CHUNK, GDN_DV + GDN_DK)


def gdn_solve(qkv, dcs, beta, *, heads):
    t = qkv.shape[0]
    rows = _pick(t, (512, 256, 128, 64))
    width = GDN_DV + GDN_DK
    return pl.pallas_call(
        functools.partial(_gdn_solve_kernel, head_lanes=heads),
        grid=(t // rows, heads),
        in_specs=[pl.BlockSpec((rows, GDN_DK), lambda i, h: (i, heads + h)),
                  pl.BlockSpec((rows, GDN_DV), lambda i, h: (i, 2 * heads + h)),
                  pl.BlockSpec((rows, 2 * heads), lambda i, h: (i, 0)),
                  pl.BlockSpec((rows, 2 * heads), lambda i, h: (i, 0))],
        out_specs=pl.BlockSpec((2, rows, width), lambda i, h: (0, i, h)),
        out_shape=jax.ShapeDtypeStruct((2, t, heads * width), F32),
        compiler_params=_params("parallel", "parallel"),
        name="gdn_solve",
    )(qkv, qkv, dcs, beta)


def _gdn_scan_kernel(qf_ref, kf_ref, uwf_ref, df_ref, qb_ref, kb_ref, uwb_ref, db_ref, s0_ref,
                     of_ref, ob_ref, sfin_ref, s_ref, *, heads_per_step, head_lanes):
    i = pl.program_id(2)
    hg = pl.program_id(1)
    rows = qf_ref.shape[0]
    n_c = rows // GDN_CHUNK
    lane = lax.broadcasted_iota(jnp.int32, (1, df_ref.shape[1]), 1)

    @pl.when(i == 0)
    def _():
        s_ref[...] = s0_ref[...].astype(F32)

    def bmm(a, b):
        return lax.dot_general(a, b, (((2,), (1,)), ((0,), (0,))), preferred_element_type=F32)

    width = GDN_DV + GDN_DK
    streams = ((qf_ref, kf_ref, uwf_ref, df_ref, of_ref), (qb_ref, kb_ref, uwb_ref, db_ref, ob_ref))
    for ci in range(n_c):
        for direction, (q_ref, k_ref, uw_ref, d_ref, o_ref) in enumerate(streams):
            backward = direction == 1
            r, c, incl, _ = _chunk_masks(backward)
            pos = (n_c - 1 - ci) if backward else ci
            sl = slice(pos * GDN_CHUNK, (pos + 1) * GDN_CHUNK)
            heads = range(heads_per_step)
            dcol = jnp.stack([jnp.sum(jnp.where(lane == direction * head_lanes + hg * heads_per_step + hh,
                                                d_ref[sl, :], 0.0), axis=1, keepdims=True) for hh in heads])
            qc = jnp.stack([q_ref[sl, hh * GDN_DK:(hh + 1) * GDN_DK] for hh in heads])
            kc = jnp.stack([k_ref[sl, hh * GDN_DK:(hh + 1) * GDN_DK] for hh in heads])
            u = jnp.stack([uw_ref[sl, hh * width:hh * width + GDN_DV] for hh in heads])
            w = jnp.stack([uw_ref[sl, hh * width + GDN_DV:(hh + 1) * width] for hh in heads])
            attn = _dot_nt(qc, kc) * _decay_mask(dcol, r, c, incl)
            d_last = dcol[:, 0:1, :] if backward else dcol[:, GDN_CHUNK - 1:GDN_CHUNK, :]
            q_dec = (qc * jnp.exp(dcol)).astype(BF16)
            k_dec = (kc * jnp.exp(d_last - dcol)).astype(BF16)
            s = s_ref[direction]
            s_b = s.astype(BF16)
            v_new = u - bmm(w.astype(BF16), s_b)
            v_b = v_new.astype(BF16)
            o = bmm(q_dec, s_b) + bmm(attn.astype(BF16), v_b)
            for hh in heads:
                o_ref[sl, hh * GDN_DV:(hh + 1) * GDN_DV] = o[hh]
            s_ref[direction] = s * jnp.exp(d_last) + lax.dot_general(
                k_dec, v_b, (((1,), (1,)), ((0,), (0,))), preferred_element_type=F32)

    @pl.when(i == pl.num_programs(2) - 1)
    def _():
        sfin_ref[...] = s_ref[...]


def gdn_scan(qkv, uw, dcs, s0, *, row_offset, batch, length, heads):
    rows = _pick(length, (256, 128, 64))
    hps = _pick(heads, (4, 2, 1))
    n_l = length // rows
    hgroups = heads // hps
    width = GDN_DV + GDN_DK
    assert row_offset % rows == 0
    first = row_offset // rows

    def fwd(b, g, i):
        return b * n_l + i

    def bwd(b, g, i):
        return b * n_l + (n_l - 1 - i)

    def stream_specs(pos, direction):
        return [pl.BlockSpec((rows, hps * GDN_DK), lambda b, g, i: (first + pos(b, g, i), g)),
                pl.BlockSpec((rows, hps * GDN_DK), lambda b, g, i: (first + pos(b, g, i), hgroups + g)),
                pl.BlockSpec((None, rows, hps * width), lambda b, g, i: (direction, first + pos(b, g, i), g)),
                pl.BlockSpec((rows, 2 * heads), lambda b, g, i: (first + pos(b, g, i), 0))]

    state_spec = pl.BlockSpec((None, 2, hps, GDN_DK, GDN_DV), lambda b, g, i: (b, 0, g, 0, 0))
    return pl.pallas_call(
        functools.partial(_gdn_scan_kernel, heads_per_step=hps, head_lanes=heads),
        grid=(batch, hgroups, n_l),
        in_specs=stream_specs(fwd, 0) + stream_specs(bwd, 1) + [state_spec],
        out_specs=[pl.BlockSpec((rows, hps * GDN_DV), lambda b, g, i: (fwd(b, g, i), g)),
                   pl.BlockSpec((rows, hps * GDN_DV), lambda b, g, i: (bwd(b, g, i), g)),
                   state_spec],
        out_shape=[jax.ShapeDtypeStruct((batch * length, heads * GDN_DV), F32),
                   jax.ShapeDtypeStruct((batch * length, heads * GDN_DV), F32),
                   jax.ShapeDtypeStruct((batch, 2, heads, GDN_DK, GDN_DV), F32)],
        scratch_shapes=[pltpu.VMEM((2, hps, GDN_DK, GDN_DV), F32)],
        compiler_params=_params("parallel", "parallel", "arbitrary"),
        name="gdn_scan",
    )(qkv, qkv, uw, dcs, qkv, qkv, uw, dcs, s0)


def _moe_kernel(be_ref, nu_ref, x_ref, wg_ref, wu_ref, wd_ref, o_ref, acc_ref):
    i = pl.program_id(0)
    f = pl.program_id(1)

    @pl.when(i < nu_ref[0])
    def _():
        @pl.when(f == 0)
        def _():
            acc_ref[...] = jnp.zeros_like(acc_ref)
        x = x_ref[...]
        gate = jnp.dot(x, wg_ref[...].astype(BF16), preferred_element_type=F32)
        up = jnp.dot(x, wu_ref[...].astype(BF16), preferred_element_type=F32)
        hmid = (gate * jax.nn.sigmoid(gate) * up).astype(BF16)
        acc_ref[...] += jnp.dot(hmid, wd_ref[...].astype(BF16), preferred_element_type=F32)

        @pl.when(f == pl.num_programs(1) - 1)
        def _():
            o_ref[...] = acc_ref[...].astype(o_ref.dtype)

    @pl.when(jnp.logical_and(i >= nu_ref[0], f == 0))
    def _():
        o_ref[...] = jnp.zeros_like(o_ref)


def moe_experts(xb, block_e, n_used, w_gate, w_up, w_down, *, tm, layer):
    p, d = xb.shape
    f_dim = w_gate.shape[3]
    tf = _pick(f_dim, (256, 128))
    n_blocks = p // tm
    last_f = f_dim // tf - 1

    def blk(i, nu):
        return jnp.minimum(i, nu[0] - 1)

    def f_blk(i, f, nu):
        return jnp.where(i < nu[0], f, last_f)

    grid_spec = pltpu.PrefetchScalarGridSpec(
        num_scalar_prefetch=2,
        grid=(n_blocks, f_dim // tf),
        in_specs=[pl.BlockSpec((tm, d), lambda i, f, be, nu: (blk(i, nu), 0)),
                  pl.BlockSpec((None, None, d, tf), lambda i, f, be, nu: (layer, be[blk(i, nu)], 0, f_blk(i, f, nu))),
                  pl.BlockSpec((None, None, d, tf), lambda i, f, be, nu: (layer, be[blk(i, nu)], 0, f_blk(i, f, nu))),
                  pl.BlockSpec((None, None, tf, d), lambda i, f, be, nu: (layer, be[blk(i, nu)], f_blk(i, f, nu), 0))],
        out_specs=pl.BlockSpec((tm, d), lambda i, f, be, nu: (i, 0)),
        scratch_shapes=[pltpu.VMEM((tm, d), F32)],
    )
    return pl.pallas_call(
        _moe_kernel,
        grid_spec=grid_spec,
        out_shape=jax.ShapeDtypeStruct((p, d), BF16),
        compiler_params=_params("arbitrary", "arbitrary"),
        name="moe_experts",
    )(block_e, n_used, xb, w_gate, w_up, w_down)


def _rms(x, g):
    return x * lax.rsqrt(jnp.mean(x * x, axis=-1, keepdims=True) + EPS) * g


def _rope_tables(t_prompt, dec_batch, dec_seq, width):
    a = width // 2
    half = a // 2
    freqs = ROPE_THETA ** (-np.arange(half, dtype=np.float32) / half)
    t = np.arange(dec_seq)
    row, col = t // GRID_W, t % GRID_W
    ang_r = row[:, None].astype(np.float32) * freqs[None, :]
    ang_c = col[:, None].astype(np.float32) * freqs[None, :]
    cos = np.concatenate([np.cos(ang_r), np.cos(ang_r), np.cos(ang_c), np.cos(ang_c)], axis=-1)
    sin = np.concatenate([-np.sin(ang_r), np.sin(ang_r), -np.sin(ang_c), np.sin(ang_c)], axis=-1)
    cos = np.concatenate([np.ones((t_prompt, width), np.float32), np.tile(cos, (dec_batch, 1))], axis=0)
    sin = np.concatenate([np.zeros((t_prompt, width), np.float32), np.tile(sin, (dec_batch, 1))], axis=0)
    return jnp.asarray(cos), jnp.asarray(sin)


def _rope_apply(x, cos, sin):
    half = x.shape[-1] // 4
    xs = x.reshape(x.shape[:-1] + (2, 2, half))
    swapped = jnp.flip(xs, axis=-2).reshape(x.shape)
    return x * cos[:, None, :] + swapped * sin[:, None, :]


def _l2(x):
    return x * lax.rsqrt(jnp.sum(x * x, axis=-1, keepdims=True) + EPS)


def _gdn_inputs(qkv, a, b, w_conv, a_log, dt_bias, heads, seq_start, seq_end):
    t = qkv.shape[0]
    prev = jnp.where(seq_start, 0.0, jnp.roll(qkv, 1, axis=0))
    nxt = jnp.where(seq_end, 0.0, jnp.roll(qkv, -1, axis=0))
    act = jax.nn.silu(prev * w_conv[0] + qkv * w_conv[1] + nxt * w_conv[2])
    nk = heads * GDN_DK
    q = _l2(act[:, :nk].reshape(t, heads, GDN_DK)) * GDN_DK ** -0.5
    k = _l2(act[:, nk:2 * nk].reshape(t, heads, GDN_DK))
    qkv_c = jnp.concatenate([q.reshape(t, nk), k.reshape(t, nk), act[:, 2 * nk:]], axis=1)
    g = -jnp.exp(a_log.reshape(2 * heads)) * jax.nn.softplus(a + dt_bias.reshape(2 * heads))
    gc = g.reshape(t // GDN_CHUNK, GDN_CHUNK, 2 * heads)
    prefix = jnp.cumsum(gc[..., :heads], axis=1)
    suffix = jnp.flip(jnp.cumsum(jnp.flip(gc[..., heads:], axis=1), axis=1), axis=1)
    dcs = jnp.concatenate([prefix, suffix], axis=-1).reshape(t, 2 * heads)
    return qkv_c, dcs, jax.nn.sigmoid(b)


def _dispatch_plan(experts, n_experts, tm):
    t, k = experts.shape
    a = t * k
    n_blocks = a // tm + n_experts
    flat_e = experts.reshape(-1)
    onehot = (flat_e[:, None] == jnp.arange(n_experts, dtype=jnp.int32)[None, :]).astype(jnp.int32)
    csum = jnp.cumsum(onehot, axis=0)
    pos = jnp.sum((csum - onehot) * onehot, axis=1)
    counts = csum[-1]
    padded = (counts + tm - 1) // tm * tm
    pend = jnp.cumsum(padded)
    pstart = pend - padded
    dest = pstart[flat_e] + pos
    flat_tok = jnp.arange(a, dtype=jnp.int32) // k
    slot_tok = jnp.zeros((n_blocks * tm,), jnp.int32).at[dest].set(flat_tok)
    block_e = jnp.minimum(jnp.searchsorted(pend, jnp.arange(n_blocks, dtype=jnp.int32) * tm, side='right'),
                          n_experts - 1).astype(jnp.int32)
    n_used = (pend[-1] // tm).astype(jnp.int32).reshape(1)
    return dest.reshape(t, k), slot_tok, block_e, n_used


def kernel(x_prompt, x_sample, cache_mla_ckv, cache_mla_krope, cache_gqa_k, cache_gqa_v, state_gdn, c, c_ctx, w_mod, b_mod, g_norm_mix, g_norm_ffn, w_in, g_cq, w_q_up, g_ckv, w_kv_up, g_q_mla, g_k_mla, g_q_gqa, g_k_gqa, w_conv, a_log, dt_bias, g_gdn_out, w_out, w_router_group, b_router_group, w_router_expert, b_router_expert, w_gate, w_up, w_down):
    batch, seq, d = x_prompt.shape
    dec_batch, dec_seq, _ = x_sample.shape
    depth = w_in.shape[0]
    past = cache_mla_ckv.shape[2]
    q_lora = w_q_up.shape[1]
    kv_lora = w_kv_up.shape[1]
    mla_heads = w_q_up.shape[2] // MLA_QK
    kv_heads = cache_gqa_k.shape[3]
    gdn_heads = state_gdn.shape[3]
    gqa_heads = (w_out.shape[1] - mla_heads * MLA_V - gdn_heads * GDN_DV) // HEAD_DIM
    group = gqa_heads // kv_heads
    n_groups = w_router_group.shape[2]
    n_experts = w_gate.shape[1]
    gdn_qkv = gdn_heads * (2 * GDN_DK + GDN_DV)
    gdn_width = gdn_heads * GDN_DV

    t_p = batch * seq
    t_s = dec_batch * dec_seq
    t_all = t_p + t_s
    tm = _pick(math.gcd(t_p, dec_seq), (512, 256, 128))
    row_fn = _mod_row_fn(tm, t_p, dec_seq)
    moe_tm = _pick(t_all * TOP_K, (512, 256, 128))

    sizes = (q_lora, kv_lora, MLA_ROPE, gqa_heads * HEAD_DIM, kv_heads * HEAD_DIM, kv_heads * HEAD_DIM,
             gdn_qkv, gdn_width, 2 * gdn_heads, 2 * gdn_heads)
    off = np.concatenate([[0], np.cumsum(sizes)])
    order = (0, 1, 3, 4, 5, 6, 7, 2, 8, 9)
    new_sizes = [sizes[i] for i in order]
    n_in = int(off[-1])
    n_in_pad = -(-n_in // LANE) * LANE
    noff = np.concatenate([[0], np.cumsum(new_sizes)])
    seg = {name: (int(noff[j]), int(noff[j + 1])) for j, name in
           enumerate(('cq', 'ckv', 'gq', 'gk', 'gv', 'qkv', 'z', 'krope', 'a', 'b'))}
    for name in ('cq', 'ckv', 'gq', 'gk', 'gv', 'qkv', 'z', 'krope'):
        assert seg[name][0] % LANE == 0, name

    row_ids = np.arange(t_all)
    local = np.where(row_ids < t_p, row_ids % seq, (row_ids - t_p) % dec_seq)
    seq_start = jnp.asarray((local == 0)[:, None])
    seq_end = jnp.asarray((local == np.where(row_ids < t_p, seq, dec_seq) - 1)[:, None])

    cos_g, sin_g = _rope_tables(t_p, dec_batch, dec_seq, HEAD_DIM)
    cos_m, sin_m = _rope_tables(t_p, dec_batch, dec_seq, MLA_ROPE)

    cond8 = jnp.zeros((8, d), F32).at[0].set(c_ctx).at[1:1 + dec_batch].set(c)

    x = jnp.concatenate([x_prompt.reshape(t_p, d), x_sample.reshape(t_s, d)], axis=0)
    new_ckv, new_krope, new_k, new_v, new_s = [], [], [], [], []
    q_scale_m = MLA_QK ** -0.5 * LOG2E
    q_scale_g = HEAD_DIM ** -0.5 * LOG2E

    for l in range(depth):
        mod = modulation(cond8, w_mod, b_mod, l).reshape(8, 6, d)

        w_in_r = jnp.concatenate([w_in[l][:, off[i]:off[i + 1]] for i in order]
                                 + [jnp.zeros((d, n_in_pad - n_in), F32)], axis=1).astype(BF16)
        u = norm_matmul(x, g_norm_mix[l], mod, w_in_r, sh_idx=0, sc_idx=1, tm=tm, row_fn=row_fn)

        def take(name):
            return u[:, seg[name][0]:seg[name][1]]

        ckv_n = _rms(take('ckv'), g_ckv[l])
        krope = take('krope')
        cq_n = _rms(take('cq'), g_cq[l]).astype(BF16)
        wq = w_q_up[l].reshape(q_lora, mla_heads, MLA_QK)
        wq = jnp.pad(wq, ((0, 0), (0, 0), (0, MLA_PAD - MLA_QK))).reshape(q_lora, mla_heads * MLA_PAD).astype(BF16)
        q_m = matmul(cq_n, wq, name="mla_q_up").reshape(t_all, mla_heads, MLA_PAD)
        g_q_pad = jnp.pad(g_q_mla[l], (0, MLA_PAD - MLA_QK))
        q_m = q_m * lax.rsqrt(jnp.sum(q_m * q_m, axis=-1, keepdims=True) / MLA_QK + EPS) * g_q_pad
        q_rot = _rope_apply(q_m[..., MLA_NOPE:MLA_QK], cos_m, sin_m)
        q_m = jnp.concatenate([q_m[..., :MLA_NOPE], q_rot, q_m[..., MLA_QK:]], axis=-1)
        q_m = (q_m * q_scale_m).astype(BF16).reshape(t_all, mla_heads * MLA_PAD)

        wkv = w_kv_up[l].reshape(kv_lora, mla_heads, MLA_NOPE + MLA_V)
        wkv = jnp.concatenate([wkv[..., :MLA_NOPE].reshape(kv_lora, -1), wkv[..., MLA_NOPE:].reshape(kv_lora, -1)],
                              axis=1).astype(BF16)
        ckv_all = jnp.concatenate([ckv_n, cache_mla_ckv[:, l].reshape(dec_batch * past, kv_lora)], axis=0)
        krope_all = jnp.concatenate([krope, cache_mla_krope[:, l].reshape(dec_batch * past, MLA_ROPE)], axis=0)
        cos_all = jnp.concatenate([cos_m, jnp.ones((dec_batch * past, MLA_ROPE), F32)], axis=0)
        sin_all = jnp.concatenate([sin_m, jnp.zeros((dec_batch * past, MLA_ROPE), F32)], axis=0)
        kv = matmul(ckv_all, wkv, name="mla_kv_up")
        n_all = kv.shape[0]
        k_nope = kv[:, :mla_heads * MLA_NOPE].reshape(n_all, mla_heads, MLA_NOPE)
        v_m = kv[:, mla_heads * MLA_NOPE:].astype(BF16)
        ssq = jnp.sum(k_nope * k_nope, axis=-1, keepdims=True) + jnp.sum(krope_all * krope_all, axis=-1)[:, None, None]
        r_k = lax.rsqrt(ssq / MLA_QK + EPS)
        k_rope_h = _rope_apply(krope_all[:, None, :] * r_k * g_k_mla[l][MLA_NOPE:], cos_all, sin_all)
        k_m = jnp.concatenate([k_nope * r_k * g_k_mla[l][:MLA_NOPE], k_rope_h,
                               jnp.zeros((n_all, mla_heads, MLA_PAD - MLA_QK), F32)], axis=-1)
        k_m = k_m.astype(BF16).reshape(n_all, mla_heads * MLA_PAD)

        def with_ctx(lat, ctx):
            lat = lat[t_p:t_all].reshape(dec_batch, dec_seq, -1)
            ctx = ctx.reshape(dec_batch, past, -1)
            return jnp.concatenate([ctx, lat], axis=1).reshape(dec_batch * (past + dec_seq), -1)

        o_m_p = attention(q_m[:t_p], k_m[:t_p], v_m[:t_p], batch=batch, lq=seq, lk=seq, kv_heads=mla_heads,
                          group=1, dq=MLA_PAD, dv=MLA_V, name="mla_attn_ctx")
        o_m_s = attention(q_m[t_p:], with_ctx(k_m, k_m[t_all:]), with_ctx(v_m, v_m[t_all:]), batch=dec_batch,
                          lq=dec_seq, lk=past + dec_seq, kv_heads=mla_heads, group=1, dq=MLA_PAD, dv=MLA_V,
                          name="mla_attn_lat")
        o_m = jnp.concatenate([o_m_p, o_m_s], axis=0)

        q_g = _rms(take('gq').reshape(t_all, gqa_heads, HEAD_DIM), g_q_gqa[l])
        q_g = (_rope_apply(q_g, cos_g, sin_g) * q_scale_g).astype(BF16).reshape(t_all, gqa_heads * HEAD_DIM)
        k_g = _rms(take('gk').reshape(t_all, kv_heads, HEAD_DIM), g_k_gqa[l])
        v_g = take('gv')
        k_g_rot = _rope_apply(k_g, cos_g, sin_g).astype(BF16).reshape(t_all, kv_heads * HEAD_DIM)
        v_g_b = v_g.astype(BF16)
        ctx_k = cache_gqa_k[:, l].reshape(dec_batch * past, kv_heads * HEAD_DIM).astype(BF16)
        ctx_v = cache_gqa_v[:, l].reshape(dec_batch * past, kv_heads * HEAD_DIM).astype(BF16)
        o_g_p = attention(q_g[:t_p], k_g_rot[:t_p], v_g_b[:t_p], batch=batch, lq=seq, lk=seq, kv_heads=kv_heads,
                          group=group, dq=HEAD_DIM, dv=HEAD_DIM, name="gqa_attn_ctx")
        o_g_s = attention(q_g[t_p:], with_ctx(k_g_rot, ctx_k), with_ctx(v_g_b, ctx_v), batch=dec_batch, lq=dec_seq,
                          lk=past + dec_seq, kv_heads=kv_heads, group=group, dq=HEAD_DIM, dv=HEAD_DIM,
                          name="gqa_attn_lat")
        o_g = jnp.concatenate([o_g_p, o_g_s], axis=0)

        qkv_c, dcs, beta = _gdn_inputs(take('qkv'), take('a'), take('b'), w_conv[l], a_log[l], dt_bias[l],
                                       gdn_heads, seq_start, seq_end)
        uw = gdn_solve(qkv_c, dcs, beta, heads=gdn_heads)
        s0_p = jnp.zeros((batch, 2, gdn_heads, GDN_DK, GDN_DV), F32)
        of_p, ob_p, s_ctx = gdn_scan(qkv_c, uw, dcs, s0_p, row_offset=0, batch=batch, length=seq, heads=gdn_heads)
        of_s, ob_s, _ = gdn_scan(qkv_c, uw, dcs, state_gdn[:, l], row_offset=t_p, batch=dec_batch,
                                 length=dec_seq, heads=gdn_heads)
        od = jnp.concatenate([of_p + ob_p, of_s + ob_s], axis=0).reshape(t_all, gdn_heads, GDN_DV)
        o_d = (_rms(od, g_gdn_out[l]) * jax.nn.silu(take('z').reshape(t_all, gdn_heads, GDN_DV)))
        o_d = o_d.reshape(t_all, gdn_width).astype(BF16)

        new_ckv.append(ckv_n[:t_p].reshape(batch, seq, kv_lora))
        new_krope.append(krope[:t_p].reshape(batch, seq, MLA_ROPE))
        new_k.append(k_g[:t_p].reshape(batch, seq, kv_heads, HEAD_DIM))
        new_v.append(v_g[:t_p].reshape(batch, seq, kv_heads, HEAD_DIM))
        new_s.append(s_ctx)

        x = out_proj_residual(o_m, o_g, o_d, w_out[l].astype(BF16), x, mod, gate_idx=2, tm=tm, row_fn=row_fn)

        w_router = jnp.concatenate([w_router_group[l], w_router_expert[l],
                                    jnp.zeros((d, LANE - n_groups - n_experts), F32)], axis=1)
        b_router = jnp.concatenate([b_router_group[l], b_router_expert[l],
                                    jnp.zeros((LANE - n_groups - n_experts,), F32)]).reshape(1, LANE)
        h2, gates, experts = ffn_norm_router(x, g_norm_ffn[l], mod, w_router, b_router, sh_idx=3, sc_idx=4, tm=tm,
                                             row_fn=row_fn, n_groups=n_groups, epg=n_experts // n_groups)
        gates, experts = gates[:, :TOP_K], experts[:, :TOP_K]
        dest, slot_tok, block_e, n_used = _dispatch_plan(experts, n_experts, moe_tm)
        yb = moe_experts(h2[slot_tok], block_e, n_used, w_gate, w_up, w_down, tm=moe_tm, layer=l)
        y = jnp.sum(yb[dest].astype(F32) * gates[:, :, None], axis=1)
        gate2 = jnp.concatenate([jnp.broadcast_to(mod[0, 5], (t_p, d))]
                                + [jnp.broadcast_to(mod[1 + bi, 5], (dec_seq, d)) for bi in range(dec_batch)], axis=0)
        x = x + gate2 * y

    return (x[:t_p].reshape(batch, seq, d), x[t_p:].reshape(dec_batch, dec_seq, d),
            jnp.stack(new_ckv, axis=1), jnp.stack(new_krope, axis=1), jnp.stack(new_k, axis=1),
            jnp.stack(new_v, axis=1), jnp.stack(new_s, axis=1))
```

```python
import functools
import math

import numpy as np
import jax
import jax.numpy as jnp
from jax import lax
from jax.experimental import pallas as pl
from jax.experimental.pallas import tpu as pltpu

F32 = jnp.float32
BF16 = jnp.bfloat16

EPS = 1e-6
ROPE_THETA = 10000.0
GRID_W = 64
HEAD_DIM = 128
MLA_NOPE = 128
MLA_ROPE = 64
MLA_V = 128
MLA_QK = MLA_NOPE + MLA_ROPE
MLA_PAD = 256
GDN_DK = 128
GDN_DV = 128
GDN_CHUNK = 64
TOP_K = 2
LANE = 128
VMEM_LIMIT_BYTES = 56 * 1024 * 1024
LOG2E = 1.4426950408889634
ATTN_UNROLL = 4
GATHER_UNROLL = 8


def _params(*sem):
    return pltpu.CompilerParams(dimension_semantics=sem, vmem_limit_bytes=VMEM_LIMIT_BYTES)


def _pick(n, prefs):
    for p in prefs:
        if n % p == 0:
            return p
    return n


def _mod_kernel(c_ref, w_ref, b_ref, o_ref):
    c = c_ref[...]
    s = (c * jax.nn.sigmoid(c)).astype(BF16)
    o_ref[...] = jnp.dot(s, w_ref[...].astype(BF16), preferred_element_type=F32) + b_ref[...]


def modulation(cond8, w_mod, b_mod, layer):
    rows, d = cond8.shape
    depth, _, n = w_mod.shape
    tn = _pick(n, (512, 256, 128))
    return pl.pallas_call(
        _mod_kernel,
        grid=(n // tn,),
        in_specs=[pl.BlockSpec((rows, d), lambda j: (0, 0)),
                  pl.BlockSpec((None, d, tn), lambda j: (layer, 0, j)),
                  pl.BlockSpec((None, 1, tn), lambda j: (layer, 0, j))],
        out_specs=pl.BlockSpec((rows, tn), lambda j: (0, j)),
        out_shape=jax.ShapeDtypeStruct((rows, n), F32),
        compiler_params=_params("parallel"),
        name="modulation",
    )(cond8, w_mod, b_mod.reshape(depth, 1, n))


def _mod_row_fn(tm, t_prompt, dec_seq):
    n_p = t_prompt // tm
    per_b = dec_seq // tm

    def row(i):
        return jnp.where(i < n_p, 0, 1 + (i - n_p) // per_b)
    return row


def _modulated_norm(x, g, sc, sh):
    ms = jnp.mean(x * x, axis=-1, keepdims=True)
    return (x * lax.rsqrt(ms + EPS) * g) * (1.0 + sc) + sh


def _norm_matmul_kernel(x_ref, g_ref, mod_ref, w_ref, o_ref, h_ref, *, sh_idx, sc_idx):
    @pl.when(pl.program_id(1) == 0)
    def _():
        h = _modulated_norm(x_ref[...], g_ref[...], mod_ref[sc_idx:sc_idx + 1, :],
                            mod_ref[sh_idx:sh_idx + 1, :])
        h_ref[...] = h.astype(BF16)
    o_ref[...] = jnp.dot(h_ref[...], w_ref[...], preferred_element_type=F32).astype(o_ref.dtype)


def norm_matmul(x, g, mod, w, *, sh_idx, sc_idx, tm, row_fn, out_dtype=F32):
    t, d = x.shape
    n = w.shape[1]
    tn = _pick(n, (1152, 1024, 512, 384, 256, 128))
    return pl.pallas_call(
        functools.partial(_norm_matmul_kernel, sh_idx=sh_idx, sc_idx=sc_idx),
        grid=(t // tm, n // tn),
        in_specs=[pl.BlockSpec((tm, d), lambda i, j: (i, 0)),
                  pl.BlockSpec((1, d), lambda i, j: (0, 0)),
                  pl.BlockSpec((None, mod.shape[1], d), lambda i, j: (row_fn(i), 0, 0)),
                  pl.BlockSpec((d, tn), lambda i, j: (0, j))],
        out_specs=pl.BlockSpec((tm, tn), lambda i, j: (i, j)),
        out_shape=jax.ShapeDtypeStruct((t, n), out_dtype),
        scratch_shapes=[pltpu.VMEM((tm, d), BF16)],
        compiler_params=_params("parallel", "arbitrary"),
        name="norm_in_proj",
    )(x, g.reshape(1, d), mod, w)


def _matmul_kernel(x_ref, w_ref, o_ref):
    o_ref[...] = jnp.dot(x_ref[...].astype(BF16), w_ref[...],
                         preferred_element_type=F32).astype(o_ref.dtype)


def matmul(x, w, *, out_dtype=F32, name="matmul"):
    t, k = x.shape
    n = w.shape[1]
    tm = _pick(t, (512, 256, 128))
    tn = _pick(n, (512, 384, 256, 128))
    return pl.pallas_call(
        _matmul_kernel,
        grid=(t // tm, n // tn),
        in_specs=[pl.BlockSpec((tm, k), lambda i, j: (i, 0)),
                  pl.BlockSpec((k, tn), lambda i, j: (0, j))],
        out_specs=pl.BlockSpec((tm, tn), lambda i, j: (i, j)),
        out_shape=jax.ShapeDtypeStruct((t, n), out_dtype),
        compiler_params=_params("parallel", "parallel"),
        name=name,
    )(x, w)


def _out_proj_kernel(om_ref, og_ref, od_ref, wm_ref, wg_ref, wd_ref, x_ref, mod_ref, o_ref, *, gate_idx):
    acc = jnp.dot(om_ref[...], wm_ref[...], preferred_element_type=F32)
    acc += jnp.dot(og_ref[...], wg_ref[...], preferred_element_type=F32)
    acc += jnp.dot(od_ref[...], wd_ref[...], preferred_element_type=F32)
    o_ref[...] = x_ref[...] + mod_ref[gate_idx:gate_idx + 1, :] * acc


def out_proj_residual(o_m, o_g, o_d, w_out, x, mod, *, gate_idx, tm, row_fn):
    t, d = x.shape
    km, kg, kd = o_m.shape[1], o_g.shape[1], o_d.shape[1]
    assert km == kg and (km + kg) % kd == 0
    tn = _pick(d, (1024, 512, 256, 128))
    return pl.pallas_call(
        functools.partial(_out_proj_kernel, gate_idx=gate_idx),
        grid=(t // tm, d // tn),
        in_specs=[pl.BlockSpec((tm, km), lambda i, j: (i, 0)),
                  pl.BlockSpec((tm, kg), lambda i, j: (i, 0)),
                  pl.BlockSpec((tm, kd), lambda i, j: (i, 0)),
                  pl.BlockSpec((km, tn), lambda i, j: (0, j)),
                  pl.BlockSpec((kg, tn), lambda i, j: (1, j)),
                  pl.BlockSpec((kd, tn), lambda i, j: ((km + kg) // kd, j)),
                  pl.BlockSpec((tm, tn), lambda i, j: (i, j)),
                  pl.BlockSpec((None, mod.shape[1], tn), lambda i, j: (row_fn(i), 0, j))],
        out_specs=pl.BlockSpec((tm, tn), lambda i, j: (i, j)),
        out_shape=jax.ShapeDtypeStruct((t, d), F32),
        compiler_params=_params("parallel", "parallel"),
        name="out_proj",
    )(o_m, o_g, o_d, w_out, w_out, w_out, x, mod)


def _pack_bf16_pair(lo, hi):
    lo_bits = lax.bitcast_convert_type(lo.astype(BF16).astype(F32), jnp.uint32)
    hi_bits = lax.bitcast_convert_type(hi.astype(BF16).astype(F32), jnp.uint32)
    return jnp.right_shift(lo_bits, jnp.uint32(16)) | (hi_bits & jnp.uint32(0xFFFF0000))


def _unpack_bf16_pair(word):
    lo = lax.bitcast_convert_type(jnp.left_shift(word, jnp.uint32(16)), F32)
    hi = lax.bitcast_convert_type(word & jnp.uint32(0xFFFF0000), F32)
    return lo, hi


def _ffn_norm_router_kernel(x_ref, g_ref, mod_ref, wr_ref, br_ref, h_ref, gate_ref, exp_ref, *,
                            sh_idx, sc_idx, n_groups, epg):
    h = _modulated_norm(x_ref[...], g_ref[...], mod_ref[sc_idx:sc_idx + 1, :],
                        mod_ref[sh_idx:sh_idx + 1, :])
    half = h.shape[1] // 2
    h_ref[...] = _pack_bf16_pair(h[:, :half], h[:, half:])
    lg = jnp.dot(h, wr_ref[...], preferred_element_type=F32, precision=lax.Precision.HIGHEST) + br_ref[...]
    lane = lax.broadcasted_iota(jnp.int32, lg.shape, 1).astype(F32)
    neg = jnp.float32(-1e30)
    far = jnp.float32(2 * LANE)

    def first_argmax(v):
        top = jnp.max(v, axis=-1, keepdims=True)
        return top, jnp.min(jnp.where(v == top, lane, far), axis=-1, keepdims=True)

    is_group = lane < n_groups
    g_max, g_idx = first_argmax(jnp.where(is_group, lg, neg))
    g_top = 1.0 / jnp.sum(jnp.where(is_group, jnp.exp(lg - g_max), 0.0), axis=-1, keepdims=True)
    lo = n_groups + g_idx * epg
    el = jnp.where(jnp.logical_and(lane >= lo, lane < lo + epg), lg, neg)
    e1, i1 = first_argmax(el)
    e2, i2 = first_argmax(jnp.where(lane == i1, neg, el))
    r = jnp.exp(e2 - e1)
    gate1 = g_top / (1.0 + r)
    gate2 = g_top * r / (1.0 + r)
    gate_ref[...] = jnp.where(lane == 0, gate1, jnp.where(lane == 1, gate2, 0.0))
    exp_ref[...] = jnp.where(lane == 0, i1 - n_groups, jnp.where(lane == 1, i2 - n_groups, 0.0)).astype(jnp.int32)


def ffn_norm_router(x, g, mod, w_router, b_router, *, sh_idx, sc_idx, tm, row_fn, n_groups, epg):
    t, d = x.shape
    nr = w_router.shape[1]
    assert d % (2 * LANE) == 0
    return pl.pallas_call(
        functools.partial(_ffn_norm_router_kernel, sh_idx=sh_idx, sc_idx=sc_idx, n_groups=n_groups, epg=epg),
        grid=(t // tm,),
        in_specs=[pl.BlockSpec((tm, d), lambda i: (i, 0)),
                  pl.BlockSpec((1, d), lambda i: (0, 0)),
                  pl.BlockSpec((None, mod.shape[1], d), lambda i: (row_fn(i), 0, 0)),
                  pl.BlockSpec((d, nr), lambda i: (0, 0)),
                  pl.BlockSpec((1, nr), lambda i: (0, 0))],
        out_specs=[pl.BlockSpec((tm, d // 2), lambda i: (i, 0)),
                   pl.BlockSpec((tm, nr), lambda i: (i, 0)),
                   pl.BlockSpec((tm, nr), lambda i: (i, 0))],
        out_shape=[jax.ShapeDtypeStruct((t, d // 2), jnp.uint32), jax.ShapeDtypeStruct((t, nr), F32),
                   jax.ShapeDtypeStruct((t, nr), jnp.int32)],
        compiler_params=_params("parallel"),
        name="ffn_norm_router",
    )(x, g.reshape(1, d), mod, w_router, b_router)


def _attn_kernel(q_ref, k_ref, v_ref, o_ref, s_ref, mx_ref, l_ref, acc_ref, *, group, dq, dv, tk):
    tq = q_ref.shape[0]
    nk = k_ref.shape[0] // tk
    q = jnp.concatenate([q_ref[:, g * dq:(g + 1) * dq] for g in range(group)], axis=0)
    slabs = tk // LANE

    def for_chunks(body):
        main = nk // ATTN_UNROLL

        def group_body(i, carry):
            for u in range(ATTN_UNROLL):
                body(i * ATTN_UNROLL + u)
            return carry
        lax.fori_loop(0, main, group_body, 0)
        for c in range(main * ATTN_UNROLL, nk):
            body(c)

    mx_ref[...] = jnp.full(mx_ref.shape, -jnp.inf, F32)

    def rows_of(c):
        return pl.ds(c * tk if isinstance(c, int) else pl.multiple_of(c * tk, tk), tk)

    def scores(c):
        s = lax.dot_general(q, k_ref[rows_of(c), :], (((1,), (1,)), ((), ())),
                            preferred_element_type=F32)
        s_ref[c] = s
        mx = mx_ref[...]
        for j in range(slabs):
            mx = jnp.maximum(mx, s[:, j * LANE:(j + 1) * LANE])
        mx_ref[...] = mx

    for_chunks(scores)
    m = jnp.broadcast_to(jnp.max(mx_ref[...], axis=-1, keepdims=True), mx_ref.shape)
    l_ref[...] = jnp.zeros_like(l_ref)
    acc_ref[...] = jnp.zeros_like(acc_ref)

    def weighted(c):
        s = s_ref[c]
        p = jnp.concatenate([jnp.exp2(s[:, j * LANE:(j + 1) * LANE] - m) for j in range(slabs)], axis=1)
        lsum = l_ref[...]
        for j in range(slabs):
            lsum = lsum + p[:, j * LANE:(j + 1) * LANE]
        l_ref[...] = lsum
        acc_ref[...] += jnp.dot(p.astype(BF16), v_ref[rows_of(c), :], preferred_element_type=F32)

    for_chunks(weighted)
    o = acc_ref[...] / jnp.sum(l_ref[...], axis=-1, keepdims=True)
    for g in range(group):
        o_ref[:, g * dv:(g + 1) * dv] = o[g * tq:(g + 1) * tq].astype(o_ref.dtype)


def attention(q, k, v, *, batch, lq, lk, kv_heads, group, dq, dv, name):
    tq = _pick(lq, (512, 256, 128)) // (2 if group > 1 else 1)
    tk = _pick(lk, (256, 128))
    nq = lq // tq
    m_rows = group * tq
    return pl.pallas_call(
        functools.partial(_attn_kernel, group=group, dq=dq, dv=dv, tk=tk),
        grid=(batch, kv_heads, nq),
        in_specs=[pl.BlockSpec((tq, group * dq), lambda b, h, i: (b * nq + i, h)),
                  pl.BlockSpec((lk, dq), lambda b, h, i: (b, h)),
                  pl.BlockSpec((lk, dv), lambda b, h, i: (b, h))],
        out_specs=pl.BlockSpec((tq, group * dv), lambda b, h, i: (b * nq + i, h)),
        out_shape=jax.ShapeDtypeStruct((batch * lq, kv_heads * group * dv), BF16),
        scratch_shapes=[pltpu.VMEM((lk // tk, m_rows, tk), F32), pltpu.VMEM((m_rows, LANE), F32),
                        pltpu.VMEM((m_rows, LANE), F32), pltpu.VMEM((m_rows, dv), F32)],
        compiler_params=_params("parallel", "parallel", "parallel"),
        name=name,
    )(q, k, v)


def _split_hi_lo(a):
    hi = a.astype(BF16).astype(F32)
    return hi, (a - hi).astype(BF16).astype(F32)


def _dot_split(a, b):
    ah, al = _split_hi_lo(a)
    bh, bl = _split_hi_lo(b)
    lhs = jnp.concatenate([ah, al, ah, al], axis=2).astype(BF16)
    rhs = jnp.concatenate([bh, bh, bl, bl], axis=1).astype(BF16)
    return lax.dot_general(lhs, rhs, (((2,), (1,)), ((0,), (0,))), preferred_element_type=F32)


def _dot_nt(a, b):
    return lax.dot_general(a.astype(BF16), b.astype(BF16), (((2,), (2,)), ((0,), (0,))), preferred_element_type=F32)


def _chunk_masks(backward):
    r = lax.broadcasted_iota(jnp.int32, (GDN_CHUNK, GDN_CHUNK), 0)
    c = lax.broadcasted_iota(jnp.int32, (GDN_CHUNK, GDN_CHUNK), 1)
    incl = (r <= c) if backward else (r >= c)
    strict = (r < c) if backward else (r > c)
    return r, c, incl, strict


def _decay_mask(dcol, r, c, incl):
    drow = jnp.sum(jnp.where(r == c, dcol, 0.0), axis=1, keepdims=True)
    return jnp.where(incl, jnp.exp(jnp.where(incl, dcol - drow, 0.0)), 0.0)


def _unit_tri_inverse(a_mat, r, c):
    eye = (r == c).astype(F32)
    same16 = jnp.right_shift(r, 4) == jnp.right_shift(c, 4)
    same32 = jnp.right_shift(r, 5) == jnp.right_shift(c, 5)
    x = jnp.where(same16, -a_mat, 0.0)
    inv = eye + x
    for _ in range(3):
        x = _dot_split(x, x)
        inv = inv + _dot_split(inv, x)
    for off_blocks in (jnp.logical_and(same32, jnp.logical_not(same16)), jnp.logical_not(same32)):
        off = jnp.where(off_blocks, a_mat, 0.0)
        inv = inv - _dot_split(inv, _dot_split(off, inv))
    return inv


def _head_column(ref, lane, index, n_c):
    col = jnp.sum(jnp.where(lane == index, ref[...], 0.0), axis=1, keepdims=True)
    return col.reshape(n_c, GDN_CHUNK, 1)


def _gdn_solve_kernel(k_ref, v_ref, d_ref, b_ref, uw_ref, *, head_lanes):
    h = pl.program_id(1)
    n_c = k_ref.shape[0] // GDN_CHUNK
    lane = lax.broadcasted_iota(jnp.int32, (1, d_ref.shape[1]), 1)
    kc = k_ref[...].reshape(n_c, GDN_CHUNK, GDN_DK)
    vc = v_ref[...].reshape(n_c, GDN_CHUNK, GDN_DV)
    for direction in range(2):
        r, c, incl, strict = _chunk_masks(direction == 1)
        dcol = _head_column(d_ref, lane, direction * head_lanes + h, n_c)
        bcol = _head_column(b_ref, lane, direction * head_lanes + h, n_c)
        kb = kc * bcol
        a_mat = jnp.where(strict, _dot_nt(kb, kc) * _decay_mask(dcol, r, c, incl), 0.0)
        rhs = jnp.concatenate([vc * bcol, kb * jnp.exp(dcol)], axis=2)
        uw = _dot_split(_unit_tri_inverse(a_mat, r, c), rhs)
        uw_ref[direction] = uw.reshape(n_c * GDN_CHUNK, GDN_DV + GDN_DK)


def gdn_solve(qkv, dcs, beta, *, heads):
    t = qkv.shape[0]
    rows = _pick(t, (512, 256, 128, 64))
    width = GDN_DV + GDN_DK
    return pl.pallas_call(
        functools.partial(_gdn_solve_kernel, head_lanes=heads),
        grid=(t // rows, heads),
        in_specs=[pl.BlockSpec((rows, GDN_DK), lambda i, h: (i, heads + h)),
                  pl.BlockSpec((rows, GDN_DV), lambda i, h: (i, 2 * heads + h)),
                  pl.BlockSpec((rows, 2 * heads), lambda i, h: (i, 0)),
                  pl.BlockSpec((rows, 2 * heads), lambda i, h: (i, 0))],
        out_specs=pl.BlockSpec((2, rows, width), lambda i, h: (0, i, h)),
        out_shape=jax.ShapeDtypeStruct((2, t, heads * width), F32),
        compiler_params=_params("parallel", "parallel"),
        name="gdn_solve",
    )(qkv, qkv, dcs, beta)


def _gdn_scan_kernel(qf_ref, kf_ref, uwf_ref, df_ref, qb_ref, kb_ref, uwb_ref, db_ref, s0_ref,
                     of_ref, ob_ref, sfin_ref, s_ref, *, heads_per_step, head_lanes):
    i = pl.program_id(2)
    hg = pl.program_id(1)
    rows = qf_ref.shape[0]
    n_c = rows // GDN_CHUNK
    lane = lax.broadcasted_iota(jnp.int32, (1, df_ref.shape[1]), 1)

    @pl.when(i == 0)
    def _():
        s_ref[...] = s0_ref[...].astype(F32)

    def bmm(a, b):
        return lax.dot_general(a, b, (((2,), (1,)), ((0,), (0,))), preferred_element_type=F32)

    width = GDN_DV + GDN_DK
    streams = ((qf_ref, kf_ref, uwf_ref, df_ref, of_ref), (qb_ref, kb_ref, uwb_ref, db_ref, ob_ref))
    for ci in range(n_c):
        for direction, (q_ref, k_ref, uw_ref, d_ref, o_ref) in enumerate(streams):
            backward = direction == 1
            r, c, incl, _ = _chunk_masks(backward)
            pos = (n_c - 1 - ci) if backward else ci
            sl = slice(pos * GDN_CHUNK, (pos + 1) * GDN_CHUNK)
            heads = range(heads_per_step)
            dcol = jnp.stack([jnp.sum(jnp.where(lane == direction * head_lanes + hg * heads_per_step + hh,
                                                d_ref[sl, :], 0.0), axis=1, keepdims=True) for hh in heads])
            qc = jnp.stack([q_ref[sl, hh * GDN_DK:(hh + 1) * GDN_DK] for hh in heads])
            kc = jnp.stack([k_ref[sl, hh * GDN_DK:(hh + 1) * GDN_DK] for hh in heads])
            u = jnp.stack([uw_ref[sl, hh * width:hh * width + GDN_DV] for hh in heads])
            w = jnp.stack([uw_ref[sl, hh * width + GDN_DV:(hh + 1) * width] for hh in heads])
            attn = _dot_nt(qc, kc) * _decay_mask(dcol, r, c, incl)
            d_last = dcol[:, 0:1, :] if backward else dcol[:, GDN_CHUNK - 1:GDN_CHUNK, :]
            q_dec = (qc * jnp.exp(dcol)).astype(BF16)
            k_dec = (kc * jnp.exp(d_last - dcol)).astype(BF16)
            s = s_ref[direction]
            s_b = s.astype(BF16)
            v_new = u - bmm(w.astype(BF16), s_b)
            v_b = v_new.astype(BF16)
            o = bmm(q_dec, s_b) + bmm(attn.astype(BF16), v_b)
            for hh in heads:
                o_ref[sl, hh * GDN_DV:(hh + 1) * GDN_DV] = o[hh]
            s_ref[direction] = s * jnp.exp(d_last) + lax.dot_general(
                k_dec, v_b, (((1,), (1,)), ((0,), (0,))), preferred_element_type=F32)

    @pl.when(i == pl.num_programs(2) - 1)
    def _():
        sfin_ref[...] = s_ref[...]


def gdn_scan(qkv, uw, dcs, s0, *, row_offset, batch, length, heads):
    rows = _pick(length, (256, 128, 64))
    hps = _pick(heads, (4, 2, 1))
    n_l = length // rows
    hgroups = heads // hps
    width = GDN_DV + GDN_DK
    assert row_offset % rows == 0
    first = row_offset // rows

    def fwd(b, g, i):
        return b * n_l + i

    def bwd(b, g, i):
        return b * n_l + (n_l - 1 - i)

    def stream_specs(pos, direction):
        return [pl.BlockSpec((rows, hps * GDN_DK), lambda b, g, i: (first + pos(b, g, i), g)),
                pl.BlockSpec((rows, hps * GDN_DK), lambda b, g, i: (first + pos(b, g, i), hgroups + g)),
                pl.BlockSpec((None, rows, hps * width), lambda b, g, i: (direction, first + pos(b, g, i), g)),
                pl.BlockSpec((rows, 2 * heads), lambda b, g, i: (first + pos(b, g, i), 0))]

    state_spec = pl.BlockSpec((None, 2, hps, GDN_DK, GDN_DV), lambda b, g, i: (b, 0, g, 0, 0))
    return pl.pallas_call(
        functools.partial(_gdn_scan_kernel, heads_per_step=hps, head_lanes=heads),
        grid=(batch, hgroups, n_l),
        in_specs=stream_specs(fwd, 0) + stream_specs(bwd, 1) + [state_spec],
        out_specs=[pl.BlockSpec((rows, hps * GDN_DV), lambda b, g, i: (fwd(b, g, i), g)),
                   pl.BlockSpec((rows, hps * GDN_DV), lambda b, g, i: (bwd(b, g, i), g)),
                   state_spec],
        out_shape=[jax.ShapeDtypeStruct((batch * length, heads * GDN_DV), F32),
                   jax.ShapeDtypeStruct((batch * length, heads * GDN_DV), F32),
                   jax.ShapeDtypeStruct((batch, 2, heads, GDN_DK, GDN_DV), F32)],
        scratch_shapes=[pltpu.VMEM((2, hps, GDN_DK, GDN_DV), F32)],
        compiler_params=_params("parallel", "parallel", "arbitrary"),
        name="gdn_scan",
    )(qkv, qkv, uw, dcs, qkv, qkv, uw, dcs, s0)


def _row_gather(idx_ref, src_hbm, dst_vmem, sem, n_rows):
    def start():
        def body(r, carry):
            pltpu.make_async_copy(src_hbm.at[pl.ds(idx_ref[0, r], 1), :], dst_vmem.at[pl.ds(r, 1), :], sem).start()
            return carry
        lax.fori_loop(0, n_rows, body, 0, unroll=GATHER_UNROLL)

    def wait():
        def body(r, carry):
            pltpu.make_async_copy(src_hbm.at[pl.ds(0, 1), :], dst_vmem.at[pl.ds(0, 1), :], sem).wait()
            return carry
        lax.fori_loop(0, n_rows, body, 0, unroll=GATHER_UNROLL)
    return start, wait


def _moe_kernel(be_ref, nu_ref, tok_ref, tok_next_ref, h_hbm, wg_ref, wu_ref, wd_ref, o_ref,
                xbuf_ref, xlo_ref, xhi_ref, acc_ref, sem):
    i = pl.program_id(0)
    f = pl.program_id(1)
    n_used = nu_ref[0]
    tm, half = xlo_ref.shape

    @pl.when(i < n_used)
    def _():
        @pl.when(f == 0)
        def _():
            for slot in range(2):
                start_this, wait_this = _row_gather(tok_ref, h_hbm, xbuf_ref.at[slot], sem.at[slot], tm)
                start_next, _ = _row_gather(tok_next_ref, h_hbm, xbuf_ref.at[1 - slot], sem.at[1 - slot], tm)

                @pl.when(lax.rem(i, 2) == slot)
                def _():
                    @pl.when(i == 0)
                    def _():
                        start_this()
                    wait_this()

                    @pl.when(i + 1 < n_used)
                    def _():
                        start_next()
                    lo, hi = _unpack_bf16_pair(xbuf_ref[slot])
                    xlo_ref[...] = lo.astype(BF16)
                    xhi_ref[...] = hi.astype(BF16)
            acc_ref[...] = jnp.zeros_like(acc_ref)

        def x_dot(w_ref):
            return (jnp.dot(xlo_ref[...], w_ref[:half, :].astype(BF16), preferred_element_type=F32)
                    + jnp.dot(xhi_ref[...], w_ref[half:, :].astype(BF16), preferred_element_type=F32))

        gate = x_dot(wg_ref)
        up = x_dot(wu_ref)
        hmid = (gate * jax.nn.sigmoid(gate) * up).astype(BF16)
        acc_ref[...] += jnp.dot(hmid, wd_ref[...].astype(BF16), preferred_element_type=F32)

        @pl.when(f == pl.num_programs(1) - 1)
        def _():
            o_ref[...] = _pack_bf16_pair(acc_ref[:, :half], acc_ref[:, half:])

    @pl.when(jnp.logical_and(i >= n_used, f == 0))
    def _():
        o_ref[...] = jnp.zeros_like(o_ref)


def moe_experts(h_packed, slot_tok, block_e, n_used, w_gate, w_up, w_down, *, tm, layer):
    d = 2 * h_packed.shape[1]
    p = slot_tok.shape[0]
    f_dim = w_gate.shape[3]
    tf = _pick(f_dim, (256, 128))
    n_blocks = p // tm
    last_f = f_dim // tf - 1
    tok_blocks = slot_tok.reshape(n_blocks, 1, tm)

    def blk(i, nu):
        return jnp.minimum(i, nu[0] - 1)

    def f_blk(i, f, nu):
        return jnp.where(i < nu[0], f, last_f)

    grid_spec = pltpu.PrefetchScalarGridSpec(
        num_scalar_prefetch=2,
        grid=(n_blocks, f_dim // tf),
        in_specs=[pl.BlockSpec((None, 1, tm), lambda i, f, be, nu: (blk(i, nu), 0, 0), memory_space=pltpu.SMEM),
                  pl.BlockSpec((None, 1, tm), lambda i, f, be, nu: (blk(i + 1, nu), 0, 0), memory_space=pltpu.SMEM),
                  pl.BlockSpec(memory_space=pl.ANY),
                  pl.BlockSpec((None, None, d, tf), lambda i, f, be, nu: (layer, be[blk(i, nu)], 0, f_blk(i, f, nu))),
                  pl.BlockSpec((None, None, d, tf), lambda i, f, be, nu: (layer, be[blk(i, nu)], 0, f_blk(i, f, nu))),
                  pl.BlockSpec((None, None, tf, d), lambda i, f, be, nu: (layer, be[blk(i, nu)], f_blk(i, f, nu), 0))],
        out_specs=pl.BlockSpec((tm, d // 2), lambda i, f, be, nu: (i, 0)),
        scratch_shapes=[pltpu.VMEM((2, tm, d // 2), jnp.uint32), pltpu.VMEM((tm, d // 2), BF16),
                        pltpu.VMEM((tm, d // 2), BF16), pltpu.VMEM((tm, d), F32),
                        pltpu.SemaphoreType.DMA((2,))],
    )
    return pl.pallas_call(
        _moe_kernel,
        grid_spec=grid_spec,
        out_shape=jax.ShapeDtypeStruct((p, d // 2), jnp.uint32),
        compiler_params=_params("arbitrary", "arbitrary"),
        name="moe_experts",
    )(block_e, n_used, tok_blocks, tok_blocks, h_packed, w_gate, w_up, w_down)


def _moe_combine_kernel(dest_ref, y_hbm, x_ref, gate_ref, mod_ref, o_ref, rows_ref, sem, *, gate_idx):
    tc, d = x_ref.shape
    half = d // 2
    start, wait = _row_gather(dest_ref, y_hbm, rows_ref, sem, 2 * tc)
    start()
    wait()
    lo0, hi0 = _unpack_bf16_pair(rows_ref[:tc, :])
    lo1, hi1 = _unpack_bf16_pair(rows_ref[tc:, :])
    g0 = gate_ref[:, 0:1]
    g1 = gate_ref[:, 1:2]
    o_ref[:, :half] = x_ref[:, :half] + mod_ref[gate_idx:gate_idx + 1, :half] * (lo0 * g0 + lo1 * g1)
    o_ref[:, half:] = x_ref[:, half:] + mod_ref[gate_idx:gate_idx + 1, half:] * (hi0 * g0 + hi1 * g1)


def moe_combine_residual(y_packed, dest, gates, x, mod, *, gate_idx, tc, row_fn):
    t, d = x.shape
    n_tiles = t // tc
    dest_tiles = dest.reshape(n_tiles, tc, TOP_K).transpose(0, 2, 1).reshape(n_tiles, 1, TOP_K * tc)
    return pl.pallas_call(
        functools.partial(_moe_combine_kernel, gate_idx=gate_idx),
        grid=(n_tiles,),
        in_specs=[pl.BlockSpec((None, 1, TOP_K * tc), lambda i: (i, 0, 0), memory_space=pltpu.SMEM),
                  pl.BlockSpec(memory_space=pl.ANY),
                  pl.BlockSpec((tc, d), lambda i: (i, 0)),
                  pl.BlockSpec((tc, gates.shape[1]), lambda i: (i, 0)),
                  pl.BlockSpec((None, mod.shape[1], d), lambda i: (row_fn(i), 0, 0))],
        out_specs=pl.BlockSpec((tc, d), lambda i: (i, 0)),
        out_shape=jax.ShapeDtypeStruct((t, d), F32),
        scratch_shapes=[pltpu.VMEM((TOP_K * tc, d // 2), jnp.uint32), pltpu.SemaphoreType.DMA(())],
        compiler_params=_params("arbitrary"),
        name="moe_combine",
    )(dest_tiles, y_packed, x, gates, mod)


def _rms(x, g):
    return x * lax.rsqrt(jnp.mean(x * x, axis=-1, keepdims=True) + EPS) * g


def _rope_tables(t_prompt, dec_batch, dec_seq, width):
    a = width // 2
    half = a // 2
    freqs = ROPE_THETA ** (-np.arange(half, dtype=np.float32) / half)
    t = np.arange(dec_seq)
    row, col = t // GRID_W, t % GRID_W
    ang_r = row[:, None].astype(np.float32) * freqs[None, :]
    ang_c = col[:, None].astype(np.float32) * freqs[None, :]
    cos = np.concatenate([np.cos(ang_r), np.cos(ang_r), np.cos(ang_c), np.cos(ang_c)], axis=-1)
    sin = np.concatenate([-np.sin(ang_r), np.sin(ang_r), -np.sin(ang_c), np.sin(ang_c)], axis=-1)
    cos = np.concatenate([np.ones((t_prompt, width), np.float32), np.tile(cos, (dec_batch, 1))], axis=0)
    sin = np.concatenate([np.zeros((t_prompt, width), np.float32), np.tile(sin, (dec_batch, 1))], axis=0)
    return jnp.asarray(cos), jnp.asarray(sin)


def _rope_apply(x, cos, sin):
    half = x.shape[-1] // 4
    xs = x.reshape(x.shape[:-1] + (2, 2, half))
    swapped = jnp.flip(xs, axis=-2).reshape(x.shape)
    return x * cos[:, None, :] + swapped * sin[:, None, :]


def _l2(x):
    return x * lax.rsqrt(jnp.sum(x * x, axis=-1, keepdims=True) + EPS)


def _gdn_inputs(qkv, a, b, w_conv, a_log, dt_bias, heads, seq_start, seq_end):
    t = qkv.shape[0]
    prev = jnp.where(seq_start, 0.0, jnp.roll(qkv, 1, axis=0))
    nxt = jnp.where(seq_end, 0.0, jnp.roll(qkv, -1, axis=0))
    act = jax.nn.silu(prev * w_conv[0] + qkv * w_conv[1] + nxt * w_conv[2])
    nk = heads * GDN_DK
    q = _l2(act[:, :nk].reshape(t, heads, GDN_DK)) * GDN_DK ** -0.5
    k = _l2(act[:, nk:2 * nk].reshape(t, heads, GDN_DK))
    qkv_c = jnp.concatenate([q.reshape(t, nk), k.reshape(t, nk), act[:, 2 * nk:]], axis=1)
    g = -jnp.exp(a_log.reshape(2 * heads)) * jax.nn.softplus(a + dt_bias.reshape(2 * heads))
    gc = g.reshape(t // GDN_CHUNK, GDN_CHUNK, 2 * heads)
    prefix = jnp.cumsum(gc[..., :heads], axis=1)
    suffix = jnp.flip(jnp.cumsum(jnp.flip(gc[..., heads:], axis=1), axis=1), axis=1)
    dcs = jnp.concatenate([prefix, suffix], axis=-1).reshape(t, 2 * heads)
    return qkv_c, dcs, jax.nn.sigmoid(b)


def _dispatch_plan(experts, n_experts, tm):
    t, k = experts.shape
    a = t * k
    n_blocks = a // tm + n_experts
    flat_e = experts.reshape(-1)
    onehot = (flat_e[:, None] == jnp.arange(n_experts, dtype=jnp.int32)[None, :]).astype(jnp.int32)
    csum = jnp.cumsum(onehot, axis=0)
    pos = jnp.sum((csum - onehot) * onehot, axis=1)
    counts = csum[-1]
    padded = (counts + tm - 1) // tm * tm
    pend = jnp.cumsum(padded)
    pstart = pend - padded
    dest = pstart[flat_e] + pos
    flat_tok = jnp.arange(a, dtype=jnp.int32) // k
    slot_tok = jnp.zeros((n_blocks * tm,), jnp.int32).at[dest].set(flat_tok)
    block_e = jnp.minimum(jnp.searchsorted(pend, jnp.arange(n_blocks, dtype=jnp.int32) * tm, side='right'),
                          n_experts - 1).astype(jnp.int32)
    n_used = (pend[-1] // tm).astype(jnp.int32).reshape(1)
    return dest.reshape(t, k), slot_tok, block_e, n_used


def kernel(x_prompt, x_sample, cache_mla_ckv, cache_mla_krope, cache_gqa_k, cache_gqa_v, state_gdn, c, c_ctx, w_mod, b_mod, g_norm_mix, g_norm_ffn, w_in, g_cq, w_q_up, g_ckv, w_kv_up, g_q_mla, g_k_mla, g_q_gqa, g_k_gqa, w_conv, a_log, dt_bias, g_gdn_out, w_out, w_router_group, b_router_group, w_router_expert, b_router_expert, w_gate, w_up, w_down):
    batch, seq, d = x_prompt.shape
    dec_batch, dec_seq, _ = x_sample.shape
    depth = w_in.shape[0]
    past = cache_mla_ckv.shape[2]
    q_lora = w_q_up.shape[1]
    kv_lora = w_kv_up.shape[1]
    mla_heads = w_q_up.shape[2] // MLA_QK
    kv_heads = cache_gqa_k.shape[3]
    gdn_heads = state_gdn.shape[3]
    gqa_heads = (w_out.shape[1] - mla_heads * MLA_V - gdn_heads * GDN_DV) // HEAD_DIM
    group = gqa_heads // kv_heads
    n_groups = w_router_group.shape[2]
    n_experts = w_gate.shape[1]
    gdn_qkv = gdn_heads * (2 * GDN_DK + GDN_DV)
    gdn_width = gdn_heads * GDN_DV

    t_p = batch * seq
    t_s = dec_batch * dec_seq
    t_all = t_p + t_s
    tm = _pick(math.gcd(t_p, dec_seq), (512, 256, 128))
    row_fn = _mod_row_fn(tm, t_p, dec_seq)
    moe_tm = _pick(t_all * TOP_K, (512, 256, 128))
    combine_tc = _pick(math.gcd(t_p, dec_seq), (256, 128))

    sizes = (q_lora, kv_lora, MLA_ROPE, gqa_heads * HEAD_DIM, kv_heads * HEAD_DIM, kv_heads * HEAD_DIM,
             gdn_qkv, gdn_width, 2 * gdn_heads, 2 * gdn_heads)
    off = np.concatenate([[0], np.cumsum(sizes)])
    order = (0, 1, 3, 4, 5, 6, 7, 2, 8, 9)
    new_sizes = [sizes[i] for i in order]
    n_in = int(off[-1])
    n_in_pad = -(-n_in // LANE) * LANE
    noff = np.concatenate([[0], np.cumsum(new_sizes)])
    seg = {name: (int(noff[j]), int(noff[j + 1])) for j, name in
           enumerate(('cq', 'ckv', 'gq', 'gk', 'gv', 'qkv', 'z', 'krope', 'a', 'b'))}
    for name in ('cq', 'ckv', 'gq', 'gk', 'gv', 'qkv', 'z', 'krope'):
        assert seg[name][0] % LANE == 0, name

    row_ids = np.arange(t_all)
    local = np.where(row_ids < t_p, row_ids % seq, (row_ids - t_p) % dec_seq)
    seq_start = jnp.asarray((local == 0)[:, None])
    seq_end = jnp.asarray((local == np.where(row_ids < t_p, seq, dec_seq) - 1)[:, None])

    cos_g, sin_g = _rope_tables(t_p, dec_batch, dec_seq, HEAD_DIM)
    cos_m, sin_m = _rope_tables(t_p, dec_batch, dec_seq, MLA_ROPE)

    cond8 = jnp.zeros((8, d), F32).at[0].set(c_ctx).at[1:1 + dec_batch].set(c)

    x = jnp.concatenate([x_prompt.reshape(t_p, d), x_sample.reshape(t_s, d)], axis=0)
    new_ckv, new_krope, new_k, new_v, new_s = [], [], [], [], []
    q_scale_m = MLA_QK ** -0.5 * LOG2E
    q_scale_g = HEAD_DIM ** -0.5 * LOG2E

    for l in range(depth):
        mod = modulation(cond8, w_mod, b_mod, l).reshape(8, 6, d)

        w_in_r = jnp.concatenate([w_in[l][:, off[i]:off[i + 1]] for i in order]
                                 + [jnp.zeros((d, n_in_pad - n_in), F32)], axis=1).astype(BF16)
        u = norm_matmul(x, g_norm_mix[l], mod, w_in_r, sh_idx=0, sc_idx=1, tm=tm, row_fn=row_fn)

        def take(name):
            return u[:, seg[name][0]:seg[name][1]]

        ckv_n = _rms(take('ckv'), g_ckv[l])
        krope = take('krope')
        cq_n = _rms(take('cq'), g_cq[l]).astype(BF16)
        wq = w_q_up[l].reshape(q_lora, mla_heads, MLA_QK)
        wq = jnp.pad(wq, ((0, 0), (0, 0), (0, MLA_PAD - MLA_QK))).reshape(q_lora, mla_heads * MLA_PAD).astype(BF16)
        q_m = matmul(cq_n, wq, name="mla_q_up").reshape(t_all, mla_heads, MLA_PAD)
        g_q_pad = jnp.pad(g_q_mla[l], (0, MLA_PAD - MLA_QK))
        q_m = q_m * lax.rsqrt(jnp.sum(q_m * q_m, axis=-1, keepdims=True) / MLA_QK + EPS) * g_q_pad
        q_rot = _rope_apply(q_m[..., MLA_NOPE:MLA_QK], cos_m, sin_m)
        q_m = jnp.concatenate([q_m[..., :MLA_NOPE], q_rot, q_m[..., MLA_QK:]], axis=-1)
        q_m = (q_m * q_scale_m).astype(BF16).reshape(t_all, mla_heads * MLA_PAD)

        wkv = w_kv_up[l].reshape(kv_lora, mla_heads, MLA_NOPE + MLA_V)
        wkv = jnp.concatenate([wkv[..., :MLA_NOPE].reshape(kv_lora, -1), wkv[..., MLA_NOPE:].reshape(kv_lora, -1)],
                              axis=1).astype(BF16)
        ckv_all = jnp.concatenate([ckv_n, cache_mla_ckv[:, l].reshape(dec_batch * past, kv_lora)], axis=0)
        krope_all = jnp.concatenate([krope, cache_mla_krope[:, l].reshape(dec_batch * past, MLA_ROPE)], axis=0)
        cos_all = jnp.concatenate([cos_m, jnp.ones((dec_batch * past, MLA_ROPE), F32)], axis=0)
        sin_all = jnp.concatenate([sin_m, jnp.zeros((dec_batch * past, MLA_ROPE), F32)], axis=0)
        kv = matmul(ckv_all, wkv, name="mla_kv_up")
        n_all = kv.shape[0]
        k_nope = kv[:, :mla_heads * MLA_NOPE].reshape(n_all, mla_heads, MLA_NOPE)
        v_m = kv[:, mla_heads * MLA_NOPE:].astype(BF16)
        ssq = jnp.sum(k_nope * k_nope, axis=-1, keepdims=True) + jnp.sum(krope_all * krope_all, axis=-1)[:, None, None]
        r_k = lax.rsqrt(ssq / MLA_QK + EPS)
        k_rope_h = _rope_apply(krope_all[:, None, :] * r_k * g_k_mla[l][MLA_NOPE:], cos_all, sin_all)
        k_m = jnp.concatenate([k_nope * r_k * g_k_mla[l][:MLA_NOPE], k_rope_h,
                               jnp.zeros((n_all, mla_heads, MLA_PAD - MLA_QK), F32)], axis=-1)
        k_m = k_m.astype(BF16).reshape(n_all, mla_heads * MLA_PAD)

        def with_ctx(lat, ctx):
            lat = lat[t_p:t_all].reshape(dec_batch, dec_seq, -1)
            ctx = ctx.reshape(dec_batch, past, -1)
            return jnp.concatenate([ctx, lat], axis=1).reshape(dec_batch * (past + dec_seq), -1)

        o_m_p = attention(q_m[:t_p], k_m[:t_p], v_m[:t_p], batch=batch, lq=seq, lk=seq, kv_heads=mla_heads,
                          group=1, dq=MLA_PAD, dv=MLA_V, name="mla_attn_ctx")
        o_m_s = attention(q_m[t_p:], with_ctx(k_m, k_m[t_all:]), with_ctx(v_m, v_m[t_all:]), batch=dec_batch,
                          lq=dec_seq, lk=past + dec_seq, kv_heads=mla_heads, group=1, dq=MLA_PAD, dv=MLA_V,
                          name="mla_attn_lat")
        o_m = jnp.concatenate([o_m_p, o_m_s], axis=0)

        q_g = _rms(take('gq').reshape(t_all, gqa_heads, HEAD_DIM), g_q_gqa[l])
        q_g = (_rope_apply(q_g, cos_g, sin_g) * q_scale_g).astype(BF16).reshape(t_all, gqa_heads * HEAD_DIM)
        k_g = _rms(take('gk').reshape(t_all, kv_heads, HEAD_DIM), g_k_gqa[l])
        v_g = take('gv')
        k_g_rot = _rope_apply(k_g, cos_g, sin_g).astype(BF16).reshape(t_all, kv_heads * HEAD_DIM)
        v_g_b = v_g.astype(BF16)
        ctx_k = cache_gqa_k[:, l].reshape(dec_batch * past, kv_heads * HEAD_DIM).astype(BF16)
        ctx_v = cache_gqa_v[:, l].reshape(dec_batch * past, kv_heads * HEAD_DIM).astype(BF16)
        o_g_p = attention(q_g[:t_p], k_g_rot[:t_p], v_g_b[:t_p], batch=batch, lq=seq, lk=seq, kv_heads=kv_heads,
                          group=group, dq=HEAD_DIM, dv=HEAD_DIM, name="gqa_attn_ctx")
        o_g_s = attention(q_g[t_p:], with_ctx(k_g_rot, ctx_k), with_ctx(v_g_b, ctx_v), batch=dec_batch, lq=dec_seq,
                          lk=past + dec_seq, kv_heads=kv_heads, group=group, dq=HEAD_DIM, dv=HEAD_DIM,
                          name="gqa_attn_lat")
        o_g = jnp.concatenate([o_g_p, o_g_s], axis=0)

        qkv_c, dcs, beta = _gdn_inputs(take('qkv'), take('a'), take('b'), w_conv[l], a_log[l], dt_bias[l],
                                       gdn_heads, seq_start, seq_end)
        uw = gdn_solve(qkv_c, dcs, beta, heads=gdn_heads)
        s0_p = jnp.zeros((batch, 2, gdn_heads, GDN_DK, GDN_DV), F32)
        of_p, ob_p, s_ctx = gdn_scan(qkv_c, uw, dcs, s0_p, row_offset=0, batch=batch, length=seq, heads=gdn_heads)
        of_s, ob_s, _ = gdn_scan(qkv_c, uw, dcs, state_gdn[:, l], row_offset=t_p, batch=dec_batch,
                                 length=dec_seq, heads=gdn_heads)
        od = jnp.concatenate([of_p + ob_p, of_s + ob_s], axis=0).reshape(t_all, gdn_heads, GDN_DV)
        o_d = (_rms(od, g_gdn_out[l]) * jax.nn.silu(take('z').reshape(t_all, gdn_heads, GDN_DV)))
        o_d = o_d.reshape(t_all, gdn_width).astype(BF16)

        new_ckv.append(ckv_n[:t_p].reshape(batch, seq, kv_lora))
        new_krope.append(krope[:t_p].reshape(batch, seq, MLA_ROPE))
        new_k.append(k_g[:t_p].reshape(batch, seq, kv_heads, HEAD_DIM))
        new_v.append(v_g[:t_p].reshape(batch, seq, kv_heads, HEAD_DIM))
        new_s.append(s_ctx)

        x = out_proj_residual(o_m, o_g, o_d, w_out[l].astype(BF16), x, mod, gate_idx=2, tm=tm, row_fn=row_fn)

        w_router = jnp.concatenate([w_router_group[l], w_router_expert[l],
                                    jnp.zeros((d, LANE - n_groups - n_experts), F32)], axis=1)
        b_router = jnp.concatenate([b_router_group[l], b_router_expert[l],
                                    jnp.zeros((LANE - n_groups - n_experts,), F32)]).reshape(1, LANE)
        h2, gates, experts = ffn_norm_router(x, g_norm_ffn[l], mod, w_router, b_router, sh_idx=3, sc_idx=4, tm=tm,
                                             row_fn=row_fn, n_groups=n_groups, epg=n_experts // n_groups)
        dest, slot_tok, block_e, n_used = _dispatch_plan(experts[:, :TOP_K], n_experts, moe_tm)
        yb = moe_experts(h2, slot_tok, block_e, n_used, w_gate, w_up, w_down, tm=moe_tm, layer=l)
        x = moe_combine_residual(yb, dest, gates, x, mod, gate_idx=5, tc=combine_tc,
                                 row_fn=_mod_row_fn(combine_tc, t_p, dec_seq))

    return (x[:t_p].reshape(batch, seq, d), x[t_p:].reshape(dec_batch, dec_seq, d),
            jnp.stack(new_ckv, axis=1), jnp.stack(new_krope, axis=1), jnp.stack(new_k, axis=1),
            jnp.stack(new_v, axis=1), jnp.stack(new_s, axis=1))
```

```python
import functools
import math

import numpy as np
import jax
import jax.numpy as jnp
from jax import lax
from jax.experimental import pallas as pl
from jax.experimental.pallas import tpu as pltpu

F32 = jnp.float32
BF16 = jnp.bfloat16

EPS = 1e-6
ROPE_THETA = 10000.0
GRID_W = 64
HEAD_DIM = 128
MLA_NOPE = 128
MLA_ROPE = 64
MLA_V = 128
MLA_QK = MLA_NOPE + MLA_ROPE
MLA_PAD = 256
GDN_DK = 128
GDN_DV = 128
GDN_CHUNK = 64
TOP_K = 2
LANE = 128
VMEM_LIMIT_BYTES = 56 * 1024 * 1024
LOG2E = 1.4426950408889634
ATTN_UNROLL = 4
GATHER_UNROLL = 8


def _params(*sem):
    return pltpu.CompilerParams(dimension_semantics=sem, vmem_limit_bytes=VMEM_LIMIT_BYTES)


def _pick(n, prefs):
    for p in prefs:
        if n % p == 0:
            return p
    return n


def _mod_kernel(c_ref, w_ref, b_ref, o_ref):
    c = c_ref[...]
    s = (c * jax.nn.sigmoid(c)).astype(BF16)
    o_ref[...] = jnp.dot(s, w_ref[...].astype(BF16), preferred_element_type=F32) + b_ref[...]


def modulation(cond8, w_mod, b_mod, layer):
    rows, d = cond8.shape
    depth, _, n = w_mod.shape
    tn = _pick(n, (512, 256, 128))
    return pl.pallas_call(
        _mod_kernel,
        grid=(n // tn,),
        in_specs=[pl.BlockSpec((rows, d), lambda j: (0, 0)),
                  pl.BlockSpec((None, d, tn), lambda j: (layer, 0, j)),
                  pl.BlockSpec((None, 1, tn), lambda j: (layer, 0, j))],
        out_specs=pl.BlockSpec((rows, tn), lambda j: (0, j)),
        out_shape=jax.ShapeDtypeStruct((rows, n), F32),
        compiler_params=_params("parallel"),
        name="modulation",
    )(cond8, w_mod, b_mod.reshape(depth, 1, n))


def _mod_row_fn(tm, t_prompt, dec_seq):
    n_p = t_prompt // tm
    per_b = dec_seq // tm

    def row(i):
        return jnp.where(i < n_p, 0, 1 + (i - n_p) // per_b)
    return row


def _modulated_norm(x, g, sc, sh):
    ms = jnp.mean(x * x, axis=-1, keepdims=True)
    return (x * lax.rsqrt(ms + EPS) * g) * (1.0 + sc) + sh


def _norm_matmul_kernel(x_ref, g_ref, mod_ref, w_ref, o_ref, h_ref, *, sh_idx, sc_idx):
    @pl.when(pl.program_id(1) == 0)
    def _():
        h = _modulated_norm(x_ref[...], g_ref[...], mod_ref[sc_idx:sc_idx + 1, :],
                            mod_ref[sh_idx:sh_idx + 1, :])
        h_ref[...] = h.astype(BF16)
    o_ref[...] = jnp.dot(h_ref[...], w_ref[...], preferred_element_type=F32).astype(o_ref.dtype)


def norm_matmul(x, g, mod, w, *, sh_idx, sc_idx, tm, row_fn, out_dtype=F32):
    t, d = x.shape
    n = w.shape[1]
    tn = _pick(n, (1152, 1024, 512, 384, 256, 128))
    return pl.pallas_call(
        functools.partial(_norm_matmul_kernel, sh_idx=sh_idx, sc_idx=sc_idx),
        grid=(t // tm, n // tn),
        in_specs=[pl.BlockSpec((tm, d), lambda i, j: (i, 0)),
                  pl.BlockSpec((1, d), lambda i, j: (0, 0)),
                  pl.BlockSpec((None, mod.shape[1], d), lambda i, j: (row_fn(i), 0, 0)),
                  pl.BlockSpec((d, tn), lambda i, j: (0, j))],
        out_specs=pl.BlockSpec((tm, tn), lambda i, j: (i, j)),
        out_shape=jax.ShapeDtypeStruct((t, n), out_dtype),
        scratch_shapes=[pltpu.VMEM((tm, d), BF16)],
        compiler_params=_params("parallel", "arbitrary"),
        name="norm_in_proj",
    )(x, g.reshape(1, d), mod, w)


def _out_proj_kernel(om_ref, og_ref, of_ref, ob_ref, z_ref, gd_ref, wm_ref, wg_ref, wd_ref, x_ref, mod_ref,
                     o_ref, od_ref, *, gate_idx):
    @pl.when(pl.program_id(1) == 0)
    def _():
        for h in range(of_ref.shape[1] // GDN_DV):
            hl = slice(h * GDN_DV, (h + 1) * GDN_DV)
            z = z_ref[:, hl]
            od_ref[:, hl] = (_head_rms(of_ref[:, hl] + ob_ref[:, hl], gd_ref[...])
                             * (z * jax.nn.sigmoid(z))).astype(BF16)
    acc = jnp.dot(om_ref[...], wm_ref[...], preferred_element_type=F32)
    acc += jnp.dot(og_ref[...], wg_ref[...], preferred_element_type=F32)
    acc += jnp.dot(od_ref[...], wd_ref[...], preferred_element_type=F32)
    o_ref[...] = x_ref[...] + mod_ref[gate_idx:gate_idx + 1, :] * acc


def out_proj_residual(o_m, o_g, o_fwd, o_bwd, u, z_col, g_gdn, w_out, x, mod, *, gate_idx, tm, row_fn):
    t, d = x.shape
    km, kg, kd = o_m.shape[1], o_g.shape[1], o_fwd.shape[1]
    z0, zw = z_col
    assert km == kg and (km + kg) % kd == 0 and zw == kd
    tn = _pick(d, (1024, 512, 256, 128))
    return pl.pallas_call(
        functools.partial(_out_proj_kernel, gate_idx=gate_idx),
        grid=(t // tm, d // tn),
        in_specs=[pl.BlockSpec((tm, km), lambda i, j: (i, 0)),
                  pl.BlockSpec((tm, kg), lambda i, j: (i, 0)),
                  pl.BlockSpec((tm, kd), lambda i, j: (i, 0)),
                  pl.BlockSpec((tm, kd), lambda i, j: (i, 0)),
                  pl.BlockSpec((tm, zw), lambda i, j: (i, z0 // zw)),
                  pl.BlockSpec((1, GDN_DV), lambda i, j: (0, 0)),
                  pl.BlockSpec((km, tn), lambda i, j: (0, j)),
                  pl.BlockSpec((kg, tn), lambda i, j: (1, j)),
                  pl.BlockSpec((kd, tn), lambda i, j: ((km + kg) // kd, j)),
                  pl.BlockSpec((tm, tn), lambda i, j: (i, j)),
                  pl.BlockSpec((None, mod.shape[1], tn), lambda i, j: (row_fn(i), 0, j))],
        out_specs=pl.BlockSpec((tm, tn), lambda i, j: (i, j)),
        out_shape=jax.ShapeDtypeStruct((t, d), F32),
        scratch_shapes=[pltpu.VMEM((tm, kd), BF16)],
        compiler_params=_params("parallel", "arbitrary"),
        name="out_proj",
    )(o_m, o_g, o_fwd, o_bwd, u, g_gdn.reshape(1, GDN_DV), w_out, w_out, w_out, x, mod)


def _pack_bf16_pair(lo, hi):
    lo_bits = lax.bitcast_convert_type(lo.astype(BF16).astype(F32), jnp.uint32)
    hi_bits = lax.bitcast_convert_type(hi.astype(BF16).astype(F32), jnp.uint32)
    return jnp.right_shift(lo_bits, jnp.uint32(16)) | (hi_bits & jnp.uint32(0xFFFF0000))


def _unpack_bf16_pair(word):
    lo = lax.bitcast_convert_type(jnp.left_shift(word, jnp.uint32(16)), F32)
    hi = lax.bitcast_convert_type(word & jnp.uint32(0xFFFF0000), F32)
    return lo, hi


def _ffn_norm_router_kernel(x_ref, g_ref, mod_ref, wr_ref, br_ref, h_ref, gate_ref, exp_ref, *,
                            sh_idx, sc_idx, n_groups, epg):
    h = _modulated_norm(x_ref[...], g_ref[...], mod_ref[sc_idx:sc_idx + 1, :],
                        mod_ref[sh_idx:sh_idx + 1, :])
    half = h.shape[1] // 2
    h_ref[...] = _pack_bf16_pair(h[:, :half], h[:, half:])
    lg = jnp.dot(h, wr_ref[...], preferred_element_type=F32, precision=lax.Precision.HIGHEST) + br_ref[...]
    lane = lax.broadcasted_iota(jnp.int32, lg.shape, 1).astype(F32)
    neg = jnp.float32(-1e30)
    far = jnp.float32(2 * LANE)

    def first_argmax(v):
        top = jnp.max(v, axis=-1, keepdims=True)
        return top, jnp.min(jnp.where(v == top, lane, far), axis=-1, keepdims=True)

    is_group = lane < n_groups
    g_max, g_idx = first_argmax(jnp.where(is_group, lg, neg))
    g_top = 1.0 / jnp.sum(jnp.where(is_group, jnp.exp(lg - g_max), 0.0), axis=-1, keepdims=True)
    lo = n_groups + g_idx * epg
    el = jnp.where(jnp.logical_and(lane >= lo, lane < lo + epg), lg, neg)
    e1, i1 = first_argmax(el)
    e2, i2 = first_argmax(jnp.where(lane == i1, neg, el))
    r = jnp.exp(e2 - e1)
    gate1 = g_top / (1.0 + r)
    gate2 = g_top * r / (1.0 + r)
    gate_ref[...] = jnp.where(lane == 0, gate1, jnp.where(lane == 1, gate2, 0.0))
    exp_ref[...] = jnp.where(lane == 0, i1 - n_groups, jnp.where(lane == 1, i2 - n_groups, 0.0)).astype(jnp.int32)


def ffn_norm_router(x, g, mod, w_router, b_router, *, sh_idx, sc_idx, tm, row_fn, n_groups, epg):
    t, d = x.shape
    nr = w_router.shape[1]
    assert d % (2 * LANE) == 0
    return pl.pallas_call(
        functools.partial(_ffn_norm_router_kernel, sh_idx=sh_idx, sc_idx=sc_idx, n_groups=n_groups, epg=epg),
        grid=(t // tm,),
        in_specs=[pl.BlockSpec((tm, d), lambda i: (i, 0)),
                  pl.BlockSpec((1, d), lambda i: (0, 0)),
                  pl.BlockSpec((None, mod.shape[1], d), lambda i: (row_fn(i), 0, 0)),
                  pl.BlockSpec((d, nr), lambda i: (0, 0)),
                  pl.BlockSpec((1, nr), lambda i: (0, 0))],
        out_specs=[pl.BlockSpec((tm, d // 2), lambda i: (i, 0)),
                   pl.BlockSpec((tm, nr), lambda i: (i, 0)),
                   pl.BlockSpec((tm, nr), lambda i: (i, 0))],
        out_shape=[jax.ShapeDtypeStruct((t, d // 2), jnp.uint32), jax.ShapeDtypeStruct((t, nr), F32),
                   jax.ShapeDtypeStruct((t, nr), jnp.int32)],
        compiler_params=_params("parallel"),
        name="ffn_norm_router",
    )(x, g.reshape(1, d), mod, w_router, b_router)


def _rope_rotate(x, cos, sin, pair):
    lane = lax.broadcasted_iota(jnp.int32, x.shape, 1)
    first = (lane & (2 * pair - 1)) < pair
    partner = jnp.where(first, pltpu.roll(x, LANE - pair, axis=1), pltpu.roll(x, pair, axis=1))
    return x * cos + partner * sin


def _head_rms(x, g):
    return x * lax.rsqrt(jnp.mean(x * x, axis=-1, keepdims=True) + EPS) * g


def _gqa_prep_kernel(gq_ref, gk_ref, gv_ref, gq_gain_ref, gk_gain_ref, cos_ref, sin_ref,
                     q_ref, k_rot_ref, k_ref, v_ref, *, q_scale):
    cos, sin = cos_ref[...], sin_ref[...]
    pair = HEAD_DIM // 4
    for h in range(gq_ref.shape[1] // HEAD_DIM):
        hl = slice(h * HEAD_DIM, (h + 1) * HEAD_DIM)
        q = _rope_rotate(_head_rms(gq_ref[:, hl], gq_gain_ref[...]), cos, sin, pair)
        q_ref[:, hl] = (q * q_scale).astype(BF16)
    for h in range(gk_ref.shape[1] // HEAD_DIM):
        hl = slice(h * HEAD_DIM, (h + 1) * HEAD_DIM)
        k = _head_rms(gk_ref[:, hl], gk_gain_ref[...])
        k_ref[:, hl] = k
        k_rot_ref[:, hl] = _rope_rotate(k, cos, sin, pair).astype(BF16)
    v_ref[...] = gv_ref[...].astype(BF16)


def gqa_prep(u, cols, g_q, g_k, cos, sin, *, tm, q_scale):
    t = u.shape[0]
    (q0, qw), (k0, kw), (v0, vw) = cols['gq'], cols['gk'], cols['gv']
    row = lambda i: (i, 0)
    return pl.pallas_call(
        functools.partial(_gqa_prep_kernel, q_scale=q_scale),
        grid=(t // tm,),
        in_specs=[pl.BlockSpec((tm, qw), lambda i: (i, q0 // qw)),
                  pl.BlockSpec((tm, kw), lambda i: (i, k0 // kw)),
                  pl.BlockSpec((tm, vw), lambda i: (i, v0 // vw)),
                  pl.BlockSpec((1, HEAD_DIM), lambda i: (0, 0)),
                  pl.BlockSpec((1, HEAD_DIM), lambda i: (0, 0)),
                  pl.BlockSpec((tm, LANE), row),
                  pl.BlockSpec((tm, LANE), row)],
        out_specs=[pl.BlockSpec((tm, qw), row), pl.BlockSpec((tm, kw), row),
                   pl.BlockSpec((tm, kw), row), pl.BlockSpec((tm, vw), row)],
        out_shape=[jax.ShapeDtypeStruct((t, qw), BF16), jax.ShapeDtypeStruct((t, kw), BF16),
                   jax.ShapeDtypeStruct((t, kw), F32), jax.ShapeDtypeStruct((t, vw), BF16)],
        compiler_params=_params("parallel"),
        name="gqa_prep",
    )(u, u, u, g_q.reshape(1, HEAD_DIM), g_k.reshape(1, HEAD_DIM), cos, sin)


def _mla_q_kernel(cq_ref, g_cq_ref, wq_ref, g_q_ref, cos_ref, sin_ref, q_ref, *, q_scale):
    cqn = _head_rms(cq_ref[...], g_cq_ref[...]).astype(BF16)
    q = jnp.dot(cqn, wq_ref[...], preferred_element_type=F32)
    cos, sin = cos_ref[...], sin_ref[...]
    g_nope, g_rope = g_q_ref[:, :MLA_NOPE], g_q_ref[:, MLA_NOPE:]
    for h in range(q.shape[1] // MLA_PAD):
        nope = q[:, h * MLA_PAD:h * MLA_PAD + MLA_NOPE]
        rope = q[:, h * MLA_PAD + MLA_NOPE:(h + 1) * MLA_PAD]
        ss = jnp.sum(nope * nope, axis=-1, keepdims=True) + jnp.sum(rope * rope, axis=-1, keepdims=True)
        r = lax.rsqrt(ss / MLA_QK + EPS) * q_scale
        q_ref[:, h * MLA_PAD:h * MLA_PAD + MLA_NOPE] = (nope * r * g_nope).astype(BF16)
        q_ref[:, h * MLA_PAD + MLA_NOPE:(h + 1) * MLA_PAD] = (
            _rope_rotate(rope * g_rope, cos, sin, MLA_ROPE // 4) * r).astype(BF16)


def mla_queries(u, cols, g_cq, wq, g_q_pad, cos, sin, *, tm, q_scale):
    t = u.shape[0]
    c0, cw = cols['cq']
    n = wq.shape[1]
    return pl.pallas_call(
        functools.partial(_mla_q_kernel, q_scale=q_scale),
        grid=(t // tm,),
        in_specs=[pl.BlockSpec((tm, cw), lambda i: (i, c0 // cw)),
                  pl.BlockSpec((1, cw), lambda i: (0, 0)),
                  pl.BlockSpec((cw, n), lambda i: (0, 0)),
                  pl.BlockSpec((1, MLA_PAD), lambda i: (0, 0)),
                  pl.BlockSpec((tm, LANE), lambda i: (i, 0)),
                  pl.BlockSpec((tm, LANE), lambda i: (i, 0))],
        out_specs=pl.BlockSpec((tm, n), lambda i: (i, 0)),
        out_shape=jax.ShapeDtypeStruct((t, n), BF16),
        compiler_params=_params("parallel"),
        name="mla_queries",
    )(u, g_cq.reshape(1, cw), wq, g_q_pad.reshape(1, MLA_PAD), cos, sin)


def _mla_kv_kernel(ckv_ref, kr_ref, g_ckv_ref, wkv_ref, g_k_ref, cos_ref, sin_ref, ckvn_ref, k_ref, v_ref, *,
                   heads, normalise):
    ckv = ckv_ref[...]
    if normalise:
        ckv = _head_rms(ckv, g_ckv_ref[...])
    ckvn_ref[...] = ckv
    kv = jnp.dot(ckv.astype(BF16), wkv_ref[...], preferred_element_type=F32)
    lane = lax.broadcasted_iota(jnp.int32, (ckv.shape[0], LANE), 1)
    krope = jnp.where(lane < MLA_ROPE, kr_ref[:, :LANE], 0.0)
    ss_rope = jnp.sum(krope * krope, axis=-1, keepdims=True)
    g_nope, g_rope = g_k_ref[:, :MLA_NOPE], g_k_ref[:, MLA_NOPE:]
    rope = _rope_rotate(krope * g_rope, cos_ref[...], sin_ref[...], MLA_ROPE // 4)
    for h in range(heads):
        nope = kv[:, h * MLA_NOPE:(h + 1) * MLA_NOPE]
        r = lax.rsqrt((jnp.sum(nope * nope, axis=-1, keepdims=True) + ss_rope) / MLA_QK + EPS)
        k_ref[:, h * MLA_PAD:h * MLA_PAD + MLA_NOPE] = (nope * r * g_nope).astype(BF16)
        k_ref[:, h * MLA_PAD + MLA_NOPE:(h + 1) * MLA_PAD] = (rope * r).astype(BF16)
    v_ref[...] = kv[:, heads * MLA_NOPE:].astype(BF16)


def mla_keys_values(ckv_src, ckv_col, kr_src, kr_col, g_ckv, wkv, g_k_pad, cos, sin, *, heads, tm, normalise):
    t = ckv_src.shape[0]
    (c0, cw), (r0, rw) = ckv_col, kr_col
    return pl.pallas_call(
        functools.partial(_mla_kv_kernel, heads=heads, normalise=normalise),
        grid=(t // tm,),
        in_specs=[pl.BlockSpec((tm, cw), lambda i: (i, c0 // cw)),
                  pl.BlockSpec((tm, rw), lambda i: (i, r0 // rw)),
                  pl.BlockSpec((1, cw), lambda i: (0, 0)),
                  pl.BlockSpec((cw, wkv.shape[1]), lambda i: (0, 0)),
                  pl.BlockSpec((1, MLA_PAD), lambda i: (0, 0)),
                  pl.BlockSpec((tm, LANE), lambda i: (i, 0)),
                  pl.BlockSpec((tm, LANE), lambda i: (i, 0))],
        out_specs=[pl.BlockSpec((tm, cw), lambda i: (i, 0)),
                   pl.BlockSpec((tm, heads * MLA_PAD), lambda i: (i, 0)),
                   pl.BlockSpec((tm, heads * MLA_V), lambda i: (i, 0))],
        out_shape=[jax.ShapeDtypeStruct((t, cw), F32), jax.ShapeDtypeStruct((t, heads * MLA_PAD), BF16),
                   jax.ShapeDtypeStruct((t, heads * MLA_V), BF16)],
        compiler_params=_params("parallel"),
        name="mla_keys_values",
    )(ckv_src, kr_src, g_ckv.reshape(1, cw), wkv, g_k_pad.reshape(1, MLA_PAD), cos, sin)


def _gdn_prep_kernel(x_ref, prev_ref, next_ref, w_ref, o_ref, *, heads, tm, n_prompt_tiles, seq_lens):
    i = pl.program_id(0)
    x = x_ref[...]
    row = lax.broadcasted_iota(jnp.int32, (tm, 1), 0)
    in_prompt = i < n_prompt_tiles
    seq_len = jnp.where(in_prompt, seq_lens[0], seq_lens[1])
    first_row = jnp.where(in_prompt, i, i - n_prompt_tiles) * tm
    span = [min(s, tm) for s in seq_lens]
    local = jnp.where(in_prompt, row & (span[0] - 1), row & (span[1] - 1))
    span_t = jnp.where(in_prompt, span[0], span[1])
    tile_starts_seq = lax.rem(first_row, seq_len) == 0
    tile_ends_seq = lax.rem(first_row + tm, seq_len) == 0
    is_start = jnp.logical_and(local == 0, jnp.logical_or(row != 0, tile_starts_seq))
    is_end = jnp.logical_and(local == span_t - 1, jnp.logical_or(row != tm - 1, tile_ends_seq))
    prev = jnp.where(row == 0, prev_ref[7:8, :], pltpu.roll(x, 1, axis=0))
    nxt = jnp.where(row == tm - 1, next_ref[0:1, :], pltpu.roll(x, tm - 1, axis=0))
    prev = jnp.where(is_start, 0.0, prev)
    nxt = jnp.where(is_end, 0.0, nxt)
    conv = prev * w_ref[0:1, :] + x * w_ref[1:2, :] + nxt * w_ref[2:3, :]
    act = conv * jax.nn.sigmoid(conv)
    for h in range(3 * heads):
        hl = slice(h * GDN_DK, (h + 1) * GDN_DK)
        slab = act[:, hl]
        if h < 2 * heads:
            slab = slab * lax.rsqrt(jnp.sum(slab * slab, axis=-1, keepdims=True) + EPS)
            if h < heads:
                slab = slab * GDN_DK ** -0.5
        o_ref[:, hl] = slab


def gdn_prep(u, col, w_conv, *, heads, tm, t_prompt, seq_lens):
    t = u.shape[0]
    c0, cw = col
    assert all(s & (s - 1) == 0 for s in (tm,) + tuple(min(s, tm) for s in seq_lens))
    eight = 8
    cb = c0 // cw
    last8 = t // eight - 1
    return pl.pallas_call(
        functools.partial(_gdn_prep_kernel, heads=heads, tm=tm, n_prompt_tiles=t_prompt // tm, seq_lens=seq_lens),
        grid=(t // tm,),
        in_specs=[pl.BlockSpec((tm, cw), lambda i: (i, cb)),
                  pl.BlockSpec((eight, cw), lambda i: (jnp.maximum(i * (tm // eight) - 1, 0), cb)),
                  pl.BlockSpec((eight, cw), lambda i: (jnp.minimum((i + 1) * (tm // eight), last8), cb)),
                  pl.BlockSpec((3, cw), lambda i: (0, 0))],
        out_specs=pl.BlockSpec((tm, cw), lambda i: (i, 0)),
        out_shape=jax.ShapeDtypeStruct((t, cw), F32),
        compiler_params=_params("parallel"),
        name="gdn_prep",
    )(u, u, u, w_conv)


def _attn_kernel(*refs, n_seg, aliased, group, dq, dv, tk):
    q_ref = refs[0]
    kv_refs = [(refs[1 + 2 * s], refs[2 + 2 * s]) for s in range(n_seg)]
    o_ref, s_ref, mx_ref, l_ref, acc_ref = refs[1 + 2 * n_seg + (1 if aliased else 0):]
    tq = q_ref.shape[0]
    q = jnp.concatenate([q_ref[:, g * dq:(g + 1) * dq] for g in range(group)], axis=0)
    slabs = tk // LANE

    def for_chunks(body):
        base = 0
        for k_ref, v_ref in kv_refs:
            nk = k_ref.shape[0] // tk
            main = nk // ATTN_UNROLL

            def group_body(i, carry, k_ref=k_ref, v_ref=v_ref, base=base):
                for u in range(ATTN_UNROLL):
                    body(k_ref, v_ref, i * ATTN_UNROLL + u, base)
                return carry
            if main:
                lax.fori_loop(0, main, group_body, 0)
            for c in range(main * ATTN_UNROLL, nk):
                body(k_ref, v_ref, c, base)
            base += nk

    mx_ref[...] = jnp.full(mx_ref.shape, -jnp.inf, F32)

    def rows_of(c):
        return pl.ds(c * tk if isinstance(c, int) else pl.multiple_of(c * tk, tk), tk)

    def scores(k_ref, v_ref, c, base):
        s = lax.dot_general(q, k_ref[rows_of(c), :], (((1,), (1,)), ((), ())),
                            preferred_element_type=F32)
        s_ref[base + c] = s
        mx = mx_ref[...]
        for j in range(slabs):
            mx = jnp.maximum(mx, s[:, j * LANE:(j + 1) * LANE])
        mx_ref[...] = mx

    for_chunks(scores)
    m = jnp.broadcast_to(jnp.max(mx_ref[...], axis=-1, keepdims=True), mx_ref.shape)
    l_ref[...] = jnp.zeros_like(l_ref)
    acc_ref[...] = jnp.zeros_like(acc_ref)

    def weighted(k_ref, v_ref, c, base):
        s = s_ref[base + c]
        p = jnp.concatenate([jnp.exp2(s[:, j * LANE:(j + 1) * LANE] - m) for j in range(slabs)], axis=1)
        lsum = l_ref[...]
        for j in range(slabs):
            lsum = lsum + p[:, j * LANE:(j + 1) * LANE]
        l_ref[...] = lsum
        acc_ref[...] += jnp.dot(p.astype(BF16), v_ref[rows_of(c), :], preferred_element_type=F32)

    for_chunks(weighted)
    o = acc_ref[...] / jnp.sum(l_ref[...], axis=-1, keepdims=True)
    for g in range(group):
        o_ref[:, g * dv:(g + 1) * dv] = o[g * tq:(g + 1) * tq].astype(o_ref.dtype)


def attention(q, q_row0, kv_segments, *, batch, lq, kv_heads, group, dq, dv, out_rows, out_row0, prev_out, name):
    tq = _pick(lq, (512, 256, 128)) // (2 if group > 1 else 1)
    tk = _pick(math.gcd(*[length for _, _, _, length in kv_segments]), (256, 128))
    nq = lq // tq
    m_rows = group * tq
    assert q_row0 % tq == 0 and out_row0 % tq == 0
    in_specs = [pl.BlockSpec((tq, group * dq), lambda b, h, i: (q_row0 // tq + b * nq + i, h))]
    operands = [q]
    for k, v, first_row, length in kv_segments:
        assert first_row % length == 0
        in_specs += [pl.BlockSpec((length, dq), lambda b, h, i, f=first_row // length: (f + b, h)),
                     pl.BlockSpec((length, dv), lambda b, h, i, f=first_row // length: (f + b, h))]
        operands += [k, v]
    aliases = {}
    if prev_out is not None:
        in_specs.append(pl.BlockSpec(memory_space=pl.ANY))
        aliases = {len(operands): 0}
        operands.append(prev_out)
    n_chunks = sum(length // tk for _, _, _, length in kv_segments)
    return pl.pallas_call(
        functools.partial(_attn_kernel, n_seg=len(kv_segments), aliased=prev_out is not None, group=group,
                          dq=dq, dv=dv, tk=tk),
        grid=(batch, kv_heads, nq),
        in_specs=in_specs,
        out_specs=pl.BlockSpec((tq, group * dv), lambda b, h, i: (out_row0 // tq + b * nq + i, h)),
        out_shape=jax.ShapeDtypeStruct((out_rows, kv_heads * group * dv), BF16),
        input_output_aliases=aliases,
        scratch_shapes=[pltpu.VMEM((n_chunks, m_rows, tk), F32), pltpu.VMEM((m_rows, LANE), F32),
                        pltpu.VMEM((m_rows, LANE), F32), pltpu.VMEM((m_rows, dv), F32)],
        compiler_params=_params("parallel", "parallel", "parallel"),
        name=name,
    )(*operands)


def _split_hi_lo(a):
    hi = a.astype(BF16).astype(F32)
    return hi, (a - hi).astype(BF16).astype(F32)


def _dot_split(a, b):
    ah, al = _split_hi_lo(a)
    bh, bl = _split_hi_lo(b)
    lhs = jnp.concatenate([ah, al, ah, al], axis=2).astype(BF16)
    rhs = jnp.concatenate([bh, bh, bl, bl], axis=1).astype(BF16)
    return lax.dot_general(lhs, rhs, (((2,), (1,)), ((0,), (0,))), preferred_element_type=F32)


def _dot_nt(a, b):
    return lax.dot_general(a.astype(BF16), b.astype(BF16), (((2,), (2,)), ((0,), (0,))), preferred_element_type=F32)


def _chunk_masks(backward):
    r = lax.broadcasted_iota(jnp.int32, (GDN_CHUNK, GDN_CHUNK), 0)
    c = lax.broadcasted_iota(jnp.int32, (GDN_CHUNK, GDN_CHUNK), 1)
    incl = (r <= c) if backward else (r >= c)
    strict = (r < c) if backward else (r > c)
    return r, c, incl, strict


def _decay_mask(dcol, r, c, incl):
    drow = jnp.sum(jnp.where(r == c, dcol, 0.0), axis=1, keepdims=True)
    return jnp.where(incl, jnp.exp(jnp.where(incl, dcol - drow, 0.0)), 0.0)


def _unit_tri_inverse(a_mat, r, c):
    eye = (r == c).astype(F32)
    same16 = jnp.right_shift(r, 4) == jnp.right_shift(c, 4)
    same32 = jnp.right_shift(r, 5) == jnp.right_shift(c, 5)
    x = jnp.where(same16, -a_mat, 0.0)
    inv = eye + x
    for _ in range(3):
        x = _dot_split(x, x)
        inv = inv + _dot_split(inv, x)
    for off_blocks in (jnp.logical_and(same32, jnp.logical_not(same16)), jnp.logical_not(same32)):
        off = jnp.where(off_blocks, a_mat, 0.0)
        inv = inv - _dot_split(inv, _dot_split(off, inv))
    return inv


def _head_column(ref, lane, index, n_c):
    col = jnp.sum(jnp.where(lane == index, ref[...], 0.0), axis=1, keepdims=True)
    return col.reshape(n_c, GDN_CHUNK, 1)


def _gdn_solve_kernel(k_ref, v_ref, d_ref, b_ref, uw_ref, *, head_lanes):
    h = pl.program_id(1)
    n_c = k_ref.shape[0] // GDN_CHUNK
    lane = lax.broadcasted_iota(jnp.int32, (1, d_ref.shape[1]), 1)
    kc = k_ref[...].reshape(n_c, GDN_CHUNK, GDN_DK)
    vc = v_ref[...].reshape(n_c, GDN_CHUNK, GDN_DV)
    for direction in range(2):
        r, c, incl, strict = _chunk_masks(direction == 1)
        dcol = _head_column(d_ref, lane, direction * head_lanes + h, n_c)
        bcol = _head_column(b_ref, lane, direction * head_lanes + h, n_c)
        kb = kc * bcol
        a_mat = jnp.where(strict, _dot_nt(kb, kc) * _decay_mask(dcol, r, c, incl), 0.0)
        rhs = jnp.concatenate([vc * bcol, kb * jnp.exp(dcol)], axis=2)
        uw = _dot_split(_unit_tri_inverse(a_mat, r, c), rhs)
        uw_ref[direction] = uw.reshape(n_c * GDN_CHUNK, GDN_DV + GDN_DK)


def gdn_solve(qkv, dcs, beta, *, heads):
    t = qkv.shape[0]
    rows = _pick(t, (512, 256, 128, 64))
    width = GDN_DV + GDN_DK
    return pl.pallas_call(
        functools.partial(_gdn_solve_kernel, head_lanes=heads),
        grid=(t // rows, heads),
        in_specs=[pl.BlockSpec((rows, GDN_DK), lambda i, h: (i, heads + h)),
                  pl.BlockSpec((rows, GDN_DV), lambda i, h: (i, 2 * heads + h)),
                  pl.BlockSpec((rows, 2 * heads), lambda i, h: (i, 0)),
                  pl.BlockSpec((rows, 2 * heads), lambda i, h: (i, 0))],
        out_specs=pl.BlockSpec((2, rows, width), lambda i, h: (0, i, h)),
        out_shape=jax.ShapeDtypeStruct((2, t, heads * width), F32),
        compiler_params=_params("parallel", "parallel"),
        name="gdn_solve",
    )(qkv, qkv, dcs, beta)


def _gdn_scan_kernel(fwd_blk_ref, bwd_blk_ref, seq_ref, edge_ref, qf_ref, kf_ref, uwf_ref, df_ref,
                     qb_ref, kb_ref, uwb_ref, db_ref, s0_ref, of_ref, ob_ref, sfin_ref, s_ref, *,
                     heads_per_step, head_lanes):
    step = pl.program_id(1)
    hg = pl.program_id(0)
    rows = qf_ref.shape[0]
    n_c = rows // GDN_CHUNK
    lane = lax.broadcasted_iota(jnp.int32, (1, df_ref.shape[1]), 1)

    @pl.when((edge_ref[step] & 1) != 0)
    def _():
        s_ref[...] = s0_ref[...].astype(F32)

    def bmm(a, b):
        return lax.dot_general(a, b, (((2,), (1,)), ((0,), (0,))), preferred_element_type=F32)

    width = GDN_DV + GDN_DK
    streams = ((qf_ref, kf_ref, uwf_ref, df_ref, of_ref), (qb_ref, kb_ref, uwb_ref, db_ref, ob_ref))
    for ci in range(n_c):
        for direction, (q_ref, k_ref, uw_ref, d_ref, o_ref) in enumerate(streams):
            backward = direction == 1
            r, c, incl, _ = _chunk_masks(backward)
            pos = (n_c - 1 - ci) if backward else ci
            sl = slice(pos * GDN_CHUNK, (pos + 1) * GDN_CHUNK)
            heads = range(heads_per_step)
            dcol = jnp.stack([jnp.sum(jnp.where(lane == direction * head_lanes + hg * heads_per_step + hh,
                                                d_ref[sl, :], 0.0), axis=1, keepdims=True) for hh in heads])
            qc = jnp.stack([q_ref[sl, hh * GDN_DK:(hh + 1) * GDN_DK] for hh in heads])
            kc = jnp.stack([k_ref[sl, hh * GDN_DK:(hh + 1) * GDN_DK] for hh in heads])
            u = jnp.stack([uw_ref[sl, hh * width:hh * width + GDN_DV] for hh in heads])
            w = jnp.stack([uw_ref[sl, hh * width + GDN_DV:(hh + 1) * width] for hh in heads])
            attn = _dot_nt(qc, kc) * _decay_mask(dcol, r, c, incl)
            d_last = dcol[:, 0:1, :] if backward else dcol[:, GDN_CHUNK - 1:GDN_CHUNK, :]
            q_dec = (qc * jnp.exp(dcol)).astype(BF16)
            k_dec = (kc * jnp.exp(d_last - dcol)).astype(BF16)
            s = s_ref[direction]
            s_b = s.astype(BF16)
            v_new = u - bmm(w.astype(BF16), s_b)
            v_b = v_new.astype(BF16)
            o = bmm(q_dec, s_b) + bmm(attn.astype(BF16), v_b)
            for hh in heads:
                o_ref[sl, hh * GDN_DV:(hh + 1) * GDN_DV] = o[hh]
            s_ref[direction] = s * jnp.exp(d_last) + lax.dot_general(
                k_dec, v_b, (((1,), (1,)), ((0,), (0,))), preferred_element_type=F32)

    @pl.when((edge_ref[step] & 2) != 0)
    def _():
        sfin_ref[...] = s_ref[...]


def gdn_scan(qkv, uw, dcs, s0, *, sequences, heads):
    t = qkv.shape[0]
    rows = _pick(math.gcd(*[length for _, length in sequences]), (256, 128, 64))
    hps = _pick(heads, (4, 2, 1))
    hgroups = heads // hps
    width = GDN_DV + GDN_DK
    fwd_blk, bwd_blk, seq_id, edge = [], [], [], []
    for s, (first_row, length) in enumerate(sequences):
        assert first_row % rows == 0 and length % rows == 0
        n_l = length // rows
        for i in range(n_l):
            fwd_blk.append(first_row // rows + i)
            bwd_blk.append(first_row // rows + n_l - 1 - i)
            seq_id.append(s)
            edge.append((1 if i == 0 else 0) | (2 if i == n_l - 1 else 0))
    assert sorted(fwd_blk) == list(range(t // rows))
    tables = [jnp.asarray(np.asarray(v, np.int32)) for v in (fwd_blk, bwd_blk, seq_id, edge)]

    def stream_specs(table, direction):
        return [pl.BlockSpec((rows, hps * GDN_DK), lambda g, s, fb, bb, sq, ed: ((fb, bb)[table][s], g)),
                pl.BlockSpec((rows, hps * GDN_DK), lambda g, s, fb, bb, sq, ed: ((fb, bb)[table][s], hgroups + g)),
                pl.BlockSpec((None, rows, hps * width), lambda g, s, fb, bb, sq, ed: (direction, (fb, bb)[table][s], g)),
                pl.BlockSpec((rows, 2 * heads), lambda g, s, fb, bb, sq, ed: ((fb, bb)[table][s], 0))]

    state_spec = pl.BlockSpec((None, 2, hps, GDN_DK, GDN_DV), lambda g, s, fb, bb, sq, ed: (sq[s], 0, g, 0, 0))
    grid_spec = pltpu.PrefetchScalarGridSpec(
        num_scalar_prefetch=4,
        grid=(hgroups, len(fwd_blk)),
        in_specs=stream_specs(0, 0) + stream_specs(1, 1) + [state_spec],
        out_specs=[pl.BlockSpec((rows, hps * GDN_DV), lambda g, s, fb, bb, sq, ed: (fb[s], g)),
                   pl.BlockSpec((rows, hps * GDN_DV), lambda g, s, fb, bb, sq, ed: (bb[s], g)),
                   state_spec],
        scratch_shapes=[pltpu.VMEM((2, hps, GDN_DK, GDN_DV), F32)],
    )
    return pl.pallas_call(
        functools.partial(_gdn_scan_kernel, heads_per_step=hps, head_lanes=heads),
        grid_spec=grid_spec,
        out_shape=[jax.ShapeDtypeStruct((t, heads * GDN_DV), F32),
                   jax.ShapeDtypeStruct((t, heads * GDN_DV), F32),
                   jax.ShapeDtypeStruct((len(sequences), 2, heads, GDN_DK, GDN_DV), F32)],
        compiler_params=_params("parallel", "arbitrary"),
        name="gdn_scan",
    )(*tables, qkv, qkv, uw, dcs, qkv, qkv, uw, dcs, s0)


def _row_gather(idx_ref, src_hbm, dst_vmem, sem, n_rows):
    def start():
        def body(r, carry):
            pltpu.make_async_copy(src_hbm.at[pl.ds(idx_ref[0, r], 1), :], dst_vmem.at[pl.ds(r, 1), :], sem).start()
            return carry
        lax.fori_loop(0, n_rows, body, 0, unroll=GATHER_UNROLL)

    def wait():
        def body(r, carry):
            pltpu.make_async_copy(src_hbm.at[pl.ds(0, 1), :], dst_vmem.at[pl.ds(0, 1), :], sem).wait()
            return carry
        lax.fori_loop(0, n_rows, body, 0, unroll=GATHER_UNROLL)
    return start, wait


def _moe_kernel(be_ref, nu_ref, tok_ref, tok_next_ref, h_hbm, wg_ref, wu_ref, wd_ref, o_ref,
                xbuf_ref, xlo_ref, xhi_ref, acc_ref, sem):
    i = pl.program_id(0)
    f = pl.program_id(1)
    n_used = nu_ref[0]
    tm, half = xlo_ref.shape

    @pl.when(i < n_used)
    def _():
        @pl.when(f == 0)
        def _():
            for slot in range(2):
                start_this, wait_this = _row_gather(tok_ref, h_hbm, xbuf_ref.at[slot], sem.at[slot], tm)
                start_next, _ = _row_gather(tok_next_ref, h_hbm, xbuf_ref.at[1 - slot], sem.at[1 - slot], tm)

                @pl.when(lax.rem(i, 2) == slot)
                def _():
                    @pl.when(i == 0)
                    def _():
                        start_this()
                    wait_this()

                    @pl.when(i + 1 < n_used)
                    def _():
                        start_next()
                    lo, hi = _unpack_bf16_pair(xbuf_ref[slot])
                    xlo_ref[...] = lo.astype(BF16)
                    xhi_ref[...] = hi.astype(BF16)
            acc_ref[...] = jnp.zeros_like(acc_ref)

        def x_dot(w_ref):
            return (jnp.dot(xlo_ref[...], w_ref[:half, :].astype(BF16), preferred_element_type=F32)
                    + jnp.dot(xhi_ref[...], w_ref[half:, :].astype(BF16), preferred_element_type=F32))

        gate = x_dot(wg_ref)
        up = x_dot(wu_ref)
        hmid = (gate * jax.nn.sigmoid(gate) * up).astype(BF16)
        acc_ref[...] += jnp.dot(hmid, wd_ref[...].astype(BF16), preferred_element_type=F32)

        @pl.when(f == pl.num_programs(1) - 1)
        def _():
            o_ref[...] = _pack_bf16_pair(acc_ref[:, :half], acc_ref[:, half:])

    @pl.when(jnp.logical_and(i >= n_used, f == 0))
    def _():
        o_ref[...] = jnp.zeros_like(o_ref)


def moe_experts(h_packed, slot_tok, block_e, n_used, w_gate, w_up, w_down, *, tm, layer):
    d = 2 * h_packed.shape[1]
    p = slot_tok.shape[0]
    f_dim = w_gate.shape[3]
    tf = _pick(f_dim, (256, 128))
    n_blocks = p // tm
    last_f = f_dim // tf - 1
    tok_blocks = slot_tok.reshape(n_blocks, 1, tm)

    def blk(i, nu):
        return jnp.minimum(i, nu[0] - 1)

    def f_blk(i, f, nu):
        return jnp.where(i < nu[0], f, last_f)

    grid_spec = pltpu.PrefetchScalarGridSpec(
        num_scalar_prefetch=2,
        grid=(n_blocks, f_dim // tf),
        in_specs=[pl.BlockSpec((None, 1, tm), lambda i, f, be, nu: (blk(i, nu), 0, 0), memory_space=pltpu.SMEM),
                  pl.BlockSpec((None, 1, tm), lambda i, f, be, nu: (blk(i + 1, nu), 0, 0), memory_space=pltpu.SMEM),
                  pl.BlockSpec(memory_space=pl.ANY),
                  pl.BlockSpec((None, None, d, tf), lambda i, f, be, nu: (layer, be[blk(i, nu)], 0, f_blk(i, f, nu))),
                  pl.BlockSpec((None, None, d, tf), lambda i, f, be, nu: (layer, be[blk(i, nu)], 0, f_blk(i, f, nu))),
                  pl.BlockSpec((None, None, tf, d), lambda i, f, be, nu: (layer, be[blk(i, nu)], f_blk(i, f, nu), 0))],
        out_specs=pl.BlockSpec((tm, d // 2), lambda i, f, be, nu: (i, 0)),
        scratch_shapes=[pltpu.VMEM((2, tm, d // 2), jnp.uint32), pltpu.VMEM((tm, d // 2), BF16),
                        pltpu.VMEM((tm, d // 2), BF16), pltpu.VMEM((tm, d), F32),
                        pltpu.SemaphoreType.DMA((2,))],
    )
    return pl.pallas_call(
        _moe_kernel,
        grid_spec=grid_spec,
        out_shape=jax.ShapeDtypeStruct((p, d // 2), jnp.uint32),
        compiler_params=_params("arbitrary", "arbitrary"),
        name="moe_experts",
    )(block_e, n_used, tok_blocks, tok_blocks, h_packed, w_gate, w_up, w_down)


def _moe_combine_kernel(dest_ref, y_hbm, x_ref, gate_ref, mod_ref, o_ref, rows_ref, sem, *, gate_idx):
    tc, d = x_ref.shape
    half = d // 2
    start, wait = _row_gather(dest_ref, y_hbm, rows_ref, sem, 2 * tc)
    start()
    wait()
    lo0, hi0 = _unpack_bf16_pair(rows_ref[:tc, :])
    lo1, hi1 = _unpack_bf16_pair(rows_ref[tc:, :])
    g0 = gate_ref[:, 0:1]
    g1 = gate_ref[:, 1:2]
    o_ref[:, :half] = x_ref[:, :half] + mod_ref[gate_idx:gate_idx + 1, :half] * (lo0 * g0 + lo1 * g1)
    o_ref[:, half:] = x_ref[:, half:] + mod_ref[gate_idx:gate_idx + 1, half:] * (hi0 * g0 + hi1 * g1)


def moe_combine_residual(y_packed, dest, gates, x, mod, *, gate_idx, tc, row_fn):
    t, d = x.shape
    n_tiles = t // tc
    dest_tiles = dest.reshape(n_tiles, tc, TOP_K).transpose(0, 2, 1).reshape(n_tiles, 1, TOP_K * tc)
    return pl.pallas_call(
        functools.partial(_moe_combine_kernel, gate_idx=gate_idx),
        grid=(n_tiles,),
        in_specs=[pl.BlockSpec((None, 1, TOP_K * tc), lambda i: (i, 0, 0), memory_space=pltpu.SMEM),
                  pl.BlockSpec(memory_space=pl.ANY),
                  pl.BlockSpec((tc, d), lambda i: (i, 0)),
                  pl.BlockSpec((tc, gates.shape[1]), lambda i: (i, 0)),
                  pl.BlockSpec((None, mod.shape[1], d), lambda i: (row_fn(i), 0, 0))],
        out_specs=pl.BlockSpec((tc, d), lambda i: (i, 0)),
        out_shape=jax.ShapeDtypeStruct((t, d), F32),
        scratch_shapes=[pltpu.VMEM((TOP_K * tc, d // 2), jnp.uint32), pltpu.SemaphoreType.DMA(())],
        compiler_params=_params("arbitrary"),
        name="moe_combine",
    )(dest_tiles, y_packed, x, gates, mod)


def _rms(x, g):
    return x * lax.rsqrt(jnp.mean(x * x, axis=-1, keepdims=True) + EPS) * g


def _rope_tables(t_prompt, dec_batch, dec_seq, width):
    a = width // 2
    half = a // 2
    freqs = ROPE_THETA ** (-np.arange(half, dtype=np.float32) / half)
    t = np.arange(dec_seq)
    row, col = t // GRID_W, t % GRID_W
    ang_r = row[:, None].astype(np.float32) * freqs[None, :]
    ang_c = col[:, None].astype(np.float32) * freqs[None, :]
    cos = np.concatenate([np.cos(ang_r), np.cos(ang_r), np.cos(ang_c), np.cos(ang_c)], axis=-1)
    sin = np.concatenate([-np.sin(ang_r), np.sin(ang_r), -np.sin(ang_c), np.sin(ang_c)], axis=-1)
    cos = np.concatenate([np.ones((t_prompt, width), np.float32), np.tile(cos, (dec_batch, 1))], axis=0)
    sin = np.concatenate([np.zeros((t_prompt, width), np.float32), np.tile(sin, (dec_batch, 1))], axis=0)
    return jnp.asarray(cos), jnp.asarray(sin)


def _gdn_gates(a, b, a_log, dt_bias, heads):
    t = a.shape[0]
    g = -jnp.exp(a_log.reshape(2 * heads)) * jax.nn.softplus(a + dt_bias.reshape(2 * heads))
    gc = g.reshape(t // GDN_CHUNK, GDN_CHUNK, 2 * heads)
    prefix = jnp.cumsum(gc[..., :heads], axis=1)
    suffix = jnp.flip(jnp.cumsum(jnp.flip(gc[..., heads:], axis=1), axis=1), axis=1)
    dcs = jnp.concatenate([prefix, suffix], axis=-1).reshape(t, 2 * heads)
    return dcs, jax.nn.sigmoid(b)


def _dispatch_plan(experts, n_experts, tm):
    t, k = experts.shape
    a = t * k
    n_blocks = a // tm + n_experts
    flat_e = experts.reshape(-1)
    onehot = (flat_e[:, None] == jnp.arange(n_experts, dtype=jnp.int32)[None, :]).astype(jnp.int32)
    csum = jnp.cumsum(onehot, axis=0)
    pos = jnp.sum((csum - onehot) * onehot, axis=1)
    counts = csum[-1]
    padded = (counts + tm - 1) // tm * tm
    pend = jnp.cumsum(padded)
    pstart = pend - padded
    dest = pstart[flat_e] + pos
    flat_tok = jnp.arange(a, dtype=jnp.int32) // k
    slot_tok = jnp.zeros((n_blocks * tm,), jnp.int32).at[dest].set(flat_tok)
    block_e = jnp.minimum(jnp.searchsorted(pend, jnp.arange(n_blocks, dtype=jnp.int32) * tm, side='right'),
                          n_experts - 1).astype(jnp.int32)
    n_used = (pend[-1] // tm).astype(jnp.int32).reshape(1)
    return dest.reshape(t, k), slot_tok, block_e, n_used


def kernel(x_prompt, x_sample, cache_mla_ckv, cache_mla_krope, cache_gqa_k, cache_gqa_v, state_gdn, c, c_ctx, w_mod, b_mod, g_norm_mix, g_norm_ffn, w_in, g_cq, w_q_up, g_ckv, w_kv_up, g_q_mla, g_k_mla, g_q_gqa, g_k_gqa, w_conv, a_log, dt_bias, g_gdn_out, w_out, w_router_group, b_router_group, w_router_expert, b_router_expert, w_gate, w_up, w_down):
    batch, seq, d = x_prompt.shape
    dec_batch, dec_seq, _ = x_sample.shape
    depth = w_in.shape[0]
    past = cache_mla_ckv.shape[2]
    q_lora = w_q_up.shape[1]
    kv_lora = w_kv_up.shape[1]
    mla_heads = w_q_up.shape[2] // MLA_QK
    kv_heads = cache_gqa_k.shape[3]
    gdn_heads = state_gdn.shape[3]
    gqa_heads = (w_out.shape[1] - mla_heads * MLA_V - gdn_heads * GDN_DV) // HEAD_DIM
    group = gqa_heads // kv_heads
    n_groups = w_router_group.shape[2]
    n_experts = w_gate.shape[1]
    gdn_qkv = gdn_heads * (2 * GDN_DK + GDN_DV)
    gdn_width = gdn_heads * GDN_DV

    t_p = batch * seq
    t_s = dec_batch * dec_seq
    t_all = t_p + t_s
    tm = _pick(math.gcd(t_p, dec_seq), (512, 256, 128))
    row_fn = _mod_row_fn(tm, t_p, dec_seq)
    moe_tm = _pick(t_all * TOP_K, (512, 256, 128))
    combine_tc = _pick(math.gcd(t_p, dec_seq), (256, 128))

    src_sizes = (q_lora, kv_lora, MLA_ROPE, gqa_heads * HEAD_DIM, kv_heads * HEAD_DIM, kv_heads * HEAD_DIM,
                 gdn_qkv, gdn_width, 2 * gdn_heads, 2 * gdn_heads)
    src_off = np.concatenate([[0], np.cumsum(src_sizes)])
    src = dict(zip(('cq', 'ckv', 'krope', 'gq', 'gk', 'gv', 'qkv', 'z', 'a', 'b'),
                   [(int(src_off[j]), int(src_off[j + 1])) for j in range(len(src_sizes))]))
    tail_w = -(-(MLA_ROPE + 4 * gdn_heads) // LANE) * LANE
    cols, pieces, pos = {}, [], 0
    for name, width in (('qkv', gdn_qkv), ('z', gdn_width), ('ckv', kv_lora), ('gq', gqa_heads * HEAD_DIM),
                        ('cq', q_lora), ('tail', tail_w), ('gk', kv_heads * HEAD_DIM), ('gv', kv_heads * HEAD_DIM)):
        start = -(-pos // width) * width
        assert width % LANE == 0
        if start > pos:
            pieces.append(('pad', start - pos))
        pieces.append((name, width))
        cols[name] = (start, width)
        pos = start + width
    n_in = pos
    tail0 = cols['tail'][0]

    def build_w_in(w):
        parts = []
        for name, width in pieces:
            if name == 'pad':
                parts.append(jnp.zeros((d, width), F32))
            elif name == 'tail':
                parts += [w[:, src[s][0]:src[s][1]] for s in ('krope', 'a', 'b')]
                parts.append(jnp.zeros((d, width - MLA_ROPE - 4 * gdn_heads), F32))
            else:
                parts.append(w[:, src[name][0]:src[name][1]])
        return jnp.concatenate(parts, axis=1).astype(BF16)

    def padded_rope_tables(width):
        cos, sin = _rope_tables(t_p, dec_batch, dec_seq, width)
        return (jnp.pad(cos, ((0, 0), (0, LANE - width)), constant_values=1.0),
                jnp.pad(sin, ((0, 0), (0, LANE - width))))

    cos_g, sin_g = padded_rope_tables(HEAD_DIM)
    cos_m, sin_m = padded_rope_tables(MLA_ROPE)
    n_ctx = dec_batch * past
    cos_id, sin_id = jnp.ones((n_ctx, LANE), F32), jnp.zeros((n_ctx, LANE), F32)
    sequences = ([(b * seq, seq) for b in range(batch)]
                 + [(t_p + b * dec_seq, dec_seq) for b in range(dec_batch)])

    cond8 = jnp.zeros((8, d), F32).at[0].set(c_ctx).at[1:1 + dec_batch].set(c)

    x = jnp.concatenate([x_prompt.reshape(t_p, d), x_sample.reshape(t_s, d)], axis=0)
    new_ckv, new_krope, new_k, new_v, new_s = [], [], [], [], []
    q_scale_m = MLA_QK ** -0.5 * LOG2E
    q_scale_g = HEAD_DIM ** -0.5 * LOG2E

    for l in range(depth):
        mod = modulation(cond8, w_mod, b_mod, l).reshape(8, 6, d)

        u = norm_matmul(x, g_norm_mix[l], mod, build_w_in(w_in[l]), sh_idx=0, sc_idx=1, tm=tm, row_fn=row_fn)

        wq = w_q_up[l].reshape(q_lora, mla_heads, MLA_QK)
        wq = jnp.pad(wq, ((0, 0), (0, 0), (0, MLA_PAD - MLA_QK))).reshape(q_lora, mla_heads * MLA_PAD).astype(BF16)
        q_m = mla_queries(u, cols, g_cq[l], wq, jnp.pad(g_q_mla[l], (0, MLA_PAD - MLA_QK)), cos_m, sin_m,
                          tm=tm, q_scale=q_scale_m)
        wkv = w_kv_up[l].reshape(kv_lora, mla_heads, MLA_NOPE + MLA_V)
        wkv = jnp.concatenate([wkv[..., :MLA_NOPE].reshape(kv_lora, -1), wkv[..., MLA_NOPE:].reshape(kv_lora, -1)],
                              axis=1).astype(BF16)
        g_k_pad = jnp.pad(g_k_mla[l], (0, MLA_PAD - MLA_QK))
        ckv_n, k_m, v_m = mla_keys_values(u, cols['ckv'], u, cols['tail'], g_ckv[l], wkv, g_k_pad, cos_m, sin_m,
                                          heads=mla_heads, tm=tm, normalise=True)
        ctx_krope = jnp.pad(cache_mla_krope[:, l].reshape(n_ctx, MLA_ROPE), ((0, 0), (0, LANE - MLA_ROPE)))
        _, k_m_ctx, v_m_ctx = mla_keys_values(cache_mla_ckv[:, l].reshape(n_ctx, kv_lora), (0, kv_lora), ctx_krope,
                                              (0, LANE), g_ckv[l], wkv, g_k_pad, cos_id, sin_id, heads=mla_heads,
                                              tm=_pick(n_ctx, (512, 256, 128)), normalise=False)

        mla_common = dict(kv_heads=mla_heads, group=1, dq=MLA_PAD, dv=MLA_V, out_rows=t_all)
        o_m = attention(q_m, 0, [(k_m, v_m, 0, seq)], batch=batch, lq=seq, out_row0=0,
                        prev_out=jnp.zeros((t_all, mla_heads * MLA_V), BF16), name="mla_attn_ctx", **mla_common)
        o_m = attention(q_m, t_p, [(k_m_ctx, v_m_ctx, 0, past), (k_m, v_m, t_p, dec_seq)], batch=dec_batch,
                        lq=dec_seq, out_row0=t_p, prev_out=o_m, name="mla_attn_lat", **mla_common)

        q_g, k_g_rot, k_g, v_g_b = gqa_prep(u, cols, g_q_gqa[l], g_k_gqa[l], cos_g, sin_g, tm=tm, q_scale=q_scale_g)
        ctx_k = cache_gqa_k[:, l].reshape(n_ctx, kv_heads * HEAD_DIM).astype(BF16)
        ctx_v = cache_gqa_v[:, l].reshape(n_ctx, kv_heads * HEAD_DIM).astype(BF16)
        gqa_common = dict(kv_heads=kv_heads, group=group, dq=HEAD_DIM, dv=HEAD_DIM, out_rows=t_all)
        o_g = attention(q_g, 0, [(k_g_rot, v_g_b, 0, seq)], batch=batch, lq=seq, out_row0=0,
                        prev_out=jnp.zeros((t_all, gqa_heads * HEAD_DIM), BF16), name="gqa_attn_ctx", **gqa_common)
        o_g = attention(q_g, t_p, [(ctx_k, ctx_v, 0, past), (k_g_rot, v_g_b, t_p, dec_seq)], batch=dec_batch,
                        lq=dec_seq, out_row0=t_p, prev_out=o_g, name="gqa_attn_lat", **gqa_common)

        qkv_c = gdn_prep(u, cols['qkv'], w_conv[l], heads=gdn_heads, tm=tm, t_prompt=t_p, seq_lens=(seq, dec_seq))
        tail = u[:, tail0:tail0 + MLA_ROPE + 4 * gdn_heads]
        dcs, beta = _gdn_gates(tail[:, MLA_ROPE:MLA_ROPE + 2 * gdn_heads], tail[:, MLA_ROPE + 2 * gdn_heads:],
                               a_log[l], dt_bias[l], gdn_heads)
        uw = gdn_solve(qkv_c, dcs, beta, heads=gdn_heads)
        s0 = jnp.concatenate([jnp.zeros((batch, 2, gdn_heads, GDN_DK, GDN_DV), F32), state_gdn[:, l]], axis=0)
        o_fwd, o_bwd, s_fin = gdn_scan(qkv_c, uw, dcs, s0, sequences=sequences, heads=gdn_heads)

        new_ckv.append(ckv_n[:t_p].reshape(batch, seq, kv_lora))
        new_krope.append(tail[:t_p, :MLA_ROPE].reshape(batch, seq, MLA_ROPE))
        new_k.append(k_g[:t_p].reshape(batch, seq, kv_heads, HEAD_DIM))
        new_v.append(u[:t_p, cols['gv'][0]:cols['gv'][0] + cols['gv'][1]].reshape(batch, seq, kv_heads, HEAD_DIM))
        new_s.append(s_fin[:batch])

        x = out_proj_residual(o_m, o_g, o_fwd, o_bwd, u, cols['z'], g_gdn_out[l], w_out[l].astype(BF16), x, mod,
                              gate_idx=2, tm=tm, row_fn=row_fn)

        w_router = jnp.concatenate([w_router_group[l], w_router_expert[l],
                                    jnp.zeros((d, LANE - n_groups - n_experts), F32)], axis=1)
        b_router = jnp.concatenate([b_router_group[l], b_router_expert[l],
                                    jnp.zeros((LANE - n_groups - n_experts,), F32)]).reshape(1, LANE)
        h2, gates, experts = ffn_norm_router(x, g_norm_ffn[l], mod, w_router, b_router, sh_idx=3, sc_idx=4, tm=tm,
                                             row_fn=row_fn, n_groups=n_groups, epg=n_experts // n_groups)
        dest, slot_tok, block_e, n_used = _dispatch_plan(experts[:, :TOP_K], n_experts, moe_tm)
        yb = moe_experts(h2, slot_tok, block_e, n_used, w_gate, w_up, w_down, tm=moe_tm, layer=l)
        x = moe_combine_residual(yb, dest, gates, x, mod, gate_idx=5, tc=combine_tc,
                                 row_fn=_mod_row_fn(combine_tc, t_p, dec_seq))

    return (x[:t_p].reshape(batch, seq, d), x[t_p:].reshape(dec_batch, dec_seq, d),
            jnp.stack(new_ckv, axis=1), jnp.stack(new_krope, axis=1), jnp.stack(new_k, axis=1),
            jnp.stack(new_v, axis=1), jnp.stack(new_s, axis=1))
```

```python
import functools
import math

import numpy as np
import jax
import jax.numpy as jnp
from jax import lax
from jax.experimental import pallas as pl
from jax.experimental.pallas import tpu as pltpu

F32 = jnp.float32
BF16 = jnp.bfloat16

EPS = 1e-6
ROPE_THETA = 10000.0
GRID_W = 64
HEAD_DIM = 128
MLA_NOPE = 128
MLA_ROPE = 64
MLA_V = 128
MLA_QK = MLA_NOPE + MLA_ROPE
MLA_PAD = 256
GDN_DK = 128
GDN_DV = 128
GDN_CHUNK = 64
TOP_K = 2
LANE = 128
VMEM_LIMIT_BYTES = 56 * 1024 * 1024
LOG2E = 1.4426950408889634
ATTN_UNROLL = 4
GATHER_UNROLL = 8


def _params(*sem):
    return pltpu.CompilerParams(dimension_semantics=sem, vmem_limit_bytes=VMEM_LIMIT_BYTES)


def _pick(n, prefs):
    for p in prefs:
        if n % p == 0:
            return p
    return n


def _mod_kernel(c_ref, w_ref, b_ref, o_ref):
    c = c_ref[...]
    s = (c * jax.nn.sigmoid(c)).astype(BF16)
    o_ref[...] = jnp.dot(s, w_ref[...].astype(BF16), preferred_element_type=F32) + b_ref[...]


def modulation(cond8, w_mod, b_mod, layer):
    rows, d = cond8.shape
    depth, _, n = w_mod.shape
    tn = _pick(n, (512, 256, 128))
    return pl.pallas_call(
        _mod_kernel,
        grid=(n // tn,),
        in_specs=[pl.BlockSpec((rows, d), lambda j: (0, 0)),
                  pl.BlockSpec((None, d, tn), lambda j: (layer, 0, j)),
                  pl.BlockSpec((None, 1, tn), lambda j: (layer, 0, j))],
        out_specs=pl.BlockSpec((rows, tn), lambda j: (0, j)),
        out_shape=jax.ShapeDtypeStruct((rows, n), F32),
        compiler_params=_params("parallel"),
        name="modulation",
    )(cond8, w_mod, b_mod.reshape(depth, 1, n))


def _mod_row_fn(tm, t_prompt, dec_seq):
    n_p = t_prompt // tm
    per_b = dec_seq // tm

    def row(i):
        return jnp.where(i < n_p, 0, 1 + (i - n_p) // per_b)
    return row


def _modulated_norm(x, g, sc, sh):
    ms = jnp.mean(x * x, axis=-1, keepdims=True)
    return (x * lax.rsqrt(ms + EPS) * g) * (1.0 + sc) + sh


def _norm_matmul_kernel(x_ref, g_ref, mod_ref, w_ref, o_ref, h_ref, *, sh_idx, sc_idx):
    @pl.when(pl.program_id(1) == 0)
    def _():
        h = _modulated_norm(x_ref[...], g_ref[...], mod_ref[sc_idx:sc_idx + 1, :],
                            mod_ref[sh_idx:sh_idx + 1, :])
        h_ref[...] = h.astype(BF16)
    o_ref[...] = jnp.dot(h_ref[...], w_ref[...], preferred_element_type=F32).astype(o_ref.dtype)


def norm_matmul(x, g, mod, w, *, sh_idx, sc_idx, tm, row_fn, out_dtype=F32):
    t, d = x.shape
    n = w.shape[1]
    tn = _pick(n, (1152, 1024, 512, 384, 256, 128))
    return pl.pallas_call(
        functools.partial(_norm_matmul_kernel, sh_idx=sh_idx, sc_idx=sc_idx),
        grid=(t // tm, n // tn),
        in_specs=[pl.BlockSpec((tm, d), lambda i, j: (i, 0)),
                  pl.BlockSpec((1, d), lambda i, j: (0, 0)),
                  pl.BlockSpec((None, mod.shape[1], d), lambda i, j: (row_fn(i), 0, 0)),
                  pl.BlockSpec((d, tn), lambda i, j: (0, j))],
        out_specs=pl.BlockSpec((tm, tn), lambda i, j: (i, j)),
        out_shape=jax.ShapeDtypeStruct((t, n), out_dtype),
        scratch_shapes=[pltpu.VMEM((tm, d), BF16)],
        compiler_params=_params("parallel", "arbitrary"),
        name="norm_in_proj",
    )(x, g.reshape(1, d), mod, w)


def _out_proj_kernel(om_ref, og_ref, of_ref, ob_ref, z_ref, gd_ref, wm_ref, wg_ref, wd_ref, x_ref, mod_ref,
                     o_ref, od_ref, *, gate_idx):
    @pl.when(pl.program_id(1) == 0)
    def _():
        for h in range(of_ref.shape[1] // GDN_DV):
            hl = slice(h * GDN_DV, (h + 1) * GDN_DV)
            z = z_ref[:, hl]
            od_ref[:, hl] = (_head_rms(of_ref[:, hl] + ob_ref[:, hl], gd_ref[...])
                             * (z * jax.nn.sigmoid(z))).astype(BF16)
    acc = jnp.dot(om_ref[...], wm_ref[...], preferred_element_type=F32)
    acc += jnp.dot(og_ref[...], wg_ref[...], preferred_element_type=F32)
    acc += jnp.dot(od_ref[...], wd_ref[...], preferred_element_type=F32)
    o_ref[...] = x_ref[...] + mod_ref[gate_idx:gate_idx + 1, :] * acc


def out_proj_residual(o_m, o_g, o_fwd, o_bwd, u, z_col, g_gdn, w_out, x, mod, *, gate_idx, tm, row_fn):
    t, d = x.shape
    km, kg, kd = o_m.shape[1], o_g.shape[1], o_fwd.shape[1]
    z0, zw = z_col
    assert km == kg and (km + kg) % kd == 0 and zw == kd
    tn = _pick(d, (1024, 512, 256, 128))
    return pl.pallas_call(
        functools.partial(_out_proj_kernel, gate_idx=gate_idx),
        grid=(t // tm, d // tn),
        in_specs=[pl.BlockSpec((tm, km), lambda i, j: (i, 0)),
                  pl.BlockSpec((tm, kg), lambda i, j: (i, 0)),
                  pl.BlockSpec((tm, kd), lambda i, j: (i, 0)),
                  pl.BlockSpec((tm, kd), lambda i, j: (i, 0)),
                  pl.BlockSpec((tm, zw), lambda i, j: (i, z0 // zw)),
                  pl.BlockSpec((1, GDN_DV), lambda i, j: (0, 0)),
                  pl.BlockSpec((km, tn), lambda i, j: (0, j)),
                  pl.BlockSpec((kg, tn), lambda i, j: (1, j)),
                  pl.BlockSpec((kd, tn), lambda i, j: ((km + kg) // kd, j)),
                  pl.BlockSpec((tm, tn), lambda i, j: (i, j)),
                  pl.BlockSpec((None, mod.shape[1], tn), lambda i, j: (row_fn(i), 0, j))],
        out_specs=pl.BlockSpec((tm, tn), lambda i, j: (i, j)),
        out_shape=jax.ShapeDtypeStruct((t, d), F32),
        scratch_shapes=[pltpu.VMEM((tm, kd), BF16)],
        compiler_params=_params("parallel", "arbitrary"),
        name="out_proj",
    )(o_m, o_g, o_fwd, o_bwd, u, g_gdn.reshape(1, GDN_DV), w_out, w_out, w_out, x, mod)


def _pack_bf16_pair(lo, hi):
    lo_bits = lax.bitcast_convert_type(lo.astype(BF16).astype(F32), jnp.uint32)
    hi_bits = lax.bitcast_convert_type(hi.astype(BF16).astype(F32), jnp.uint32)
    return jnp.right_shift(lo_bits, jnp.uint32(16)) | (hi_bits & jnp.uint32(0xFFFF0000))


def _unpack_bf16_pair(word):
    lo = lax.bitcast_convert_type(jnp.left_shift(word, jnp.uint32(16)), F32)
    hi = lax.bitcast_convert_type(word & jnp.uint32(0xFFFF0000), F32)
    return lo, hi


def _ffn_norm_router_kernel(x_ref, g_ref, mod_ref, wr_ref, br_ref, h_ref, gate_ref, exp_ref, *,
                            sh_idx, sc_idx, n_groups, epg):
    h = _modulated_norm(x_ref[...], g_ref[...], mod_ref[sc_idx:sc_idx + 1, :],
                        mod_ref[sh_idx:sh_idx + 1, :])
    half = h.shape[1] // 2
    h_ref[...] = _pack_bf16_pair(h[:, :half], h[:, half:])
    lg = jnp.dot(h, wr_ref[...], preferred_element_type=F32, precision=lax.Precision.HIGHEST) + br_ref[...]
    lane = lax.broadcasted_iota(jnp.int32, lg.shape, 1).astype(F32)
    neg = jnp.float32(-1e30)
    far = jnp.float32(2 * LANE)

    def first_argmax(v):
        top = jnp.max(v, axis=-1, keepdims=True)
        return top, jnp.min(jnp.where(v == top, lane, far), axis=-1, keepdims=True)

    is_group = lane < n_groups
    g_max, g_idx = first_argmax(jnp.where(is_group, lg, neg))
    g_top = 1.0 / jnp.sum(jnp.where(is_group, jnp.exp(lg - g_max), 0.0), axis=-1, keepdims=True)
    lo = n_groups + g_idx * epg
    el = jnp.where(jnp.logical_and(lane >= lo, lane < lo + epg), lg, neg)
    e1, i1 = first_argmax(el)
    e2, i2 = first_argmax(jnp.where(lane == i1, neg, el))
    r = jnp.exp(e2 - e1)
    gate1 = g_top / (1.0 + r)
    gate2 = g_top * r / (1.0 + r)
    gate_ref[...] = jnp.where(lane == 0, gate1, jnp.where(lane == 1, gate2, 0.0))
    exp_ref[...] = jnp.where(lane == 0, i1 - n_groups, jnp.where(lane == 1, i2 - n_groups, 0.0)).astype(jnp.int32)


def ffn_norm_router(x, g, mod, w_router, b_router, *, sh_idx, sc_idx, tm, row_fn, n_groups, epg):
    t, d = x.shape
    nr = w_router.shape[1]
    assert d % (2 * LANE) == 0
    return pl.pallas_call(
        functools.partial(_ffn_norm_router_kernel, sh_idx=sh_idx, sc_idx=sc_idx, n_groups=n_groups, epg=epg),
        grid=(t // tm,),
        in_specs=[pl.BlockSpec((tm, d), lambda i: (i, 0)),
                  pl.BlockSpec((1, d), lambda i: (0, 0)),
                  pl.BlockSpec((None, mod.shape[1], d), lambda i: (row_fn(i), 0, 0)),
                  pl.BlockSpec((d, nr), lambda i: (0, 0)),
                  pl.BlockSpec((1, nr), lambda i: (0, 0))],
        out_specs=[pl.BlockSpec((tm, d // 2), lambda i: (i, 0)),
                   pl.BlockSpec((tm, nr), lambda i: (i, 0)),
                   pl.BlockSpec((tm, nr), lambda i: (i, 0))],
        out_shape=[jax.ShapeDtypeStruct((t, d // 2), jnp.uint32), jax.ShapeDtypeStruct((t, nr), F32),
                   jax.ShapeDtypeStruct((t, nr), jnp.int32)],
        compiler_params=_params("parallel"),
        name="ffn_norm_router",
    )(x, g.reshape(1, d), mod, w_router, b_router)


def _rope_rotate(x, cos, sin, pair):
    lane = lax.broadcasted_iota(jnp.int32, x.shape, 1)
    first = (lane & (2 * pair - 1)) < pair
    partner = jnp.where(first, pltpu.roll(x, LANE - pair, axis=1), pltpu.roll(x, pair, axis=1))
    return x * cos + partner * sin


def _head_rms(x, g):
    return x * lax.rsqrt(jnp.mean(x * x, axis=-1, keepdims=True) + EPS) * g


def _gqa_prep_kernel(gq_ref, gk_ref, gv_ref, gq_gain_ref, gk_gain_ref, cos_ref, sin_ref,
                     q_ref, k_rot_ref, k_ref, v_ref, *, q_scale):
    cos, sin = cos_ref[...], sin_ref[...]
    pair = HEAD_DIM // 4
    for h in range(gq_ref.shape[1] // HEAD_DIM):
        hl = slice(h * HEAD_DIM, (h + 1) * HEAD_DIM)
        q = _rope_rotate(_head_rms(gq_ref[:, hl], gq_gain_ref[...]), cos, sin, pair)
        q_ref[:, hl] = (q * q_scale).astype(BF16)
    for h in range(gk_ref.shape[1] // HEAD_DIM):
        hl = slice(h * HEAD_DIM, (h + 1) * HEAD_DIM)
        k = _head_rms(gk_ref[:, hl], gk_gain_ref[...])
        k_ref[:, hl] = k
        k_rot_ref[:, hl] = _rope_rotate(k, cos, sin, pair).astype(BF16)
    v_ref[...] = gv_ref[...].astype(BF16)


def gqa_prep(u, cols, g_q, g_k, cos, sin, *, tm, q_scale):
    t = u.shape[0]
    (q0, qw), (k0, kw), (v0, vw) = cols['gq'], cols['gk'], cols['gv']
    row = lambda i: (i, 0)
    return pl.pallas_call(
        functools.partial(_gqa_prep_kernel, q_scale=q_scale),
        grid=(t // tm,),
        in_specs=[pl.BlockSpec((tm, qw), lambda i: (i, q0 // qw)),
                  pl.BlockSpec((tm, kw), lambda i: (i, k0 // kw)),
                  pl.BlockSpec((tm, vw), lambda i: (i, v0 // vw)),
                  pl.BlockSpec((1, HEAD_DIM), lambda i: (0, 0)),
                  pl.BlockSpec((1, HEAD_DIM), lambda i: (0, 0)),
                  pl.BlockSpec((tm, LANE), row),
                  pl.BlockSpec((tm, LANE), row)],
        out_specs=[pl.BlockSpec((tm, qw), row), pl.BlockSpec((tm, kw), row),
                   pl.BlockSpec((tm, kw), row), pl.BlockSpec((tm, vw), row)],
        out_shape=[jax.ShapeDtypeStruct((t, qw), BF16), jax.ShapeDtypeStruct((t, kw), BF16),
                   jax.ShapeDtypeStruct((t, kw), F32), jax.ShapeDtypeStruct((t, vw), BF16)],
        compiler_params=_params("parallel"),
        name="gqa_prep",
    )(u, u, u, g_q.reshape(1, HEAD_DIM), g_k.reshape(1, HEAD_DIM), cos, sin)


def _mla_q_kernel(cq_ref, g_cq_ref, wq_ref, g_q_ref, cos_ref, sin_ref, q_ref, *, q_scale):
    cqn = _head_rms(cq_ref[...], g_cq_ref[...]).astype(BF16)
    q = jnp.dot(cqn, wq_ref[...], preferred_element_type=F32)
    cos, sin = cos_ref[...], sin_ref[...]
    g_nope, g_rope = g_q_ref[:, :MLA_NOPE], g_q_ref[:, MLA_NOPE:]
    for h in range(q.shape[1] // MLA_PAD):
        nope = q[:, h * MLA_PAD:h * MLA_PAD + MLA_NOPE]
        rope = q[:, h * MLA_PAD + MLA_NOPE:(h + 1) * MLA_PAD]
        ss = jnp.sum(nope * nope, axis=-1, keepdims=True) + jnp.sum(rope * rope, axis=-1, keepdims=True)
        r = lax.rsqrt(ss / MLA_QK + EPS) * q_scale
        q_ref[:, h * MLA_PAD:h * MLA_PAD + MLA_NOPE] = (nope * r * g_nope).astype(BF16)
        q_ref[:, h * MLA_PAD + MLA_NOPE:(h + 1) * MLA_PAD] = (
            _rope_rotate(rope * g_rope, cos, sin, MLA_ROPE // 4) * r).astype(BF16)


def mla_queries(u, cols, g_cq, wq, g_q_pad, cos, sin, *, tm, q_scale):
    t = u.shape[0]
    c0, cw = cols['cq']
    n = wq.shape[1]
    return pl.pallas_call(
        functools.partial(_mla_q_kernel, q_scale=q_scale),
        grid=(t // tm,),
        in_specs=[pl.BlockSpec((tm, cw), lambda i: (i, c0 // cw)),
                  pl.BlockSpec((1, cw), lambda i: (0, 0)),
                  pl.BlockSpec((cw, n), lambda i: (0, 0)),
                  pl.BlockSpec((1, MLA_PAD), lambda i: (0, 0)),
                  pl.BlockSpec((tm, LANE), lambda i: (i, 0)),
                  pl.BlockSpec((tm, LANE), lambda i: (i, 0))],
        out_specs=pl.BlockSpec((tm, n), lambda i: (i, 0)),
        out_shape=jax.ShapeDtypeStruct((t, n), BF16),
        compiler_params=_params("parallel"),
        name="mla_queries",
    )(u, g_cq.reshape(1, cw), wq, g_q_pad.reshape(1, MLA_PAD), cos, sin)


def _mla_kv_kernel(ckv_ref, kr_ref, g_ckv_ref, wkv_ref, g_k_ref, cos_ref, sin_ref, ckvn_ref, k_ref, v_ref, *,
                   heads, normalise):
    ckv = ckv_ref[...]
    if normalise:
        ckv = _head_rms(ckv, g_ckv_ref[...])
    ckvn_ref[...] = ckv
    kv = jnp.dot(ckv.astype(BF16), wkv_ref[...], preferred_element_type=F32)
    lane = lax.broadcasted_iota(jnp.int32, (ckv.shape[0], LANE), 1)
    krope = jnp.where(lane < MLA_ROPE, kr_ref[:, :LANE], 0.0)
    ss_rope = jnp.sum(krope * krope, axis=-1, keepdims=True)
    g_nope, g_rope = g_k_ref[:, :MLA_NOPE], g_k_ref[:, MLA_NOPE:]
    rope = _rope_rotate(krope * g_rope, cos_ref[...], sin_ref[...], MLA_ROPE // 4)
    for h in range(heads):
        nope = kv[:, h * MLA_NOPE:(h + 1) * MLA_NOPE]
        r = lax.rsqrt((jnp.sum(nope * nope, axis=-1, keepdims=True) + ss_rope) / MLA_QK + EPS)
        k_ref[:, h * MLA_PAD:h * MLA_PAD + MLA_NOPE] = (nope * r * g_nope).astype(BF16)
        k_ref[:, h * MLA_PAD + MLA_NOPE:(h + 1) * MLA_PAD] = (rope * r).astype(BF16)
    v_ref[...] = kv[:, heads * MLA_NOPE:].astype(BF16)


def mla_keys_values(ckv_src, ckv_col, kr_src, kr_col, g_ckv, wkv, g_k_pad, cos, sin, *, heads, tm, normalise):
    t = ckv_src.shape[0]
    (c0, cw), (r0, rw) = ckv_col, kr_col
    return pl.pallas_call(
        functools.partial(_mla_kv_kernel, heads=heads, normalise=normalise),
        grid=(t // tm,),
        in_specs=[pl.BlockSpec((tm, cw), lambda i: (i, c0 // cw)),
                  pl.BlockSpec((tm, rw), lambda i: (i, r0 // rw)),
                  pl.BlockSpec((1, cw), lambda i: (0, 0)),
                  pl.BlockSpec((cw, wkv.shape[1]), lambda i: (0, 0)),
                  pl.BlockSpec((1, MLA_PAD), lambda i: (0, 0)),
                  pl.BlockSpec((tm, LANE), lambda i: (i, 0)),
                  pl.BlockSpec((tm, LANE), lambda i: (i, 0))],
        out_specs=[pl.BlockSpec((tm, cw), lambda i: (i, 0)),
                   pl.BlockSpec((tm, heads * MLA_PAD), lambda i: (i, 0)),
                   pl.BlockSpec((tm, heads * MLA_V), lambda i: (i, 0))],
        out_shape=[jax.ShapeDtypeStruct((t, cw), F32), jax.ShapeDtypeStruct((t, heads * MLA_PAD), BF16),
                   jax.ShapeDtypeStruct((t, heads * MLA_V), BF16)],
        compiler_params=_params("parallel"),
        name="mla_keys_values",
    )(ckv_src, kr_src, g_ckv.reshape(1, cw), wkv, g_k_pad.reshape(1, MLA_PAD), cos, sin)


def _gdn_prep_kernel(x_ref, prev_ref, next_ref, w_ref, o_ref, *, heads, tm, n_prompt_tiles, seq_lens):
    i = pl.program_id(0)
    x = x_ref[...]
    row = lax.broadcasted_iota(jnp.int32, (tm, 1), 0)
    in_prompt = i < n_prompt_tiles
    seq_len = jnp.where(in_prompt, seq_lens[0], seq_lens[1])
    first_row = jnp.where(in_prompt, i, i - n_prompt_tiles) * tm
    span = [min(s, tm) for s in seq_lens]
    local = jnp.where(in_prompt, row & (span[0] - 1), row & (span[1] - 1))
    span_t = jnp.where(in_prompt, span[0], span[1])
    tile_starts_seq = lax.rem(first_row, seq_len) == 0
    tile_ends_seq = lax.rem(first_row + tm, seq_len) == 0
    is_start = jnp.logical_and(local == 0, jnp.logical_or(row != 0, tile_starts_seq))
    is_end = jnp.logical_and(local == span_t - 1, jnp.logical_or(row != tm - 1, tile_ends_seq))
    prev = jnp.where(row == 0, prev_ref[7:8, :], pltpu.roll(x, 1, axis=0))
    nxt = jnp.where(row == tm - 1, next_ref[0:1, :], pltpu.roll(x, tm - 1, axis=0))
    prev = jnp.where(is_start, 0.0, prev)
    nxt = jnp.where(is_end, 0.0, nxt)
    conv = prev * w_ref[0:1, :] + x * w_ref[1:2, :] + nxt * w_ref[2:3, :]
    act = conv * jax.nn.sigmoid(conv)
    for h in range(3 * heads):
        hl = slice(h * GDN_DK, (h + 1) * GDN_DK)
        slab = act[:, hl]
        if h < 2 * heads:
            slab = slab * lax.rsqrt(jnp.sum(slab * slab, axis=-1, keepdims=True) + EPS)
            if h < heads:
                slab = slab * GDN_DK ** -0.5
        o_ref[:, hl] = slab


def gdn_prep(u, col, w_conv, *, heads, tm, t_prompt, seq_lens):
    t = u.shape[0]
    c0, cw = col
    assert all(s & (s - 1) == 0 for s in (tm,) + tuple(min(s, tm) for s in seq_lens))
    eight = 8
    cb = c0 // cw
    last8 = t // eight - 1
    return pl.pallas_call(
        functools.partial(_gdn_prep_kernel, heads=heads, tm=tm, n_prompt_tiles=t_prompt // tm, seq_lens=seq_lens),
        grid=(t // tm,),
        in_specs=[pl.BlockSpec((tm, cw), lambda i: (i, cb)),
                  pl.BlockSpec((eight, cw), lambda i: (jnp.maximum(i * (tm // eight) - 1, 0), cb)),
                  pl.BlockSpec((eight, cw), lambda i: (jnp.minimum((i + 1) * (tm // eight), last8), cb)),
                  pl.BlockSpec((3, cw), lambda i: (0, 0))],
        out_specs=pl.BlockSpec((tm, cw), lambda i: (i, 0)),
        out_shape=jax.ShapeDtypeStruct((t, cw), F32),
        compiler_params=_params("parallel"),
        name="gdn_prep",
    )(u, u, u, w_conv)


def _attn_kernel(*refs, n_seg, aliased, group, dq, dv, tk):
    q_ref = refs[0]
    kv_refs = [(refs[1 + 2 * s], refs[2 + 2 * s]) for s in range(n_seg)]
    o_ref, s_ref, mx_ref, l_ref, acc_ref = refs[1 + 2 * n_seg + (1 if aliased else 0):]
    tq = q_ref.shape[0]
    q = jnp.concatenate([q_ref[:, g * dq:(g + 1) * dq] for g in range(group)], axis=0)
    slabs = tk // LANE

    def for_chunks(body):
        base = 0
        for k_ref, v_ref in kv_refs:
            nk = k_ref.shape[0] // tk
            main = nk // ATTN_UNROLL

            def group_body(i, carry, k_ref=k_ref, v_ref=v_ref, base=base):
                for u in range(ATTN_UNROLL):
                    body(k_ref, v_ref, i * ATTN_UNROLL + u, base)
                return carry
            if main:
                lax.fori_loop(0, main, group_body, 0)
            for c in range(main * ATTN_UNROLL, nk):
                body(k_ref, v_ref, c, base)
            base += nk

    mx_ref[...] = jnp.full(mx_ref.shape, -jnp.inf, F32)

    def rows_of(c):
        return pl.ds(c * tk if isinstance(c, int) else pl.multiple_of(c * tk, tk), tk)

    def scores(k_ref, v_ref, c, base):
        s = lax.dot_general(q, k_ref[rows_of(c), :], (((1,), (1,)), ((), ())),
                            preferred_element_type=F32)
        s_ref[base + c] = s
        mx = mx_ref[...]
        for j in range(slabs):
            mx = jnp.maximum(mx, s[:, j * LANE:(j + 1) * LANE])
        mx_ref[...] = mx

    for_chunks(scores)
    m = jnp.broadcast_to(jnp.max(mx_ref[...], axis=-1, keepdims=True), mx_ref.shape)
    l_ref[...] = jnp.zeros_like(l_ref)
    acc_ref[...] = jnp.zeros_like(acc_ref)

    def weighted(k_ref, v_ref, c, base):
        s = s_ref[base + c]
        p = jnp.concatenate([jnp.exp2(s[:, j * LANE:(j + 1) * LANE] - m) for j in range(slabs)], axis=1)
        lsum = l_ref[...]
        for j in range(slabs):
            lsum = lsum + p[:, j * LANE:(j + 1) * LANE]
        l_ref[...] = lsum
        acc_ref[...] += jnp.dot(p.astype(BF16), v_ref[rows_of(c), :], preferred_element_type=F32)

    for_chunks(weighted)
    o = acc_ref[...] / jnp.sum(l_ref[...], axis=-1, keepdims=True)
    for g in range(group):
        o_ref[:, g * dv:(g + 1) * dv] = o[g * tq:(g + 1) * tq].astype(o_ref.dtype)


def attention(q, q_row0, kv_segments, *, batch, lq, kv_heads, group, dq, dv, out_rows, out_row0, prev_out, name):
    tq = _pick(lq, (512, 256, 128)) // (2 if group > 1 else 1)
    tk = _pick(math.gcd(*[length for _, _, _, length in kv_segments]), (256, 128))
    nq = lq // tq
    m_rows = group * tq
    assert q_row0 % tq == 0 and out_row0 % tq == 0
    in_specs = [pl.BlockSpec((tq, group * dq), lambda b, h, i: (q_row0 // tq + b * nq + i, h))]
    operands = [q]
    for k, v, first_row, length in kv_segments:
        assert first_row % length == 0
        in_specs += [pl.BlockSpec((length, dq), lambda b, h, i, f=first_row // length: (f + b, h)),
                     pl.BlockSpec((length, dv), lambda b, h, i, f=first_row // length: (f + b, h))]
        operands += [k, v]
    aliases = {}
    if prev_out is not None:
        in_specs.append(pl.BlockSpec(memory_space=pl.ANY))
        aliases = {len(operands): 0}
        operands.append(prev_out)
    n_chunks = sum(length // tk for _, _, _, length in kv_segments)
    return pl.pallas_call(
        functools.partial(_attn_kernel, n_seg=len(kv_segments), aliased=prev_out is not None, group=group,
                          dq=dq, dv=dv, tk=tk),
        grid=(batch, kv_heads, nq),
        in_specs=in_specs,
        out_specs=pl.BlockSpec((tq, group * dv), lambda b, h, i: (out_row0 // tq + b * nq + i, h)),
        out_shape=jax.ShapeDtypeStruct((out_rows, kv_heads * group * dv), BF16),
        input_output_aliases=aliases,
        scratch_shapes=[pltpu.VMEM((n_chunks, m_rows, tk), F32), pltpu.VMEM((m_rows, LANE), F32),
                        pltpu.VMEM((m_rows, LANE), F32), pltpu.VMEM((m_rows, dv), F32)],
        compiler_params=_params("parallel", "parallel", "parallel"),
        name=name,
    )(*operands)


def _split_hi_lo(a):
    hi = a.astype(BF16).astype(F32)
    return hi, (a - hi).astype(BF16).astype(F32)


def _dot_split(a_parts, b_parts):
    ah, al = a_parts
    bh, bl = b_parts
    lhs = jnp.concatenate([ah, al, ah, al], axis=2).astype(BF16)
    rhs = jnp.concatenate([bh, bh, bl, bl], axis=1).astype(BF16)
    return lax.dot_general(lhs, rhs, (((2,), (1,)), ((0,), (0,))), preferred_element_type=F32)


def _dot_nt(a, b):
    return lax.dot_general(a.astype(BF16), b.astype(BF16), (((2,), (2,)), ((0,), (0,))), preferred_element_type=F32)


def _chunk_masks(backward):
    r = lax.broadcasted_iota(jnp.int32, (GDN_CHUNK, GDN_CHUNK), 0)
    c = lax.broadcasted_iota(jnp.int32, (GDN_CHUNK, GDN_CHUNK), 1)
    incl = (r <= c) if backward else (r >= c)
    strict = (r < c) if backward else (r > c)
    return r, c, incl, strict


def _decay_mask(dcol, r, c, incl):
    drow = jnp.sum(jnp.where(r == c, dcol, 0.0), axis=1, keepdims=True)
    return jnp.where(incl, jnp.exp(jnp.where(incl, dcol - drow, 0.0)), 0.0)


def _unit_tri_inverse(a_mat, r, c):
    eye = (r == c).astype(F32)
    same16 = jnp.right_shift(r, 4) == jnp.right_shift(c, 4)
    same32 = jnp.right_shift(r, 5) == jnp.right_shift(c, 5)
    x = jnp.where(same16, -a_mat, 0.0)
    inv = eye + x
    x_parts = _split_hi_lo(x)
    for _ in range(3):
        x_parts = _split_hi_lo(_dot_split(x_parts, x_parts))
        inv = inv + _dot_split(_split_hi_lo(inv), x_parts)
    for off_blocks in (jnp.logical_and(same32, jnp.logical_not(same16)), jnp.logical_not(same32)):
        off = jnp.where(off_blocks, a_mat, 0.0)
        inv_parts = _split_hi_lo(inv)
        inv = inv - _dot_split(inv_parts, _split_hi_lo(_dot_split(_split_hi_lo(off), inv_parts)))
    return inv


def _head_column(ref, lane, index, n_c):
    col = jnp.sum(jnp.where(lane == index, ref[...], 0.0), axis=1, keepdims=True)
    return col.reshape(n_c, GDN_CHUNK, 1)


def _gdn_solve_kernel(k_ref, v_ref, d_ref, b_ref, uw_ref, *, head_lanes):
    h = pl.program_id(1)
    n_c = k_ref.shape[0] // GDN_CHUNK
    lane = lax.broadcasted_iota(jnp.int32, (1, d_ref.shape[1]), 1)
    kc = k_ref[...].reshape(n_c, GDN_CHUNK, GDN_DK)
    vc = v_ref[...].reshape(n_c, GDN_CHUNK, GDN_DV)
    for direction in range(2):
        r, c, incl, strict = _chunk_masks(direction == 1)
        dcol = _head_column(d_ref, lane, direction * head_lanes + h, n_c)
        bcol = _head_column(b_ref, lane, direction * head_lanes + h, n_c)
        kb = kc * bcol
        a_mat = jnp.where(strict, _dot_nt(kb, kc) * _decay_mask(dcol, r, c, incl), 0.0)
        rhs = jnp.concatenate([vc * bcol, kb * jnp.exp(dcol)], axis=2)
        uw = _dot_split(_split_hi_lo(_unit_tri_inverse(a_mat, r, c)), _split_hi_lo(rhs))
        uw_ref[direction] = uw.reshape(n_c * GDN_CHUNK, GDN_DV + GDN_DK)


def gdn_solve(qkv, dcs, beta, *, heads):
    t = qkv.shape[0]
    rows = _pick(t, (512, 256, 128, 64))
    width = GDN_DV + GDN_DK
    return pl.pallas_call(
        functools.partial(_gdn_solve_kernel, head_lanes=heads),
        grid=(t // rows, heads),
        in_specs=[pl.BlockSpec((rows, GDN_DK), lambda i, h: (i, heads + h)),
                  pl.BlockSpec((rows, GDN_DV), lambda i, h: (i, 2 * heads + h)),
                  pl.BlockSpec((rows, 2 * heads), lambda i, h: (i, 0)),
                  pl.BlockSpec((rows, 2 * heads), lambda i, h: (i, 0))],
        out_specs=pl.BlockSpec((2, rows, width), lambda i, h: (0, i, h)),
        out_shape=jax.ShapeDtypeStruct((2, t, heads * width), F32),
        compiler_params=_params("parallel", "parallel"),
        name="gdn_solve",
    )(qkv, qkv, dcs, beta)


def _gdn_scan_kernel(fwd_blk_ref, bwd_blk_ref, seq_ref, edge_ref, qf_ref, kf_ref, uwf_ref, df_ref,
                     qb_ref, kb_ref, uwb_ref, db_ref, s0_ref, of_ref, ob_ref, sfin_ref, s_ref, *,
                     heads_per_step, head_lanes):
    step = pl.program_id(1)
    hg = pl.program_id(0)
    rows = qf_ref.shape[0]
    n_c = rows // GDN_CHUNK
    lane = lax.broadcasted_iota(jnp.int32, (1, df_ref.shape[1]), 1)

    @pl.when((edge_ref[step] & 1) != 0)
    def _():
        s_ref[...] = s0_ref[...].astype(F32)

    def bmm(a, b):
        return lax.dot_general(a, b, (((2,), (1,)), ((0,), (0,))), preferred_element_type=F32)

    width = GDN_DV + GDN_DK
    streams = ((qf_ref, kf_ref, uwf_ref, df_ref, of_ref), (qb_ref, kb_ref, uwb_ref, db_ref, ob_ref))
    for ci in range(n_c):
        for direction, (q_ref, k_ref, uw_ref, d_ref, o_ref) in enumerate(streams):
            backward = direction == 1
            r, c, incl, _ = _chunk_masks(backward)
            pos = (n_c - 1 - ci) if backward else ci
            sl = slice(pos * GDN_CHUNK, (pos + 1) * GDN_CHUNK)
            heads = range(heads_per_step)
            dcol = jnp.stack([jnp.sum(jnp.where(lane == direction * head_lanes + hg * heads_per_step + hh,
                                                d_ref[sl, :], 0.0), axis=1, keepdims=True) for hh in heads])
            qc = jnp.stack([q_ref[sl, hh * GDN_DK:(hh + 1) * GDN_DK] for hh in heads])
            kc = jnp.stack([k_ref[sl, hh * GDN_DK:(hh + 1) * GDN_DK] for hh in heads])
            u = jnp.stack([uw_ref[sl, hh * width:hh * width + GDN_DV] for hh in heads])
            w = jnp.stack([uw_ref[sl, hh * width + GDN_DV:(hh + 1) * width] for hh in heads])
            attn = _dot_nt(qc, kc) * _decay_mask(dcol, r, c, incl)
            d_last = dcol[:, 0:1, :] if backward else dcol[:, GDN_CHUNK - 1:GDN_CHUNK, :]
            q_dec = (qc * jnp.exp(dcol)).astype(BF16)
            k_dec = (kc * jnp.exp(d_last - dcol)).astype(BF16)
            s = s_ref[direction]
            s_b = s.astype(BF16)
            v_new = u - bmm(w.astype(BF16), s_b)
            v_b = v_new.astype(BF16)
            o = bmm(q_dec, s_b) + bmm(attn.astype(BF16), v_b)
            for hh in heads:
                o_ref[sl, hh * GDN_DV:(hh + 1) * GDN_DV] = o[hh]
            s_ref[direction] = s * jnp.exp(d_last) + lax.dot_general(
                k_dec, v_b, (((1,), (1,)), ((0,), (0,))), preferred_element_type=F32)

    @pl.when((edge_ref[step] & 2) != 0)
    def _():
        sfin_ref[...] = s_ref[...]


def gdn_scan(qkv, uw, dcs, s0, *, sequences, heads):
    t = qkv.shape[0]
    rows = _pick(math.gcd(*[length for _, length in sequences]), (256, 128, 64))
    hps = _pick(heads, (4, 2, 1))
    hgroups = heads // hps
    width = GDN_DV + GDN_DK
    fwd_blk, bwd_blk, seq_id, edge = [], [], [], []
    for s, (first_row, length) in enumerate(sequences):
        assert first_row % rows == 0 and length % rows == 0
        n_l = length // rows
        for i in range(n_l):
            fwd_blk.append(first_row // rows + i)
            bwd_blk.append(first_row // rows + n_l - 1 - i)
            seq_id.append(s)
            edge.append((1 if i == 0 else 0) | (2 if i == n_l - 1 else 0))
    assert sorted(fwd_blk) == list(range(t // rows))
    tables = [jnp.asarray(np.asarray(v, np.int32)) for v in (fwd_blk, bwd_blk, seq_id, edge)]

    def stream_specs(table, direction):
        return [pl.BlockSpec((rows, hps * GDN_DK), lambda g, s, fb, bb, sq, ed: ((fb, bb)[table][s], g)),
                pl.BlockSpec((rows, hps * GDN_DK), lambda g, s, fb, bb, sq, ed: ((fb, bb)[table][s], hgroups + g)),
                pl.BlockSpec((None, rows, hps * width), lambda g, s, fb, bb, sq, ed: (direction, (fb, bb)[table][s], g)),
                pl.BlockSpec((rows, 2 * heads), lambda g, s, fb, bb, sq, ed: ((fb, bb)[table][s], 0))]

    state_spec = pl.BlockSpec((None, 2, hps, GDN_DK, GDN_DV), lambda g, s, fb, bb, sq, ed: (sq[s], 0, g, 0, 0))
    grid_spec = pltpu.PrefetchScalarGridSpec(
        num_scalar_prefetch=4,
        grid=(hgroups, len(fwd_blk)),
        in_specs=stream_specs(0, 0) + stream_specs(1, 1) + [state_spec],
        out_specs=[pl.BlockSpec((rows, hps * GDN_DV), lambda g, s, fb, bb, sq, ed: (fb[s], g)),
                   pl.BlockSpec((rows, hps * GDN_DV), lambda g, s, fb, bb, sq, ed: (bb[s], g)),
                   state_spec],
        scratch_shapes=[pltpu.VMEM((2, hps, GDN_DK, GDN_DV), F32)],
    )
    return pl.pallas_call(
        functools.partial(_gdn_scan_kernel, heads_per_step=hps, head_lanes=heads),
        grid_spec=grid_spec,
        out_shape=[jax.ShapeDtypeStruct((t, heads * GDN_DV), F32),
                   jax.ShapeDtypeStruct((t, heads * GDN_DV), F32),
                   jax.ShapeDtypeStruct((len(sequences), 2, heads, GDN_DK, GDN_DV), F32)],
        compiler_params=_params("parallel", "arbitrary"),
        name="gdn_scan",
    )(*tables, qkv, qkv, uw, dcs, qkv, qkv, uw, dcs, s0)


def _row_gather(idx_ref, src_hbm, dst_vmem, sem, n_rows):
    def start():
        def body(r, carry):
            pltpu.make_async_copy(src_hbm.at[pl.ds(idx_ref[0, r], 1), :], dst_vmem.at[pl.ds(r, 1), :], sem).start()
            return carry
        lax.fori_loop(0, n_rows, body, 0, unroll=GATHER_UNROLL)

    def wait():
        def body(r, carry):
            pltpu.make_async_copy(src_hbm.at[pl.ds(0, 1), :], dst_vmem.at[pl.ds(0, 1), :], sem).wait()
            return carry
        lax.fori_loop(0, n_rows, body, 0, unroll=GATHER_UNROLL)
    return start, wait


def _moe_kernel(be_ref, nu_ref, nv_ref, tok_ref, tok_next_ref, h_hbm, wg_ref, wu_ref, wd_ref, o_ref,
                xbuf_ref, xlo_ref, xhi_ref, hmid_ref, sem, *, n_f):
    i = pl.program_id(0)
    s = pl.program_id(1)
    n_used = nu_ref[0]
    tm, half = xlo_ref.shape
    hr = tm // 2
    halves = (slice(0, hr), slice(hr, tm))

    @pl.when(i < n_used)
    def _():
        second_half = nv_ref[i] > hr

        @pl.when(s == 0)
        def _():
            for slot in range(2):
                start_this, wait_this = _row_gather(tok_ref, h_hbm, xbuf_ref.at[slot], sem.at[slot], tm)
                start_next, _ = _row_gather(tok_next_ref, h_hbm, xbuf_ref.at[1 - slot], sem.at[1 - slot], tm)

                @pl.when(lax.rem(i, 2) == slot)
                def _():
                    @pl.when(i == 0)
                    def _():
                        start_this()
                    wait_this()

                    @pl.when(i + 1 < n_used)
                    def _():
                        start_next()
                    lo, hi = _unpack_bf16_pair(xbuf_ref[slot])
                    xlo_ref[...] = lo.astype(BF16)
                    xhi_ref[...] = hi.astype(BF16)

        @pl.when(s < n_f)
        def _():
            wg = wg_ref[...].astype(BF16)
            wu = wu_ref[...].astype(BF16)

            def hidden(rows):
                xl, xh = xlo_ref[rows, :], xhi_ref[rows, :]
                gate = (jnp.dot(xl, wg[:half], preferred_element_type=F32)
                        + jnp.dot(xh, wg[half:], preferred_element_type=F32))
                up = (jnp.dot(xl, wu[:half], preferred_element_type=F32)
                      + jnp.dot(xh, wu[half:], preferred_element_type=F32))
                hmid_ref[s, rows, :] = (gate * jax.nn.sigmoid(gate) * up).astype(BF16)

            hidden(halves[0])

            @pl.when(second_half)
            def _():
                hidden(halves[1])

        @pl.when(s >= n_f)
        def _():
            wd = wd_ref[...].astype(BF16)
            tn = wd.shape[1]

            def project(rows):
                hm = jnp.concatenate([hmid_ref[f, rows, :] for f in range(n_f)], axis=1)
                y = jnp.dot(hm, wd, preferred_element_type=F32)
                o_ref[rows, :] = _pack_bf16_pair(y[:, :tn // 2], y[:, tn // 2:])

            project(halves[0])

            @pl.when(second_half)
            def _():
                project(halves[1])

            @pl.when(jnp.logical_not(second_half))
            def _():
                o_ref[halves[1], :] = jnp.zeros((hr, o_ref.shape[1]), o_ref.dtype)

    @pl.when(jnp.logical_and(i >= n_used, s >= n_f))
    def _():
        o_ref[...] = jnp.zeros_like(o_ref)


def moe_experts(h_packed, slot_tok, block_e, n_used, n_valid, w_gate, w_up, w_down, *, tm, tn, layer):
    d = 2 * h_packed.shape[1]
    p = slot_tok.shape[0]
    f_dim = w_gate.shape[3]
    tf = _pick(f_dim, (128,))
    n_f, n_n = f_dim // tf, d // tn
    n_blocks = p // tm
    tok_blocks = slot_tok.reshape(n_blocks, 1, tm)

    def blk(i, nu):
        return jnp.minimum(i, nu[0] - 1)

    def f_blk(i, s, nu):
        return jnp.where(i < nu[0], jnp.minimum(s, n_f - 1), n_f - 1)

    def n_blk(i, s, nu):
        return jnp.where(i < nu[0], jnp.maximum(s - n_f, 0), n_n - 1)

    grid_spec = pltpu.PrefetchScalarGridSpec(
        num_scalar_prefetch=3,
        grid=(n_blocks, n_f + n_n),
        in_specs=[pl.BlockSpec((None, 1, tm), lambda i, s, be, nu, nv: (blk(i, nu), 0, 0), memory_space=pltpu.SMEM),
                  pl.BlockSpec((None, 1, tm), lambda i, s, be, nu, nv: (blk(i + 1, nu), 0, 0), memory_space=pltpu.SMEM),
                  pl.BlockSpec(memory_space=pl.ANY),
                  pl.BlockSpec((None, None, d, tf), lambda i, s, be, nu, nv: (layer, be[blk(i, nu)], 0, f_blk(i, s, nu))),
                  pl.BlockSpec((None, None, d, tf), lambda i, s, be, nu, nv: (layer, be[blk(i, nu)], 0, f_blk(i, s, nu))),
                  pl.BlockSpec((None, None, f_dim, tn), lambda i, s, be, nu, nv: (layer, be[blk(i, nu)], 0, n_blk(i, s, nu)))],
        out_specs=pl.BlockSpec((tm, tn // 2), lambda i, s, be, nu, nv: (i, jnp.maximum(s - n_f, 0))),
        scratch_shapes=[pltpu.VMEM((2, tm, d // 2), jnp.uint32), pltpu.VMEM((tm, d // 2), BF16),
                        pltpu.VMEM((tm, d // 2), BF16), pltpu.VMEM((n_f, tm, tf), BF16),
                        pltpu.SemaphoreType.DMA((2,))],
    )
    return pl.pallas_call(
        functools.partial(_moe_kernel, n_f=n_f),
        grid_spec=grid_spec,
        out_shape=jax.ShapeDtypeStruct((p, d // 2), jnp.uint32),
        compiler_params=_params("arbitrary", "arbitrary"),
        name="moe_experts",
    )(block_e, n_used, n_valid, tok_blocks, tok_blocks, h_packed, w_gate, w_up, w_down)


def _moe_combine_kernel(dest_ref, y_hbm, x_ref, gate_ref, mod_ref, o_ref, rows_ref, sem, *, gate_idx, tn):
    tc, d = x_ref.shape
    start, wait = _row_gather(dest_ref, y_hbm, rows_ref, sem, 2 * tc)
    start()
    wait()
    g0 = gate_ref[:, 0:1]
    g1 = gate_ref[:, 1:2]
    for n in range(d // tn):
        words = slice(n * tn // 2, (n + 1) * tn // 2)
        lo0, hi0 = _unpack_bf16_pair(rows_ref[:tc, words])
        lo1, hi1 = _unpack_bf16_pair(rows_ref[tc:, words])
        for part, y in enumerate((lo0 * g0 + lo1 * g1, hi0 * g0 + hi1 * g1)):
            cs = slice(n * tn + part * tn // 2, n * tn + (part + 1) * tn // 2)
            o_ref[:, cs] = x_ref[:, cs] + mod_ref[gate_idx:gate_idx + 1, cs] * y


def moe_combine_residual(y_packed, dest, gates, x, mod, *, gate_idx, tc, tn, row_fn):
    t, d = x.shape
    n_tiles = t // tc
    dest_tiles = dest.reshape(n_tiles, tc, TOP_K).transpose(0, 2, 1).reshape(n_tiles, 1, TOP_K * tc)
    return pl.pallas_call(
        functools.partial(_moe_combine_kernel, gate_idx=gate_idx, tn=tn),
        grid=(n_tiles,),
        in_specs=[pl.BlockSpec((None, 1, TOP_K * tc), lambda i: (i, 0, 0), memory_space=pltpu.SMEM),
                  pl.BlockSpec(memory_space=pl.ANY),
                  pl.BlockSpec((tc, d), lambda i: (i, 0)),
                  pl.BlockSpec((tc, gates.shape[1]), lambda i: (i, 0)),
                  pl.BlockSpec((None, mod.shape[1], d), lambda i: (row_fn(i), 0, 0))],
        out_specs=pl.BlockSpec((tc, d), lambda i: (i, 0)),
        out_shape=jax.ShapeDtypeStruct((t, d), F32),
        scratch_shapes=[pltpu.VMEM((TOP_K * tc, d // 2), jnp.uint32), pltpu.SemaphoreType.DMA(())],
        compiler_params=_params("arbitrary"),
        name="moe_combine",
    )(dest_tiles, y_packed, x, gates, mod)


def _rms(x, g):
    return x * lax.rsqrt(jnp.mean(x * x, axis=-1, keepdims=True) + EPS) * g


def _rope_tables(t_prompt, dec_batch, dec_seq, width):
    a = width // 2
    half = a // 2
    freqs = ROPE_THETA ** (-np.arange(half, dtype=np.float32) / half)
    t = np.arange(dec_seq)
    row, col = t // GRID_W, t % GRID_W
    ang_r = row[:, None].astype(np.float32) * freqs[None, :]
    ang_c = col[:, None].astype(np.float32) * freqs[None, :]
    cos = np.concatenate([np.cos(ang_r), np.cos(ang_r), np.cos(ang_c), np.cos(ang_c)], axis=-1)
    sin = np.concatenate([-np.sin(ang_r), np.sin(ang_r), -np.sin(ang_c), np.sin(ang_c)], axis=-1)
    cos = np.concatenate([np.ones((t_prompt, width), np.float32), np.tile(cos, (dec_batch, 1))], axis=0)
    sin = np.concatenate([np.zeros((t_prompt, width), np.float32), np.tile(sin, (dec_batch, 1))], axis=0)
    return jnp.asarray(cos), jnp.asarray(sin)


def _gdn_gates(a, b, a_log, dt_bias, heads):
    t = a.shape[0]
    g = -jnp.exp(a_log.reshape(2 * heads)) * jax.nn.softplus(a + dt_bias.reshape(2 * heads))
    gc = g.reshape(t // GDN_CHUNK, GDN_CHUNK, 2 * heads)
    prefix = jnp.cumsum(gc[..., :heads], axis=1)
    suffix = jnp.flip(jnp.cumsum(jnp.flip(gc[..., heads:], axis=1), axis=1), axis=1)
    dcs = jnp.concatenate([prefix, suffix], axis=-1).reshape(t, 2 * heads)
    return dcs, jax.nn.sigmoid(b)


def _dispatch_plan(experts, n_experts, tm):
    t, k = experts.shape
    a = t * k
    n_blocks = a // tm + n_experts
    flat_e = experts.reshape(-1)
    onehot = (flat_e[:, None] == jnp.arange(n_experts, dtype=jnp.int32)[None, :]).astype(jnp.int32)
    csum = jnp.cumsum(onehot, axis=0)
    pos = jnp.sum((csum - onehot) * onehot, axis=1)
    counts = csum[-1]
    padded = (counts + tm - 1) // tm * tm
    pend = jnp.cumsum(padded)
    pstart = pend - padded
    dest = pstart[flat_e] + pos
    flat_tok = jnp.arange(a, dtype=jnp.int32) // k
    slot_tok = jnp.zeros((n_blocks * tm,), jnp.int32).at[dest].set(flat_tok)
    block_start = jnp.arange(n_blocks, dtype=jnp.int32) * tm
    block_e = jnp.minimum(jnp.searchsorted(pend, block_start, side='right'), n_experts - 1).astype(jnp.int32)
    n_used = (pend[-1] // tm).astype(jnp.int32).reshape(1)
    n_valid = jnp.clip(pstart[block_e] + counts[block_e] - block_start, 0, tm).astype(jnp.int32)
    return dest.reshape(t, k), slot_tok, block_e, n_used, n_valid


def kernel(x_prompt, x_sample, cache_mla_ckv, cache_mla_krope, cache_gqa_k, cache_gqa_v, state_gdn, c, c_ctx, w_mod, b_mod, g_norm_mix, g_norm_ffn, w_in, g_cq, w_q_up, g_ckv, w_kv_up, g_q_mla, g_k_mla, g_q_gqa, g_k_gqa, w_conv, a_log, dt_bias, g_gdn_out, w_out, w_router_group, b_router_group, w_router_expert, b_router_expert, w_gate, w_up, w_down):
    batch, seq, d = x_prompt.shape
    dec_batch, dec_seq, _ = x_sample.shape
    depth = w_in.shape[0]
    past = cache_mla_ckv.shape[2]
    q_lora = w_q_up.shape[1]
    kv_lora = w_kv_up.shape[1]
    mla_heads = w_q_up.shape[2] // MLA_QK
    kv_heads = cache_gqa_k.shape[3]
    gdn_heads = state_gdn.shape[3]
    gqa_heads = (w_out.shape[1] - mla_heads * MLA_V - gdn_heads * GDN_DV) // HEAD_DIM
    group = gqa_heads // kv_heads
    n_groups = w_router_group.shape[2]
    n_experts = w_gate.shape[1]
    gdn_qkv = gdn_heads * (2 * GDN_DK + GDN_DV)
    gdn_width = gdn_heads * GDN_DV

    t_p = batch * seq
    t_s = dec_batch * dec_seq
    t_all = t_p + t_s
    tm = _pick(math.gcd(t_p, dec_seq), (512, 256, 128))
    row_fn = _mod_row_fn(tm, t_p, dec_seq)
    moe_tm = _pick(t_all * TOP_K, (1024, 512, 256, 128))
    moe_tn = _pick(d, (1024, 512, 256))
    combine_tc = _pick(math.gcd(t_p, dec_seq), (256, 128))

    src_sizes = (q_lora, kv_lora, MLA_ROPE, gqa_heads * HEAD_DIM, kv_heads * HEAD_DIM, kv_heads * HEAD_DIM,
                 gdn_qkv, gdn_width, 2 * gdn_heads, 2 * gdn_heads)
    src_off = np.concatenate([[0], np.cumsum(src_sizes)])
    src = dict(zip(('cq', 'ckv', 'krope', 'gq', 'gk', 'gv', 'qkv', 'z', 'a', 'b'),
                   [(int(src_off[j]), int(src_off[j + 1])) for j in range(len(src_sizes))]))
    tail_w = -(-(MLA_ROPE + 4 * gdn_heads) // LANE) * LANE
    cols, pieces, pos = {}, [], 0
    for name, width in (('qkv', gdn_qkv), ('z', gdn_width), ('ckv', kv_lora), ('gq', gqa_heads * HEAD_DIM),
                        ('cq', q_lora), ('tail', tail_w), ('gk', kv_heads * HEAD_DIM), ('gv', kv_heads * HEAD_DIM)):
        start = -(-pos // width) * width
        assert width % LANE == 0
        if start > pos:
            pieces.append(('pad', start - pos))
        pieces.append((name, width))
        cols[name] = (start, width)
        pos = start + width
    n_in = pos
    tail0 = cols['tail'][0]

    def build_w_in(w):
        parts = []
        for name, width in pieces:
            if name == 'pad':
                parts.append(jnp.zeros((d, width), F32))
            elif name == 'tail':
                parts += [w[:, src[s][0]:src[s][1]] for s in ('krope', 'a', 'b')]
                parts.append(jnp.zeros((d, width - MLA_ROPE - 4 * gdn_heads), F32))
            else:
                parts.append(w[:, src[name][0]:src[name][1]])
        return jnp.concatenate(parts, axis=1).astype(BF16)

    def padded_rope_tables(width):
        cos, sin = _rope_tables(t_p, dec_batch, dec_seq, width)
        return (jnp.pad(cos, ((0, 0), (0, LANE - width)), constant_values=1.0),
                jnp.pad(sin, ((0, 0), (0, LANE - width))))

    cos_g, sin_g = padded_rope_tables(HEAD_DIM)
    cos_m, sin_m = padded_rope_tables(MLA_ROPE)
    n_ctx = dec_batch * past
    cos_id, sin_id = jnp.ones((n_ctx, LANE), F32), jnp.zeros((n_ctx, LANE), F32)
    sequences = ([(b * seq, seq) for b in range(batch)]
                 + [(t_p + b * dec_seq, dec_seq) for b in range(dec_batch)])

    cond8 = jnp.zeros((8, d), F32).at[0].set(c_ctx).at[1:1 + dec_batch].set(c)

    x = jnp.concatenate([x_prompt.reshape(t_p, d), x_sample.reshape(t_s, d)], axis=0)
    new_ckv, new_krope, new_k, new_v, new_s = [], [], [], [], []
    q_scale_m = MLA_QK ** -0.5 * LOG2E
    q_scale_g = HEAD_DIM ** -0.5 * LOG2E

    for l in range(depth):
        mod = modulation(cond8, w_mod, b_mod, l).reshape(8, 6, d)

        u = norm_matmul(x, g_norm_mix[l], mod, build_w_in(w_in[l]), sh_idx=0, sc_idx=1, tm=tm, row_fn=row_fn)

        wq = w_q_up[l].reshape(q_lora, mla_heads, MLA_QK)
        wq = jnp.pad(wq, ((0, 0), (0, 0), (0, MLA_PAD - MLA_QK))).reshape(q_lora, mla_heads * MLA_PAD).astype(BF16)
        q_m = mla_queries(u, cols, g_cq[l], wq, jnp.pad(g_q_mla[l], (0, MLA_PAD - MLA_QK)), cos_m, sin_m,
                          tm=tm, q_scale=q_scale_m)
        wkv = w_kv_up[l].reshape(kv_lora, mla_heads, MLA_NOPE + MLA_V)
        wkv = jnp.concatenate([wkv[..., :MLA_NOPE].reshape(kv_lora, -1), wkv[..., MLA_NOPE:].reshape(kv_lora, -1)],
                              axis=1).astype(BF16)
        g_k_pad = jnp.pad(g_k_mla[l], (0, MLA_PAD - MLA_QK))
        ckv_n, k_m, v_m = mla_keys_values(u, cols['ckv'], u, cols['tail'], g_ckv[l], wkv, g_k_pad, cos_m, sin_m,
                                          heads=mla_heads, tm=tm, normalise=True)
        ctx_krope = jnp.pad(cache_mla_krope[:, l].reshape(n_ctx, MLA_ROPE), ((0, 0), (0, LANE - MLA_ROPE)))
        _, k_m_ctx, v_m_ctx = mla_keys_values(cache_mla_ckv[:, l].reshape(n_ctx, kv_lora), (0, kv_lora), ctx_krope,
                                              (0, LANE), g_ckv[l], wkv, g_k_pad, cos_id, sin_id, heads=mla_heads,
                                              tm=_pick(n_ctx, (512, 256, 128)), normalise=False)

        mla_common = dict(kv_heads=mla_heads, group=1, dq=MLA_PAD, dv=MLA_V, out_rows=t_all)
        o_m = attention(q_m, 0, [(k_m, v_m, 0, seq)], batch=batch, lq=seq, out_row0=0,
                        prev_out=jnp.zeros((t_all, mla_heads * MLA_V), BF16), name="mla_attn_ctx", **mla_common)
        o_m = attention(q_m, t_p, [(k_m_ctx, v_m_ctx, 0, past), (k_m, v_m, t_p, dec_seq)], batch=dec_batch,
                        lq=dec_seq, out_row0=t_p, prev_out=o_m, name="mla_attn_lat", **mla_common)

        q_g, k_g_rot, k_g, v_g_b = gqa_prep(u, cols, g_q_gqa[l], g_k_gqa[l], cos_g, sin_g, tm=tm, q_scale=q_scale_g)
        ctx_k = cache_gqa_k[:, l].reshape(n_ctx, kv_heads * HEAD_DIM).astype(BF16)
        ctx_v = cache_gqa_v[:, l].reshape(n_ctx, kv_heads * HEAD_DIM).astype(BF16)
        gqa_common = dict(kv_heads=kv_heads, group=group, dq=HEAD_DIM, dv=HEAD_DIM, out_rows=t_all)
        o_g = attention(q_g, 0, [(k_g_rot, v_g_b, 0, seq)], batch=batch, lq=seq, out_row0=0,
                        prev_out=jnp.zeros((t_all, gqa_heads * HEAD_DIM), BF16), name="gqa_attn_ctx", **gqa_common)
        o_g = attention(q_g, t_p, [(ctx_k, ctx_v, 0, past), (k_g_rot, v_g_b, t_p, dec_seq)], batch=dec_batch,
                        lq=dec_seq, out_row0=t_p, prev_out=o_g, name="gqa_attn_lat", **gqa_common)

        qkv_c = gdn_prep(u, cols['qkv'], w_conv[l], heads=gdn_heads, tm=tm, t_prompt=t_p, seq_lens=(seq, dec_seq))
        tail = u[:, tail0:tail0 + MLA_ROPE + 4 * gdn_heads]
        dcs, beta = _gdn_gates(tail[:, MLA_ROPE:MLA_ROPE + 2 * gdn_heads], tail[:, MLA_ROPE + 2 * gdn_heads:],
                               a_log[l], dt_bias[l], gdn_heads)
        uw = gdn_solve(qkv_c, dcs, beta, heads=gdn_heads)
        s0 = jnp.concatenate([jnp.zeros((batch, 2, gdn_heads, GDN_DK, GDN_DV), F32), state_gdn[:, l]], axis=0)
        o_fwd, o_bwd, s_fin = gdn_scan(qkv_c, uw, dcs, s0, sequences=sequences, heads=gdn_heads)

        new_ckv.append(ckv_n[:t_p].reshape(batch, seq, kv_lora))
        new_krope.append(tail[:t_p, :MLA_ROPE].reshape(batch, seq, MLA_ROPE))
        new_k.append(k_g[:t_p].reshape(batch, seq, kv_heads, HEAD_DIM))
        new_v.append(u[:t_p, cols['gv'][0]:cols['gv'][0] + cols['gv'][1]].reshape(batch, seq, kv_heads, HEAD_DIM))
        new_s.append(s_fin[:batch])

        x = out_proj_residual(o_m, o_g, o_fwd, o_bwd, u, cols['z'], g_gdn_out[l], w_out[l].astype(BF16), x, mod,
                              gate_idx=2, tm=tm, row_fn=row_fn)

        w_router = jnp.concatenate([w_router_group[l], w_router_expert[l],
                                    jnp.zeros((d, LANE - n_groups - n_experts), F32)], axis=1)
        b_router = jnp.concatenate([b_router_group[l], b_router_expert[l],
                                    jnp.zeros((LANE - n_groups - n_experts,), F32)]).reshape(1, LANE)
        h2, gates, experts = ffn_norm_router(x, g_norm_ffn[l], mod, w_router, b_router, sh_idx=3, sc_idx=4, tm=tm,
                                             row_fn=row_fn, n_groups=n_groups, epg=n_experts // n_groups)
        dest, slot_tok, block_e, n_used, n_valid = _dispatch_plan(experts[:, :TOP_K], n_experts, moe_tm)
        yb = moe_experts(h2, slot_tok, block_e, n_used, n_valid, w_gate, w_up, w_down, tm=moe_tm, tn=moe_tn, layer=l)
        x = moe_combine_residual(yb, dest, gates, x, mod, gate_idx=5, tc=combine_tc, tn=moe_tn,
                                 row_fn=_mod_row_fn(combine_tc, t_p, dec_seq))

    return (x[:t_p].reshape(batch, seq, d), x[t_p:].reshape(dec_batch, dec_seq, d),
            jnp.stack(new_ckv, axis=1), jnp.stack(new_krope, axis=1), jnp.stack(new_k, axis=1),
            jnp.stack(new_v, axis=1), jnp.stack(new_s, axis=1))
```

```python
import functools
import math

import numpy as np
import jax
import jax.numpy as jnp
from jax import lax
from jax.experimental import pallas as pl
from jax.experimental.pallas import tpu as pltpu

F32 = jnp.float32
BF16 = jnp.bfloat16

EPS = 1e-6
ROPE_THETA = 10000.0
GRID_W = 64
HEAD_DIM = 128
MLA_NOPE = 128
MLA_ROPE = 64
MLA_V = 128
MLA_QK = MLA_NOPE + MLA_ROPE
MLA_PAD = 256
GDN_DK = 128
GDN_DV = 128
GDN_CHUNK = 64
TOP_K = 2
LANE = 128
VMEM_LIMIT_BYTES = 56 * 1024 * 1024
LOG2E = 1.4426950408889634
ATTN_UNROLL = 4
GATHER_UNROLL = 8


def _params(*sem):
    return pltpu.CompilerParams(dimension_semantics=sem, vmem_limit_bytes=VMEM_LIMIT_BYTES)


def _pick(n, prefs):
    for p in prefs:
        if n % p == 0:
            return p
    return n


def _mod_kernel(c_ref, w_ref, b_ref, o_ref):
    c = c_ref[...]
    s = (c * jax.nn.sigmoid(c)).astype(BF16)
    o_ref[...] = jnp.dot(s, w_ref[...].astype(BF16), preferred_element_type=F32) + b_ref[...]


def modulation(cond8, w_mod, b_mod, layer):
    rows, d = cond8.shape
    depth, _, n = w_mod.shape
    tn = _pick(n, (512, 256, 128))
    return pl.pallas_call(
        _mod_kernel,
        grid=(n // tn,),
        in_specs=[pl.BlockSpec((rows, d), lambda j: (0, 0)),
                  pl.BlockSpec((None, d, tn), lambda j: (layer, 0, j)),
                  pl.BlockSpec((None, 1, tn), lambda j: (layer, 0, j))],
        out_specs=pl.BlockSpec((rows, tn), lambda j: (0, j)),
        out_shape=jax.ShapeDtypeStruct((rows, n), F32),
        compiler_params=_params("parallel"),
        name="modulation",
    )(cond8, w_mod, b_mod.reshape(depth, 1, n))


def _mod_row_fn(tm, t_prompt, dec_seq):
    n_p = t_prompt // tm
    per_b = dec_seq // tm

    def row(i):
        return jnp.where(i < n_p, 0, 1 + (i - n_p) // per_b)
    return row


def _modulated_norm(x, g, sc, sh):
    ms = jnp.mean(x * x, axis=-1, keepdims=True)
    return (x * lax.rsqrt(ms + EPS) * g) * (1.0 + sc) + sh


def _norm_matmul_kernel(x_ref, g_ref, mod_ref, w_ref, o_ref, h_ref, *, sh_idx, sc_idx):
    @pl.when(pl.program_id(1) == 0)
    def _():
        h = _modulated_norm(x_ref[...], g_ref[...], mod_ref[sc_idx:sc_idx + 1, :],
                            mod_ref[sh_idx:sh_idx + 1, :])
        h_ref[...] = h.astype(BF16)
    o_ref[...] = jnp.dot(h_ref[...], w_ref[...], preferred_element_type=F32).astype(o_ref.dtype)


def norm_matmul(x, g, mod, w, *, sh_idx, sc_idx, tm, row_fn, out_dtype=F32):
    t, d = x.shape
    n = w.shape[1]
    tn = _pick(n, (1152, 1024, 512, 384, 256, 128))
    return pl.pallas_call(
        functools.partial(_norm_matmul_kernel, sh_idx=sh_idx, sc_idx=sc_idx),
        grid=(t // tm, n // tn),
        in_specs=[pl.BlockSpec((tm, d), lambda i, j: (i, 0)),
                  pl.BlockSpec((1, d), lambda i, j: (0, 0)),
                  pl.BlockSpec((None, mod.shape[1], d), lambda i, j: (row_fn(i), 0, 0)),
                  pl.BlockSpec((d, tn), lambda i, j: (0, j))],
        out_specs=pl.BlockSpec((tm, tn), lambda i, j: (i, j)),
        out_shape=jax.ShapeDtypeStruct((t, n), out_dtype),
        scratch_shapes=[pltpu.VMEM((tm, d), BF16)],
        compiler_params=_params("parallel", "arbitrary"),
        name="norm_in_proj",
    )(x, g.reshape(1, d), mod, w)


def _out_proj_kernel(om_ref, og_ref, of_ref, ob_ref, z_ref, gd_ref, wm_ref, wg_ref, wd_ref, x_ref, mod_ref,
                     o_ref, od_ref, *, gate_idx):
    @pl.when(pl.program_id(1) == 0)
    def _():
        for h in range(of_ref.shape[1] // GDN_DV):
            hl = slice(h * GDN_DV, (h + 1) * GDN_DV)
            z = z_ref[:, hl]
            od_ref[:, hl] = (_head_rms(of_ref[:, hl] + ob_ref[:, hl], gd_ref[...])
                             * (z * jax.nn.sigmoid(z))).astype(BF16)
    acc = jnp.dot(om_ref[...], wm_ref[...], preferred_element_type=F32)
    acc += jnp.dot(og_ref[...], wg_ref[...], preferred_element_type=F32)
    acc += jnp.dot(od_ref[...], wd_ref[...], preferred_element_type=F32)
    o_ref[...] = x_ref[...] + mod_ref[gate_idx:gate_idx + 1, :] * acc


def out_proj_residual(o_m, o_g, o_fwd, o_bwd, u, z_col, g_gdn, w_out, x, mod, *, gate_idx, tm, row_fn):
    t, d = x.shape
    km, kg, kd = o_m.shape[1], o_g.shape[1], o_fwd.shape[1]
    z0, zw = z_col
    assert km == kg and (km + kg) % kd == 0 and zw == kd
    tn = _pick(d, (1024, 512, 256, 128))
    return pl.pallas_call(
        functools.partial(_out_proj_kernel, gate_idx=gate_idx),
        grid=(t // tm, d // tn),
        in_specs=[pl.BlockSpec((tm, km), lambda i, j: (i, 0)),
                  pl.BlockSpec((tm, kg), lambda i, j: (i, 0)),
                  pl.BlockSpec((tm, kd), lambda i, j: (i, 0)),
                  pl.BlockSpec((tm, kd), lambda i, j: (i, 0)),
                  pl.BlockSpec((tm, zw), lambda i, j: (i, z0 // zw)),
                  pl.BlockSpec((1, GDN_DV), lambda i, j: (0, 0)),
                  pl.BlockSpec((km, tn), lambda i, j: (0, j)),
                  pl.BlockSpec((kg, tn), lambda i, j: (1, j)),
                  pl.BlockSpec((kd, tn), lambda i, j: ((km + kg) // kd, j)),
                  pl.BlockSpec((tm, tn), lambda i, j: (i, j)),
                  pl.BlockSpec((None, mod.shape[1], tn), lambda i, j: (row_fn(i), 0, j))],
        out_specs=pl.BlockSpec((tm, tn), lambda i, j: (i, j)),
        out_shape=jax.ShapeDtypeStruct((t, d), F32),
        scratch_shapes=[pltpu.VMEM((tm, kd), BF16)],
        compiler_params=_params("parallel", "arbitrary"),
        name="out_proj",
    )(o_m, o_g, o_fwd, o_bwd, u, g_gdn.reshape(1, GDN_DV), w_out, w_out, w_out, x, mod)


def _pack_bf16_pair(lo, hi):
    lo_bits = lax.bitcast_convert_type(lo.astype(BF16).astype(F32), jnp.uint32)
    hi_bits = lax.bitcast_convert_type(hi.astype(BF16).astype(F32), jnp.uint32)
    return jnp.right_shift(lo_bits, jnp.uint32(16)) | (hi_bits & jnp.uint32(0xFFFF0000))


def _unpack_bf16_pair(word):
    lo = lax.bitcast_convert_type(jnp.left_shift(word, jnp.uint32(16)), F32)
    hi = lax.bitcast_convert_type(word & jnp.uint32(0xFFFF0000), F32)
    return lo, hi


def _ffn_norm_router_kernel(x_ref, g_ref, mod_ref, wr_ref, br_ref, h_ref, gate_ref, exp_ref, *,
                            sh_idx, sc_idx, n_groups, epg):
    h = _modulated_norm(x_ref[...], g_ref[...], mod_ref[sc_idx:sc_idx + 1, :],
                        mod_ref[sh_idx:sh_idx + 1, :])
    half = h.shape[1] // 2
    h_ref[...] = _pack_bf16_pair(h[:, :half], h[:, half:])
    lg = jnp.dot(h, wr_ref[...], preferred_element_type=F32, precision=lax.Precision.HIGHEST) + br_ref[...]
    lane = lax.broadcasted_iota(jnp.int32, lg.shape, 1).astype(F32)
    neg = jnp.float32(-1e30)
    far = jnp.float32(2 * LANE)

    def first_argmax(v):
        top = jnp.max(v, axis=-1, keepdims=True)
        return top, jnp.min(jnp.where(v == top, lane, far), axis=-1, keepdims=True)

    is_group = lane < n_groups
    g_max, g_idx = first_argmax(jnp.where(is_group, lg, neg))
    g_top = 1.0 / jnp.sum(jnp.where(is_group, jnp.exp(lg - g_max), 0.0), axis=-1, keepdims=True)
    lo = n_groups + g_idx * epg
    el = jnp.where(jnp.logical_and(lane >= lo, lane < lo + epg), lg, neg)
    e1, i1 = first_argmax(el)
    e2, i2 = first_argmax(jnp.where(lane == i1, neg, el))
    r = jnp.exp(e2 - e1)
    gate1 = g_top / (1.0 + r)
    gate2 = g_top * r / (1.0 + r)
    gate_ref[...] = jnp.where(lane == 0, gate1, jnp.where(lane == 1, gate2, 0.0))
    exp_ref[...] = jnp.where(lane == 0, i1 - n_groups, jnp.where(lane == 1, i2 - n_groups, 0.0)).astype(jnp.int32)


def ffn_norm_router(x, g, mod, w_router, b_router, *, sh_idx, sc_idx, tm, row_fn, n_groups, epg):
    t, d = x.shape
    nr = w_router.shape[1]
    assert d % (2 * LANE) == 0
    return pl.pallas_call(
        functools.partial(_ffn_norm_router_kernel, sh_idx=sh_idx, sc_idx=sc_idx, n_groups=n_groups, epg=epg),
        grid=(t // tm,),
        in_specs=[pl.BlockSpec((tm, d), lambda i: (i, 0)),
                  pl.BlockSpec((1, d), lambda i: (0, 0)),
                  pl.BlockSpec((None, mod.shape[1], d), lambda i: (row_fn(i), 0, 0)),
                  pl.BlockSpec((d, nr), lambda i: (0, 0)),
                  pl.BlockSpec((1, nr), lambda i: (0, 0))],
        out_specs=[pl.BlockSpec((tm, d // 2), lambda i: (i, 0)),
                   pl.BlockSpec((tm, nr), lambda i: (i, 0)),
                   pl.BlockSpec((tm, nr), lambda i: (i, 0))],
        out_shape=[jax.ShapeDtypeStruct((t, d // 2), jnp.uint32), jax.ShapeDtypeStruct((t, nr), F32),
                   jax.ShapeDtypeStruct((t, nr), jnp.int32)],
        compiler_params=_params("parallel"),
        name="ffn_norm_router",
    )(x, g.reshape(1, d), mod, w_router, b_router)


def _rope_rotate(x, cos, sin, pair):
    lane = lax.broadcasted_iota(jnp.int32, x.shape, 1)
    first = (lane & (2 * pair - 1)) < pair
    partner = jnp.where(first, pltpu.roll(x, LANE - pair, axis=1), pltpu.roll(x, pair, axis=1))
    return x * cos + partner * sin


def _head_rms(x, g):
    return x * lax.rsqrt(jnp.mean(x * x, axis=-1, keepdims=True) + EPS) * g


def _gqa_prep_kernel(gq_ref, gk_ref, gv_ref, gq_gain_ref, gk_gain_ref, cos_ref, sin_ref,
                     q_ref, k_rot_ref, k_ref, v_ref, *, q_scale):
    cos, sin = cos_ref[...], sin_ref[...]
    pair = HEAD_DIM // 4
    for h in range(gq_ref.shape[1] // HEAD_DIM):
        hl = slice(h * HEAD_DIM, (h + 1) * HEAD_DIM)
        q = _rope_rotate(_head_rms(gq_ref[:, hl], gq_gain_ref[...]), cos, sin, pair)
        q_ref[:, hl] = (q * q_scale).astype(BF16)
    for h in range(gk_ref.shape[1] // HEAD_DIM):
        hl = slice(h * HEAD_DIM, (h + 1) * HEAD_DIM)
        k = _head_rms(gk_ref[:, hl], gk_gain_ref[...])
        k_ref[:, hl] = k
        k_rot_ref[:, hl] = _rope_rotate(k, cos, sin, pair).astype(BF16)
    v_ref[...] = gv_ref[...].astype(BF16)


def gqa_prep(u, cols, g_q, g_k, cos, sin, *, tm, q_scale):
    t = u.shape[0]
    (q0, qw), (k0, kw), (v0, vw) = cols['gq'], cols['gk'], cols['gv']
    row = lambda i: (i, 0)
    return pl.pallas_call(
        functools.partial(_gqa_prep_kernel, q_scale=q_scale),
        grid=(t // tm,),
        in_specs=[pl.BlockSpec((tm, qw), lambda i: (i, q0 // qw)),
                  pl.BlockSpec((tm, kw), lambda i: (i, k0 // kw)),
                  pl.BlockSpec((tm, vw), lambda i: (i, v0 // vw)),
                  pl.BlockSpec((1, HEAD_DIM), lambda i: (0, 0)),
                  pl.BlockSpec((1, HEAD_DIM), lambda i: (0, 0)),
                  pl.BlockSpec((tm, LANE), row),
                  pl.BlockSpec((tm, LANE), row)],
        out_specs=[pl.BlockSpec((tm, qw), row), pl.BlockSpec((tm, kw), row),
                   pl.BlockSpec((tm, kw), row), pl.BlockSpec((tm, vw), row)],
        out_shape=[jax.ShapeDtypeStruct((t, qw), BF16), jax.ShapeDtypeStruct((t, kw), BF16),
                   jax.ShapeDtypeStruct((t, kw), F32), jax.ShapeDtypeStruct((t, vw), BF16)],
        compiler_params=_params("parallel"),
        name="gqa_prep",
    )(u, u, u, g_q.reshape(1, HEAD_DIM), g_k.reshape(1, HEAD_DIM), cos, sin)


def _mla_q_kernel(cq_ref, g_cq_ref, wq_ref, g_q_ref, cos_ref, sin_ref, q_ref, *, q_scale):
    cqn = _head_rms(cq_ref[...], g_cq_ref[...]).astype(BF16)
    q = jnp.dot(cqn, wq_ref[...], preferred_element_type=F32)
    cos, sin = cos_ref[...], sin_ref[...]
    g_nope, g_rope = g_q_ref[:, :MLA_NOPE], g_q_ref[:, MLA_NOPE:]
    for h in range(q.shape[1] // MLA_PAD):
        nope = q[:, h * MLA_PAD:h * MLA_PAD + MLA_NOPE]
        rope = q[:, h * MLA_PAD + MLA_NOPE:(h + 1) * MLA_PAD]
        ss = jnp.sum(nope * nope, axis=-1, keepdims=True) + jnp.sum(rope * rope, axis=-1, keepdims=True)
        r = lax.rsqrt(ss / MLA_QK + EPS) * q_scale
        q_ref[:, h * MLA_PAD:h * MLA_PAD + MLA_NOPE] = (nope * r * g_nope).astype(BF16)
        q_ref[:, h * MLA_PAD + MLA_NOPE:(h + 1) * MLA_PAD] = (
            _rope_rotate(rope * g_rope, cos, sin, MLA_ROPE // 4) * r).astype(BF16)


def mla_queries(u, cols, g_cq, wq, g_q_pad, cos, sin, *, tm, q_scale):
    t = u.shape[0]
    c0, cw = cols['cq']
    n = wq.shape[1]
    return pl.pallas_call(
        functools.partial(_mla_q_kernel, q_scale=q_scale),
        grid=(t // tm,),
        in_specs=[pl.BlockSpec((tm, cw), lambda i: (i, c0 // cw)),
                  pl.BlockSpec((1, cw), lambda i: (0, 0)),
                  pl.BlockSpec((cw, n), lambda i: (0, 0)),
                  pl.BlockSpec((1, MLA_PAD), lambda i: (0, 0)),
                  pl.BlockSpec((tm, LANE), lambda i: (i, 0)),
                  pl.BlockSpec((tm, LANE), lambda i: (i, 0))],
        out_specs=pl.BlockSpec((tm, n), lambda i: (i, 0)),
        out_shape=jax.ShapeDtypeStruct((t, n), BF16),
        compiler_params=_params("parallel"),
        name="mla_queries",
    )(u, g_cq.reshape(1, cw), wq, g_q_pad.reshape(1, MLA_PAD), cos, sin)


def _mla_kv_kernel(ckv_ref, kr_ref, g_ckv_ref, wkv_ref, g_k_ref, cos_ref, sin_ref, ckvn_ref, k_ref, v_ref, *,
                   heads, normalise):
    ckv = ckv_ref[...]
    if normalise:
        ckv = _head_rms(ckv, g_ckv_ref[...])
    ckvn_ref[...] = ckv
    kv = jnp.dot(ckv.astype(BF16), wkv_ref[...], preferred_element_type=F32)
    lane = lax.broadcasted_iota(jnp.int32, (ckv.shape[0], LANE), 1)
    krope = jnp.where(lane < MLA_ROPE, kr_ref[:, :LANE], 0.0)
    ss_rope = jnp.sum(krope * krope, axis=-1, keepdims=True)
    g_nope, g_rope = g_k_ref[:, :MLA_NOPE], g_k_ref[:, MLA_NOPE:]
    rope = _rope_rotate(krope * g_rope, cos_ref[...], sin_ref[...], MLA_ROPE // 4)
    for h in range(heads):
        nope = kv[:, h * MLA_NOPE:(h + 1) * MLA_NOPE]
        r = lax.rsqrt((jnp.sum(nope * nope, axis=-1, keepdims=True) + ss_rope) / MLA_QK + EPS)
        k_ref[:, h * MLA_PAD:h * MLA_PAD + MLA_NOPE] = (nope * r * g_nope).astype(BF16)
        k_ref[:, h * MLA_PAD + MLA_NOPE:(h + 1) * MLA_PAD] = (rope * r).astype(BF16)
    v_ref[...] = kv[:, heads * MLA_NOPE:].astype(BF16)


def mla_keys_values(ckv_src, ckv_col, kr_src, kr_col, g_ckv, wkv, g_k_pad, cos, sin, *, heads, tm, normalise):
    t = ckv_src.shape[0]
    (c0, cw), (r0, rw) = ckv_col, kr_col
    return pl.pallas_call(
        functools.partial(_mla_kv_kernel, heads=heads, normalise=normalise),
        grid=(t // tm,),
        in_specs=[pl.BlockSpec((tm, cw), lambda i: (i, c0 // cw)),
                  pl.BlockSpec((tm, rw), lambda i: (i, r0 // rw)),
                  pl.BlockSpec((1, cw), lambda i: (0, 0)),
                  pl.BlockSpec((cw, wkv.shape[1]), lambda i: (0, 0)),
                  pl.BlockSpec((1, MLA_PAD), lambda i: (0, 0)),
                  pl.BlockSpec((tm, LANE), lambda i: (i, 0)),
                  pl.BlockSpec((tm, LANE), lambda i: (i, 0))],
        out_specs=[pl.BlockSpec((tm, cw), lambda i: (i, 0)),
                   pl.BlockSpec((tm, heads * MLA_PAD), lambda i: (i, 0)),
                   pl.BlockSpec((tm, heads * MLA_V), lambda i: (i, 0))],
        out_shape=[jax.ShapeDtypeStruct((t, cw), F32), jax.ShapeDtypeStruct((t, heads * MLA_PAD), BF16),
                   jax.ShapeDtypeStruct((t, heads * MLA_V), BF16)],
        compiler_params=_params("parallel"),
        name="mla_keys_values",
    )(ckv_src, kr_src, g_ckv.reshape(1, cw), wkv, g_k_pad.reshape(1, MLA_PAD), cos, sin)


def _gdn_prep_kernel(x_ref, prev_ref, next_ref, w_ref, o_ref, *, heads, tm, n_prompt_tiles, seq_lens):
    i = pl.program_id(0)
    x = x_ref[...]
    row = lax.broadcasted_iota(jnp.int32, (tm, 1), 0)
    in_prompt = i < n_prompt_tiles
    seq_len = jnp.where(in_prompt, seq_lens[0], seq_lens[1])
    first_row = jnp.where(in_prompt, i, i - n_prompt_tiles) * tm
    span = [min(s, tm) for s in seq_lens]
    local = jnp.where(in_prompt, row & (span[0] - 1), row & (span[1] - 1))
    span_t = jnp.where(in_prompt, span[0], span[1])
    tile_starts_seq = lax.rem(first_row, seq_len) == 0
    tile_ends_seq = lax.rem(first_row + tm, seq_len) == 0
    is_start = jnp.logical_and(local == 0, jnp.logical_or(row != 0, tile_starts_seq))
    is_end = jnp.logical_and(local == span_t - 1, jnp.logical_or(row != tm - 1, tile_ends_seq))
    prev = jnp.where(row == 0, prev_ref[7:8, :], pltpu.roll(x, 1, axis=0))
    nxt = jnp.where(row == tm - 1, next_ref[0:1, :], pltpu.roll(x, tm - 1, axis=0))
    prev = jnp.where(is_start, 0.0, prev)
    nxt = jnp.where(is_end, 0.0, nxt)
    conv = prev * w_ref[0:1, :] + x * w_ref[1:2, :] + nxt * w_ref[2:3, :]
    act = conv * jax.nn.sigmoid(conv)
    for h in range(3 * heads):
        hl = slice(h * GDN_DK, (h + 1) * GDN_DK)
        slab = act[:, hl]
        if h < 2 * heads:
            slab = slab * lax.rsqrt(jnp.sum(slab * slab, axis=-1, keepdims=True) + EPS)
            if h < heads:
                slab = slab * GDN_DK ** -0.5
        o_ref[:, hl] = slab


def gdn_prep(u, col, w_conv, *, heads, tm, t_prompt, seq_lens):
    t = u.shape[0]
    c0, cw = col
    assert all(s & (s - 1) == 0 for s in (tm,) + tuple(min(s, tm) for s in seq_lens))
    eight = 8
    cb = c0 // cw
    last8 = t // eight - 1
    return pl.pallas_call(
        functools.partial(_gdn_prep_kernel, heads=heads, tm=tm, n_prompt_tiles=t_prompt // tm, seq_lens=seq_lens),
        grid=(t // tm,),
        in_specs=[pl.BlockSpec((tm, cw), lambda i: (i, cb)),
                  pl.BlockSpec((eight, cw), lambda i: (jnp.maximum(i * (tm // eight) - 1, 0), cb)),
                  pl.BlockSpec((eight, cw), lambda i: (jnp.minimum((i + 1) * (tm // eight), last8), cb)),
                  pl.BlockSpec((3, cw), lambda i: (0, 0))],
        out_specs=pl.BlockSpec((tm, cw), lambda i: (i, 0)),
        out_shape=jax.ShapeDtypeStruct((t, cw), F32),
        compiler_params=_params("parallel"),
        name="gdn_prep",
    )(u, u, u, w_conv)


def _attn_kernel(*refs, n_seg, aliased, group, dq, dv, tk):
    q_ref = refs[0]
    kv_refs = [(refs[1 + 2 * s], refs[2 + 2 * s]) for s in range(n_seg)]
    o_ref, s_ref, mx_ref, l_ref, acc_ref = refs[1 + 2 * n_seg + (1 if aliased else 0):]
    tq = q_ref.shape[0]
    q = jnp.concatenate([q_ref[:, g * dq:(g + 1) * dq] for g in range(group)], axis=0)
    slabs = tk // LANE

    def for_chunks(body):
        base = 0
        for k_ref, v_ref in kv_refs:
            nk = k_ref.shape[0] // tk
            main = nk // ATTN_UNROLL

            def group_body(i, carry, k_ref=k_ref, v_ref=v_ref, base=base):
                for u in range(ATTN_UNROLL):
                    body(k_ref, v_ref, i * ATTN_UNROLL + u, base)
                return carry
            if main:
                lax.fori_loop(0, main, group_body, 0)
            for c in range(main * ATTN_UNROLL, nk):
                body(k_ref, v_ref, c, base)
            base += nk

    mx_ref[...] = jnp.full(mx_ref.shape, -jnp.inf, F32)

    def rows_of(c):
        return pl.ds(c * tk if isinstance(c, int) else pl.multiple_of(c * tk, tk), tk)

    def scores(k_ref, v_ref, c, base):
        s = lax.dot_general(q, k_ref[rows_of(c), :], (((1,), (1,)), ((), ())),
                            preferred_element_type=F32)
        s_ref[base + c] = s
        mx = mx_ref[...]
        for j in range(slabs):
            mx = jnp.maximum(mx, s[:, j * LANE:(j + 1) * LANE])
        mx_ref[...] = mx

    for_chunks(scores)
    m = jnp.broadcast_to(jnp.max(mx_ref[...], axis=-1, keepdims=True), mx_ref.shape)
    l_ref[...] = jnp.zeros_like(l_ref)
    acc_ref[...] = jnp.zeros_like(acc_ref)

    def weighted(k_ref, v_ref, c, base):
        s = s_ref[base + c]
        p = jnp.concatenate([jnp.exp2(s[:, j * LANE:(j + 1) * LANE] - m) for j in range(slabs)], axis=1)
        lsum = l_ref[...]
        for j in range(slabs):
            lsum = lsum + p[:, j * LANE:(j + 1) * LANE]
        l_ref[...] = lsum
        acc_ref[...] += jnp.dot(p.astype(BF16), v_ref[rows_of(c), :], preferred_element_type=F32)

    for_chunks(weighted)
    o = acc_ref[...] / jnp.sum(l_ref[...], axis=-1, keepdims=True)
    for g in range(group):
        o_ref[:, g * dv:(g + 1) * dv] = o[g * tq:(g + 1) * tq].astype(o_ref.dtype)


def attention(q, q_row0, kv_segments, *, batch, lq, kv_heads, group, dq, dv, out_rows, out_row0, prev_out, name):
    tq = _pick(lq, (512, 256, 128)) // (2 if group > 1 else 1)
    tk = _pick(math.gcd(*[length for _, _, _, length in kv_segments]), (256, 128))
    nq = lq // tq
    m_rows = group * tq
    assert q_row0 % tq == 0 and out_row0 % tq == 0
    in_specs = [pl.BlockSpec((tq, group * dq), lambda b, h, i: (q_row0 // tq + b * nq + i, h))]
    operands = [q]
    for k, v, first_row, length in kv_segments:
        assert first_row % length == 0
        in_specs += [pl.BlockSpec((length, dq), lambda b, h, i, f=first_row // length: (f + b, h)),
                     pl.BlockSpec((length, dv), lambda b, h, i, f=first_row // length: (f + b, h))]
        operands += [k, v]
    aliases = {}
    if prev_out is not None:
        in_specs.append(pl.BlockSpec(memory_space=pl.ANY))
        aliases = {len(operands): 0}
        operands.append(prev_out)
    n_chunks = sum(length // tk for _, _, _, length in kv_segments)
    return pl.pallas_call(
        functools.partial(_attn_kernel, n_seg=len(kv_segments), aliased=prev_out is not None, group=group,
                          dq=dq, dv=dv, tk=tk),
        grid=(batch, kv_heads, nq),
        in_specs=in_specs,
        out_specs=pl.BlockSpec((tq, group * dv), lambda b, h, i: (out_row0 // tq + b * nq + i, h)),
        out_shape=jax.ShapeDtypeStruct((out_rows, kv_heads * group * dv), BF16),
        input_output_aliases=aliases,
        scratch_shapes=[pltpu.VMEM((n_chunks, m_rows, tk), F32), pltpu.VMEM((m_rows, LANE), F32),
                        pltpu.VMEM((m_rows, LANE), F32), pltpu.VMEM((m_rows, dv), F32)],
        compiler_params=_params("parallel", "parallel", "parallel"),
        name=name,
    )(*operands)


def _split_hi_lo(a):
    hi = a.astype(BF16).astype(F32)
    return hi, (a - hi).astype(BF16).astype(F32)


def _dot_split(a_parts, b_parts):
    ah, al = a_parts
    bh, bl = b_parts
    lhs = jnp.concatenate([ah, al, ah, al], axis=2).astype(BF16)
    rhs = jnp.concatenate([bh, bh, bl, bl], axis=1).astype(BF16)
    return lax.dot_general(lhs, rhs, (((2,), (1,)), ((0,), (0,))), preferred_element_type=F32)


def _dot_nt(a, b):
    return lax.dot_general(a.astype(BF16), b.astype(BF16), (((2,), (2,)), ((0,), (0,))), preferred_element_type=F32)


def _chunk_masks(backward):
    r = lax.broadcasted_iota(jnp.int32, (GDN_CHUNK, GDN_CHUNK), 0)
    c = lax.broadcasted_iota(jnp.int32, (GDN_CHUNK, GDN_CHUNK), 1)
    incl = (r <= c) if backward else (r >= c)
    strict = (r < c) if backward else (r > c)
    return r, c, incl, strict


def _decay_mask(dcol, r, c, incl):
    drow = jnp.sum(jnp.where(r == c, dcol, 0.0), axis=1, keepdims=True)
    return jnp.where(incl, jnp.exp(jnp.where(incl, dcol - drow, 0.0)), 0.0)


def _unit_tri_inverse(a_mat, r, c):
    eye = (r == c).astype(F32)
    same16 = jnp.right_shift(r, 4) == jnp.right_shift(c, 4)
    same32 = jnp.right_shift(r, 5) == jnp.right_shift(c, 5)
    x = jnp.where(same16, -a_mat, 0.0)
    inv = eye + x
    x_parts = _split_hi_lo(x)
    for _ in range(3):
        x_parts = _split_hi_lo(_dot_split(x_parts, x_parts))
        inv = inv + _dot_split(_split_hi_lo(inv), x_parts)
    for off_blocks in (jnp.logical_and(same32, jnp.logical_not(same16)), jnp.logical_not(same32)):
        off = jnp.where(off_blocks, a_mat, 0.0)
        inv_parts = _split_hi_lo(inv)
        inv = inv - _dot_split(inv_parts, _split_hi_lo(_dot_split(_split_hi_lo(off), inv_parts)))
    return inv


def _head_column(ref, lane, index, n_c):
    col = jnp.sum(jnp.where(lane == index, ref[...], 0.0), axis=1, keepdims=True)
    return col.reshape(n_c, GDN_CHUNK, 1)


def _gdn_solve_kernel(k_ref, v_ref, d_ref, b_ref, uw_ref, *, head_lanes):
    h = pl.program_id(1)
    n_c = k_ref.shape[0] // GDN_CHUNK
    lane = lax.broadcasted_iota(jnp.int32, (1, d_ref.shape[1]), 1)
    kc = k_ref[...].reshape(n_c, GDN_CHUNK, GDN_DK)
    vc = v_ref[...].reshape(n_c, GDN_CHUNK, GDN_DV)
    for direction in range(2):
        r, c, incl, strict = _chunk_masks(direction == 1)
        dcol = _head_column(d_ref, lane, direction * head_lanes + h, n_c)
        bcol = _head_column(b_ref, lane, direction * head_lanes + h, n_c)
        kb = kc * bcol
        a_mat = jnp.where(strict, _dot_nt(kb, kc) * _decay_mask(dcol, r, c, incl), 0.0)
        rhs = jnp.concatenate([vc * bcol, kb * jnp.exp(dcol)], axis=2)
        uw = _dot_split(_split_hi_lo(_unit_tri_inverse(a_mat, r, c)), _split_hi_lo(rhs))
        uw_ref[direction] = uw.reshape(n_c * GDN_CHUNK, GDN_DV + GDN_DK)


def gdn_solve(qkv, dcs, beta, *, heads):
    t = qkv.shape[0]
    rows = _pick(t, (512, 256, 128, 64))
    width = GDN_DV + GDN_DK
    return pl.pallas_call(
        functools.partial(_gdn_solve_kernel, head_lanes=heads),
        grid=(t // rows, heads),
        in_specs=[pl.BlockSpec((rows, GDN_DK), lambda i, h: (i, heads + h)),
                  pl.BlockSpec((rows, GDN_DV), lambda i, h: (i, 2 * heads + h)),
                  pl.BlockSpec((rows, 2 * heads), lambda i, h: (i, 0)),
                  pl.BlockSpec((rows, 2 * heads), lambda i, h: (i, 0))],
        out_specs=pl.BlockSpec((2, rows, width), lambda i, h: (0, i, h)),
        out_shape=jax.ShapeDtypeStruct((2, t, heads * width), F32),
        compiler_params=_params("parallel", "parallel"),
        name="gdn_solve",
    )(qkv, qkv, dcs, beta)


def _gdn_scan_kernel(fwd_blk_ref, bwd_blk_ref, seq_ref, edge_ref, qf_ref, kf_ref, uwf_ref, df_ref,
                     qb_ref, kb_ref, uwb_ref, db_ref, s0_ref, of_ref, ob_ref, sfin_ref, s_ref, *,
                     heads_per_step, head_lanes):
    step = pl.program_id(1)
    hg = pl.program_id(0)
    rows = qf_ref.shape[0]
    n_c = rows // GDN_CHUNK
    lane = lax.broadcasted_iota(jnp.int32, (1, df_ref.shape[1]), 1)

    @pl.when((edge_ref[step] & 1) != 0)
    def _():
        s_ref[...] = s0_ref[...].astype(F32)

    def bmm(a, b):
        return lax.dot_general(a, b, (((2,), (1,)), ((0,), (0,))), preferred_element_type=F32)

    width = GDN_DV + GDN_DK
    streams = ((qf_ref, kf_ref, uwf_ref, df_ref, of_ref), (qb_ref, kb_ref, uwb_ref, db_ref, ob_ref))
    for ci in range(n_c):
        for direction, (q_ref, k_ref, uw_ref, d_ref, o_ref) in enumerate(streams):
            backward = direction == 1
            r, c, incl, _ = _chunk_masks(backward)
            pos = (n_c - 1 - ci) if backward else ci
            sl = slice(pos * GDN_CHUNK, (pos + 1) * GDN_CHUNK)
            heads = range(heads_per_step)
            dcol = jnp.stack([jnp.sum(jnp.where(lane == direction * head_lanes + hg * heads_per_step + hh,
                                                d_ref[sl, :], 0.0), axis=1, keepdims=True) for hh in heads])
            qc = jnp.stack([q_ref[sl, hh * GDN_DK:(hh + 1) * GDN_DK] for hh in heads])
            kc = jnp.stack([k_ref[sl, hh * GDN_DK:(hh + 1) * GDN_DK] for hh in heads])
            u = jnp.stack([uw_ref[sl, hh * width:hh * width + GDN_DV] for hh in heads])
            w = jnp.stack([uw_ref[sl, hh * width + GDN_DV:(hh + 1) * width] for hh in heads])
            attn = _dot_nt(qc, kc) * _decay_mask(dcol, r, c, incl)
            d_last = dcol[:, 0:1, :] if backward else dcol[:, GDN_CHUNK - 1:GDN_CHUNK, :]
            q_dec = (qc * jnp.exp(dcol)).astype(BF16)
            k_dec = (kc * jnp.exp(d_last - dcol)).astype(BF16)
            s = s_ref[direction]
            s_b = s.astype(BF16)
            v_new = u - bmm(w.astype(BF16), s_b)
            v_b = v_new.astype(BF16)
            o = bmm(q_dec, s_b) + bmm(attn.astype(BF16), v_b)
            for hh in heads:
                o_ref[sl, hh * GDN_DV:(hh + 1) * GDN_DV] = o[hh]
            s_ref[direction] = s * jnp.exp(d_last) + lax.dot_general(
                k_dec, v_b, (((1,), (1,)), ((0,), (0,))), preferred_element_type=F32)

    @pl.when((edge_ref[step] & 2) != 0)
    def _():
        sfin_ref[...] = s_ref[...]


def gdn_scan(qkv, uw, dcs, s0, *, sequences, heads):
    t = qkv.shape[0]
    rows = _pick(math.gcd(*[length for _, length in sequences]), (256, 128, 64))
    hps = _pick(heads, (4, 2, 1))
    hgroups = heads // hps
    width = GDN_DV + GDN_DK
    fwd_blk, bwd_blk, seq_id, edge = [], [], [], []
    for s, (first_row, length) in enumerate(sequences):
        assert first_row % rows == 0 and length % rows == 0
        n_l = length // rows
        for i in range(n_l):
            fwd_blk.append(first_row // rows + i)
            bwd_blk.append(first_row // rows + n_l - 1 - i)
            seq_id.append(s)
            edge.append((1 if i == 0 else 0) | (2 if i == n_l - 1 else 0))
    assert sorted(fwd_blk) == list(range(t // rows))
    tables = [jnp.asarray(np.asarray(v, np.int32)) for v in (fwd_blk, bwd_blk, seq_id, edge)]

    def stream_specs(table, direction):
        return [pl.BlockSpec((rows, hps * GDN_DK), lambda g, s, fb, bb, sq, ed: ((fb, bb)[table][s], g)),
                pl.BlockSpec((rows, hps * GDN_DK), lambda g, s, fb, bb, sq, ed: ((fb, bb)[table][s], hgroups + g)),
                pl.BlockSpec((None, rows, hps * width), lambda g, s, fb, bb, sq, ed: (direction, (fb, bb)[table][s], g)),
                pl.BlockSpec((rows, 2 * heads), lambda g, s, fb, bb, sq, ed: ((fb, bb)[table][s], 0))]

    state_spec = pl.BlockSpec((None, 2, hps, GDN_DK, GDN_DV), lambda g, s, fb, bb, sq, ed: (sq[s], 0, g, 0, 0))
    grid_spec = pltpu.PrefetchScalarGridSpec(
        num_scalar_prefetch=4,
        grid=(hgroups, len(fwd_blk)),
        in_specs=stream_specs(0, 0) + stream_specs(1, 1) + [state_spec],
        out_specs=[pl.BlockSpec((rows, hps * GDN_DV), lambda g, s, fb, bb, sq, ed: (fb[s], g)),
                   pl.BlockSpec((rows, hps * GDN_DV), lambda g, s, fb, bb, sq, ed: (bb[s], g)),
                   state_spec],
        scratch_shapes=[pltpu.VMEM((2, hps, GDN_DK, GDN_DV), F32)],
    )
    return pl.pallas_call(
        functools.partial(_gdn_scan_kernel, heads_per_step=hps, head_lanes=heads),
        grid_spec=grid_spec,
        out_shape=[jax.ShapeDtypeStruct((t, heads * GDN_DV), F32),
                   jax.ShapeDtypeStruct((t, heads * GDN_DV), F32),
                   jax.ShapeDtypeStruct((len(sequences), 2, heads, GDN_DK, GDN_DV), F32)],
        compiler_params=_params("parallel", "arbitrary"),
        name="gdn_scan",
    )(*tables, qkv, qkv, uw, dcs, qkv, qkv, uw, dcs, s0)


def _row_gather(idx_ref, src_hbm, dst_vmem, sem, n_rows):
    def start():
        def body(r, carry):
            pltpu.make_async_copy(src_hbm.at[pl.ds(idx_ref[0, r], 1), :], dst_vmem.at[pl.ds(r, 1), :], sem).start()
            return carry
        lax.fori_loop(0, n_rows, body, 0, unroll=GATHER_UNROLL)

    def wait():
        def body(r, carry):
            pltpu.make_async_copy(src_hbm.at[pl.ds(0, 1), :], dst_vmem.at[pl.ds(0, 1), :], sem).wait()
            return carry
        lax.fori_loop(0, n_rows, body, 0, unroll=GATHER_UNROLL)
    return start, wait


def _moe_kernel(be_ref, nu_ref, nv_ref, tok_ref, tok_next_ref, h_hbm, wg_ref, wu_ref, wd_ref, o_ref,
                xbuf_ref, xlo_ref, xhi_ref, hmid_ref, sem, *, n_f):
    i = pl.program_id(0)
    s = pl.program_id(1)
    n_used = nu_ref[0]
    tm, half = xlo_ref.shape
    hr = tm // 2
    halves = (slice(0, hr), slice(hr, tm))

    @pl.when(i < n_used)
    def _():
        second_half = nv_ref[i] > hr

        @pl.when(s == 0)
        def _():
            for slot in range(2):
                start_this, wait_this = _row_gather(tok_ref, h_hbm, xbuf_ref.at[slot], sem.at[slot], tm)
                start_next, _ = _row_gather(tok_next_ref, h_hbm, xbuf_ref.at[1 - slot], sem.at[1 - slot], tm)

                @pl.when(lax.rem(i, 2) == slot)
                def _():
                    @pl.when(i == 0)
                    def _():
                        start_this()
                    wait_this()

                    @pl.when(i + 1 < n_used)
                    def _():
                        start_next()
                    lo, hi = _unpack_bf16_pair(xbuf_ref[slot])
                    xlo_ref[...] = lo.astype(BF16)
                    xhi_ref[...] = hi.astype(BF16)

        @pl.when(s < n_f)
        def _():
            tf = wg_ref.shape[1]
            w = jnp.concatenate([wg_ref[...].astype(BF16), wu_ref[...].astype(BF16)], axis=1)

            def hidden(rows):
                gu = (jnp.dot(xlo_ref[rows, :], w[:half], preferred_element_type=F32)
                      + jnp.dot(xhi_ref[rows, :], w[half:], preferred_element_type=F32))
                gate, up = gu[:, :tf], gu[:, tf:]
                hmid_ref[s, rows, :] = (gate * jax.nn.sigmoid(gate) * up).astype(BF16)

            hidden(halves[0])

            @pl.when(second_half)
            def _():
                hidden(halves[1])

        @pl.when(s >= n_f)
        def _():
            wd = wd_ref[...].astype(BF16)
            tn = wd.shape[1]

            def project(rows):
                hm = jnp.concatenate([hmid_ref[f, rows, :] for f in range(n_f)], axis=1)
                y = jnp.dot(hm, wd, preferred_element_type=F32)
                o_ref[rows, :] = _pack_bf16_pair(y[:, :tn // 2], y[:, tn // 2:])

            project(halves[0])

            @pl.when(second_half)
            def _():
                project(halves[1])

            @pl.when(jnp.logical_not(second_half))
            def _():
                o_ref[halves[1], :] = jnp.zeros((hr, o_ref.shape[1]), o_ref.dtype)

    @pl.when(jnp.logical_and(i >= n_used, s >= n_f))
    def _():
        o_ref[...] = jnp.zeros_like(o_ref)


def moe_experts(h_packed, slot_tok, block_e, n_used, n_valid, w_gate, w_up, w_down, *, tm, tn, layer):
    d = 2 * h_packed.shape[1]
    p = slot_tok.shape[0]
    f_dim = w_gate.shape[3]
    tf = _pick(f_dim, (128,))
    n_f, n_n = f_dim // tf, d // tn
    n_blocks = p // tm
    tok_blocks = slot_tok.reshape(n_blocks, 1, tm)

    def blk(i, nu):
        return jnp.minimum(i, nu[0] - 1)

    def f_blk(i, s, nu):
        return jnp.where(i < nu[0], jnp.minimum(s, n_f - 1), n_f - 1)

    def n_blk(i, s, nu):
        return jnp.where(i < nu[0], jnp.maximum(s - n_f, 0), n_n - 1)

    grid_spec = pltpu.PrefetchScalarGridSpec(
        num_scalar_prefetch=3,
        grid=(n_blocks, n_f + n_n),
        in_specs=[pl.BlockSpec((None, 1, tm), lambda i, s, be, nu, nv: (blk(i, nu), 0, 0), memory_space=pltpu.SMEM),
                  pl.BlockSpec((None, 1, tm), lambda i, s, be, nu, nv: (blk(i + 1, nu), 0, 0), memory_space=pltpu.SMEM),
                  pl.BlockSpec(memory_space=pl.ANY),
                  pl.BlockSpec((None, None, d, tf), lambda i, s, be, nu, nv: (layer, be[blk(i, nu)], 0, f_blk(i, s, nu))),
                  pl.BlockSpec((None, None, d, tf), lambda i, s, be, nu, nv: (layer, be[blk(i, nu)], 0, f_blk(i, s, nu))),
                  pl.BlockSpec((None, None, f_dim, tn), lambda i, s, be, nu, nv: (layer, be[blk(i, nu)], 0, n_blk(i, s, nu)))],
        out_specs=pl.BlockSpec((tm, tn // 2), lambda i, s, be, nu, nv: (i, jnp.maximum(s - n_f, 0))),
        scratch_shapes=[pltpu.VMEM((2, tm, d // 2), jnp.uint32), pltpu.VMEM((tm, d // 2), BF16),
                        pltpu.VMEM((tm, d // 2), BF16), pltpu.VMEM((n_f, tm, tf), BF16),
                        pltpu.SemaphoreType.DMA((2,))],
    )
    return pl.pallas_call(
        functools.partial(_moe_kernel, n_f=n_f),
        grid_spec=grid_spec,
        out_shape=jax.ShapeDtypeStruct((p, d // 2), jnp.uint32),
        compiler_params=_params("arbitrary", "arbitrary"),
        name="moe_experts",
    )(block_e, n_used, n_valid, tok_blocks, tok_blocks, h_packed, w_gate, w_up, w_down)


def _moe_combine_kernel(dest_ref, y_hbm, x_ref, gate_ref, mod_ref, o_ref, rows_ref, sem, *, gate_idx, tn):
    tc, d = x_ref.shape
    start, wait = _row_gather(dest_ref, y_hbm, rows_ref, sem, 2 * tc)
    start()
    wait()
    g0 = gate_ref[:, 0:1]
    g1 = gate_ref[:, 1:2]
    for n in range(d // tn):
        words = slice(n * tn // 2, (n + 1) * tn // 2)
        lo0, hi0 = _unpack_bf16_pair(rows_ref[:tc, words])
        lo1, hi1 = _unpack_bf16_pair(rows_ref[tc:, words])
        for part, y in enumerate((lo0 * g0 + lo1 * g1, hi0 * g0 + hi1 * g1)):
            cs = slice(n * tn + part * tn // 2, n * tn + (part + 1) * tn // 2)
            o_ref[:, cs] = x_ref[:, cs] + mod_ref[gate_idx:gate_idx + 1, cs] * y


def moe_combine_residual(y_packed, dest, gates, x, mod, *, gate_idx, tc, tn, row_fn):
    t, d = x.shape
    n_tiles = t // tc
    dest_tiles = dest.reshape(n_tiles, tc, TOP_K).transpose(0, 2, 1).reshape(n_tiles, 1, TOP_K * tc)
    return pl.pallas_call(
        functools.partial(_moe_combine_kernel, gate_idx=gate_idx, tn=tn),
        grid=(n_tiles,),
        in_specs=[pl.BlockSpec((None, 1, TOP_K * tc), lambda i: (i, 0, 0), memory_space=pltpu.SMEM),
                  pl.BlockSpec(memory_space=pl.ANY),
                  pl.BlockSpec((tc, d), lambda i: (i, 0)),
                  pl.BlockSpec((tc, gates.shape[1]), lambda i: (i, 0)),
                  pl.BlockSpec((None, mod.shape[1], d), lambda i: (row_fn(i), 0, 0))],
        out_specs=pl.BlockSpec((tc, d), lambda i: (i, 0)),
        out_shape=jax.ShapeDtypeStruct((t, d), F32),
        scratch_shapes=[pltpu.VMEM((TOP_K * tc, d // 2), jnp.uint32), pltpu.SemaphoreType.DMA(())],
        compiler_params=_params("arbitrary"),
        name="moe_combine",
    )(dest_tiles, y_packed, x, gates, mod)


def _rms(x, g):
    return x * lax.rsqrt(jnp.mean(x * x, axis=-1, keepdims=True) + EPS) * g


def _rope_tables(t_prompt, dec_batch, dec_seq, width):
    a = width // 2
    half = a // 2
    freqs = ROPE_THETA ** (-np.arange(half, dtype=np.float32) / half)
    t = np.arange(dec_seq)
    row, col = t // GRID_W, t % GRID_W
    ang_r = row[:, None].astype(np.float32) * freqs[None, :]
    ang_c = col[:, None].astype(np.float32) * freqs[None, :]
    cos = np.concatenate([np.cos(ang_r), np.cos(ang_r), np.cos(ang_c), np.cos(ang_c)], axis=-1)
    sin = np.concatenate([-np.sin(ang_r), np.sin(ang_r), -np.sin(ang_c), np.sin(ang_c)], axis=-1)
    cos = np.concatenate([np.ones((t_prompt, width), np.float32), np.tile(cos, (dec_batch, 1))], axis=0)
    sin = np.concatenate([np.zeros((t_prompt, width), np.float32), np.tile(sin, (dec_batch, 1))], axis=0)
    return jnp.asarray(cos), jnp.asarray(sin)


def _gdn_gates(a, b, a_log, dt_bias, heads):
    t = a.shape[0]
    g = -jnp.exp(a_log.reshape(2 * heads)) * jax.nn.softplus(a + dt_bias.reshape(2 * heads))
    gc = g.reshape(t // GDN_CHUNK, GDN_CHUNK, 2 * heads)
    prefix = jnp.cumsum(gc[..., :heads], axis=1)
    suffix = jnp.flip(jnp.cumsum(jnp.flip(gc[..., heads:], axis=1), axis=1), axis=1)
    dcs = jnp.concatenate([prefix, suffix], axis=-1).reshape(t, 2 * heads)
    return dcs, jax.nn.sigmoid(b)


def _dispatch_plan(experts, n_experts, tm):
    t, k = experts.shape
    a = t * k
    n_blocks = a // tm + n_experts
    flat_e = experts.reshape(-1)
    onehot = (flat_e[:, None] == jnp.arange(n_experts, dtype=jnp.int32)[None, :]).astype(jnp.int32)
    csum = jnp.cumsum(onehot, axis=0)
    pos = jnp.sum((csum - onehot) * onehot, axis=1)
    counts = csum[-1]
    padded = (counts + tm - 1) // tm * tm
    pend = jnp.cumsum(padded)
    pstart = pend - padded
    dest = pstart[flat_e] + pos
    flat_tok = jnp.arange(a, dtype=jnp.int32) // k
    slot_tok = jnp.zeros((n_blocks * tm,), jnp.int32).at[dest].set(flat_tok)
    block_start = jnp.arange(n_blocks, dtype=jnp.int32) * tm
    block_e = jnp.minimum(jnp.searchsorted(pend, block_start, side='right'), n_experts - 1).astype(jnp.int32)
    n_used = (pend[-1] // tm).astype(jnp.int32).reshape(1)
    n_valid = jnp.clip(pstart[block_e] + counts[block_e] - block_start, 0, tm).astype(jnp.int32)
    return dest.reshape(t, k), slot_tok, block_e, n_used, n_valid


def kernel(x_prompt, x_sample, cache_mla_ckv, cache_mla_krope, cache_gqa_k, cache_gqa_v, state_gdn, c, c_ctx, w_mod, b_mod, g_norm_mix, g_norm_ffn, w_in, g_cq, w_q_up, g_ckv, w_kv_up, g_q_mla, g_k_mla, g_q_gqa, g_k_gqa, w_conv, a_log, dt_bias, g_gdn_out, w_out, w_router_group, b_router_group, w_router_expert, b_router_expert, w_gate, w_up, w_down):
    batch, seq, d = x_prompt.shape
    dec_batch, dec_seq, _ = x_sample.shape
    depth = w_in.shape[0]
    past = cache_mla_ckv.shape[2]
    q_lora = w_q_up.shape[1]
    kv_lora = w_kv_up.shape[1]
    mla_heads = w_q_up.shape[2] // MLA_QK
    kv_heads = cache_gqa_k.shape[3]
    gdn_heads = state_gdn.shape[3]
    gqa_heads = (w_out.shape[1] - mla_heads * MLA_V - gdn_heads * GDN_DV) // HEAD_DIM
    group = gqa_heads // kv_heads
    n_groups = w_router_group.shape[2]
    n_experts = w_gate.shape[1]
    gdn_qkv = gdn_heads * (2 * GDN_DK + GDN_DV)
    gdn_width = gdn_heads * GDN_DV

    t_p = batch * seq
    t_s = dec_batch * dec_seq
    t_all = t_p + t_s
    tm = _pick(math.gcd(t_p, dec_seq), (512, 256, 128))
    row_fn = _mod_row_fn(tm, t_p, dec_seq)
    moe_tm = _pick(t_all * TOP_K, (1024, 512, 256, 128))
    moe_tn = _pick(d, (1024, 512, 256))
    combine_tc = _pick(math.gcd(t_p, dec_seq), (256, 128))

    src_sizes = (q_lora, kv_lora, MLA_ROPE, gqa_heads * HEAD_DIM, kv_heads * HEAD_DIM, kv_heads * HEAD_DIM,
                 gdn_qkv, gdn_width, 2 * gdn_heads, 2 * gdn_heads)
    src_off = np.concatenate([[0], np.cumsum(src_sizes)])
    src = dict(zip(('cq', 'ckv', 'krope', 'gq', 'gk', 'gv', 'qkv', 'z', 'a', 'b'),
                   [(int(src_off[j]), int(src_off[j + 1])) for j in range(len(src_sizes))]))
    tail_w = -(-(MLA_ROPE + 4 * gdn_heads) // LANE) * LANE
    cols, pieces, pos = {}, [], 0
    for name, width in (('qkv', gdn_qkv), ('z', gdn_width), ('ckv', kv_lora), ('gq', gqa_heads * HEAD_DIM),
                        ('cq', q_lora), ('tail', tail_w), ('gk', kv_heads * HEAD_DIM), ('gv', kv_heads * HEAD_DIM)):
        start = -(-pos // width) * width
        assert width % LANE == 0
        if start > pos:
            pieces.append(('pad', start - pos))
        pieces.append((name, width))
        cols[name] = (start, width)
        pos = start + width
    n_in = pos
    tail0 = cols['tail'][0]

    def build_w_in(w):
        parts = []
        for name, width in pieces:
            if name == 'pad':
                parts.append(jnp.zeros((d, width), F32))
            elif name == 'tail':
                parts += [w[:, src[s][0]:src[s][1]] for s in ('krope', 'a', 'b')]
                parts.append(jnp.zeros((d, width - MLA_ROPE - 4 * gdn_heads), F32))
            else:
                parts.append(w[:, src[name][0]:src[name][1]])
        return jnp.concatenate(parts, axis=1).astype(BF16)

    def padded_rope_tables(width):
        cos, sin = _rope_tables(t_p, dec_batch, dec_seq, width)
        return (jnp.pad(cos, ((0, 0), (0, LANE - width)), constant_values=1.0),
                jnp.pad(sin, ((0, 0), (0, LANE - width))))

    cos_g, sin_g = padded_rope_tables(HEAD_DIM)
    cos_m, sin_m = padded_rope_tables(MLA_ROPE)
    n_ctx = dec_batch * past
    cos_id, sin_id = jnp.ones((n_ctx, LANE), F32), jnp.zeros((n_ctx, LANE), F32)
    sequences = ([(b * seq, seq) for b in range(batch)]
                 + [(t_p + b * dec_seq, dec_seq) for b in range(dec_batch)])

    cond8 = jnp.zeros((8, d), F32).at[0].set(c_ctx).at[1:1 + dec_batch].set(c)

    x = jnp.concatenate([x_prompt.reshape(t_p, d), x_sample.reshape(t_s, d)], axis=0)
    new_ckv, new_krope, new_k, new_v, new_s = [], [], [], [], []
    q_scale_m = MLA_QK ** -0.5 * LOG2E
    q_scale_g = HEAD_DIM ** -0.5 * LOG2E

    for l in range(depth):
        mod = modulation(cond8, w_mod, b_mod, l).reshape(8, 6, d)

        u = norm_matmul(x, g_norm_mix[l], mod, build_w_in(w_in[l]), sh_idx=0, sc_idx=1, tm=tm, row_fn=row_fn)

        wq = w_q_up[l].reshape(q_lora, mla_heads, MLA_QK)
        wq = jnp.pad(wq, ((0, 0), (0, 0), (0, MLA_PAD - MLA_QK))).reshape(q_lora, mla_heads * MLA_PAD).astype(BF16)
        q_m = mla_queries(u, cols, g_cq[l], wq, jnp.pad(g_q_mla[l], (0, MLA_PAD - MLA_QK)), cos_m, sin_m,
                          tm=tm, q_scale=q_scale_m)
        wkv = w_kv_up[l].reshape(kv_lora, mla_heads, MLA_NOPE + MLA_V)
        wkv = jnp.concatenate([wkv[..., :MLA_NOPE].reshape(kv_lora, -1), wkv[..., MLA_NOPE:].reshape(kv_lora, -1)],
                              axis=1).astype(BF16)
        g_k_pad = jnp.pad(g_k_mla[l], (0, MLA_PAD - MLA_QK))
        ckv_n, k_m, v_m = mla_keys_values(u, cols['ckv'], u, cols['tail'], g_ckv[l], wkv, g_k_pad, cos_m, sin_m,
                                          heads=mla_heads, tm=tm, normalise=True)
        ctx_krope = jnp.pad(cache_mla_krope[:, l].reshape(n_ctx, MLA_ROPE), ((0, 0), (0, LANE - MLA_ROPE)))
        _, k_m_ctx, v_m_ctx = mla_keys_values(cache_mla_ckv[:, l].reshape(n_ctx, kv_lora), (0, kv_lora), ctx_krope,
                                              (0, LANE), g_ckv[l], wkv, g_k_pad, cos_id, sin_id, heads=mla_heads,
                                              tm=_pick(n_ctx, (512, 256, 128)), normalise=False)

        mla_common = dict(kv_heads=mla_heads, group=1, dq=MLA_PAD, dv=MLA_V, out_rows=t_all)
        o_m = attention(q_m, 0, [(k_m, v_m, 0, seq)], batch=batch, lq=seq, out_row0=0,
                        prev_out=jnp.zeros((t_all, mla_heads * MLA_V), BF16), name="mla_attn_ctx", **mla_common)
        o_m = attention(q_m, t_p, [(k_m_ctx, v_m_ctx, 0, past), (k_m, v_m, t_p, dec_seq)], batch=dec_batch,
                        lq=dec_seq, out_row0=t_p, prev_out=o_m, name="mla_attn_lat", **mla_common)

        q_g, k_g_rot, k_g, v_g_b = gqa_prep(u, cols, g_q_gqa[l], g_k_gqa[l], cos_g, sin_g, tm=tm, q_scale=q_scale_g)
        ctx_k = cache_gqa_k[:, l].reshape(n_ctx, kv_heads * HEAD_DIM).astype(BF16)
        ctx_v = cache_gqa_v[:, l].reshape(n_ctx, kv_heads * HEAD_DIM).astype(BF16)
        gqa_common = dict(kv_heads=kv_heads, group=group, dq=HEAD_DIM, dv=HEAD_DIM, out_rows=t_all)
        o_g = attention(q_g, 0, [(k_g_rot, v_g_b, 0, seq)], batch=batch, lq=seq, out_row0=0,
                        prev_out=jnp.zeros((t_all, gqa_heads * HEAD_DIM), BF16), name="gqa_attn_ctx", **gqa_common)
        o_g = attention(q_g, t_p, [(ctx_k, ctx_v, 0, past), (k_g_rot, v_g_b, t_p, dec_seq)], batch=dec_batch,
                        lq=dec_seq, out_row0=t_p, prev_out=o_g, name="gqa_attn_lat", **gqa_common)

        qkv_c = gdn_prep(u, cols['qkv'], w_conv[l], heads=gdn_heads, tm=tm, t_prompt=t_p, seq_lens=(seq, dec_seq))
        tail = u[:, tail0:tail0 + MLA_ROPE + 4 * gdn_heads]
        dcs, beta = _gdn_gates(tail[:, MLA_ROPE:MLA_ROPE + 2 * gdn_heads], tail[:, MLA_ROPE + 2 * gdn_heads:],
                               a_log[l], dt_bias[l], gdn_heads)
        uw = gdn_solve(qkv_c, dcs, beta, heads=gdn_heads)
        s0 = jnp.concatenate([jnp.zeros((batch, 2, gdn_heads, GDN_DK, GDN_DV), F32), state_gdn[:, l]], axis=0)
        o_fwd, o_bwd, s_fin = gdn_scan(qkv_c, uw, dcs, s0, sequences=sequences, heads=gdn_heads)

        new_ckv.append(ckv_n[:t_p].reshape(batch, seq, kv_lora))
        new_krope.append(tail[:t_p, :MLA_ROPE].reshape(batch, seq, MLA_ROPE))
        new_k.append(k_g[:t_p].reshape(batch, seq, kv_heads, HEAD_DIM))
        new_v.append(u[:t_p, cols['gv'][0]:cols['gv'][0] + cols['gv'][1]].reshape(batch, seq, kv_heads, HEAD_DIM))
        new_s.append(s_fin[:batch])

        x = out_proj_residual(o_m, o_g, o_fwd, o_bwd, u, cols['z'], g_gdn_out[l], w_out[l].astype(BF16), x, mod,
                              gate_idx=2, tm=tm, row_fn=row_fn)

        w_router = jnp.concatenate([w_router_group[l], w_router_expert[l],
                                    jnp.zeros((d, LANE - n_groups - n_experts), F32)], axis=1)
        b_router = jnp.concatenate([b_router_group[l], b_router_expert[l],
                                    jnp.zeros((LANE - n_groups - n_experts,), F32)]).reshape(1, LANE)
        h2, gates, experts = ffn_norm_router(x, g_norm_ffn[l], mod, w_router, b_router, sh_idx=3, sc_idx=4, tm=tm,
                                             row_fn=row_fn, n_groups=n_groups, epg=n_experts // n_groups)
        dest, slot_tok, block_e, n_used, n_valid = _dispatch_plan(experts[:, :TOP_K], n_experts, moe_tm)
        yb = moe_experts(h2, slot_tok, block_e, n_used, n_valid, w_gate, w_up, w_down, tm=moe_tm, tn=moe_tn, layer=l)
        x = moe_combine_residual(yb, dest, gates, x, mod, gate_idx=5, tc=combine_tc, tn=moe_tn,
                                 row_fn=_mod_row_fn(combine_tc, t_p, dec_seq))

    return (x[:t_p].reshape(batch, seq, d), x[t_p:].reshape(dec_batch, dec_seq, d),
            jnp.stack(new_ckv, axis=1), jnp.stack(new_krope, axis=1), jnp.stack(new_k, axis=1),
            jnp.stack(new_v, axis=1), jnp.stack(new_s, axis=1))
```

```python
import functools
import math

import numpy as np
import jax
import jax.numpy as jnp
from jax import lax
from jax.experimental import pallas as pl
from jax.experimental.pallas import tpu as pltpu

F32 = jnp.float32
BF16 = jnp.bfloat16

EPS = 1e-6
ROPE_THETA = 10000.0
GRID_W = 64
HEAD_DIM = 128
MLA_NOPE = 128
MLA_ROPE = 64
MLA_V = 128
MLA_QK = MLA_NOPE + MLA_ROPE
MLA_PAD = 256
GDN_DK = 128
GDN_DV = 128
GDN_CHUNK = 64
TOP_K = 2
LANE = 128
VMEM_LIMIT_BYTES = 56 * 1024 * 1024
LOG2E = 1.4426950408889634
ATTN_UNROLL = 4
GATHER_UNROLL = 8


def _params(*sem):
    return pltpu.CompilerParams(dimension_semantics=sem, vmem_limit_bytes=VMEM_LIMIT_BYTES)


def _pick(n, prefs):
    for p in prefs:
        if n % p == 0:
            return p
    return n


def _mod_kernel(c_ref, w_ref, b_ref, o_ref):
    c = c_ref[...]
    s = (c * jax.nn.sigmoid(c)).astype(BF16)
    o_ref[...] = jnp.dot(s, w_ref[...].astype(BF16), preferred_element_type=F32) + b_ref[...]


def modulation(cond8, w_mod, b_mod, layer):
    rows, d = cond8.shape
    depth, _, n = w_mod.shape
    tn = _pick(n, (512, 256, 128))
    return pl.pallas_call(
        _mod_kernel,
        grid=(n // tn,),
        in_specs=[pl.BlockSpec((rows, d), lambda j: (0, 0)),
                  pl.BlockSpec((None, d, tn), lambda j: (layer, 0, j)),
                  pl.BlockSpec((None, 1, tn), lambda j: (layer, 0, j))],
        out_specs=pl.BlockSpec((rows, tn), lambda j: (0, j)),
        out_shape=jax.ShapeDtypeStruct((rows, n), F32),
        compiler_params=_params("parallel"),
        name="modulation",
    )(cond8, w_mod, b_mod.reshape(depth, 1, n))


def _mod_row_fn(tm, t_prompt, dec_seq):
    n_p = t_prompt // tm
    per_b = dec_seq // tm

    def row(i):
        return jnp.where(i < n_p, 0, 1 + (i - n_p) // per_b)
    return row


def _modulated_norm(x, g, sc, sh):
    ms = jnp.mean(x * x, axis=-1, keepdims=True)
    return (x * lax.rsqrt(ms + EPS) * g) * (1.0 + sc) + sh


def _norm_matmul_kernel(x_ref, g_ref, mod_ref, w_ref, o_ref, h_ref, *, sh_idx, sc_idx):
    @pl.when(pl.program_id(1) == 0)
    def _():
        h = _modulated_norm(x_ref[...], g_ref[...], mod_ref[sc_idx:sc_idx + 1, :],
                            mod_ref[sh_idx:sh_idx + 1, :])
        h_ref[...] = h.astype(BF16)
    o_ref[...] = jnp.dot(h_ref[...], w_ref[...], preferred_element_type=F32).astype(o_ref.dtype)


def norm_matmul(x, g, mod, w, *, sh_idx, sc_idx, tm, row_fn, out_dtype=F32):
    t, d = x.shape
    n = w.shape[1]
    tn = _pick(n, (1152, 1024, 512, 384, 256, 128))
    return pl.pallas_call(
        functools.partial(_norm_matmul_kernel, sh_idx=sh_idx, sc_idx=sc_idx),
        grid=(t // tm, n // tn),
        in_specs=[pl.BlockSpec((tm, d), lambda i, j: (i, 0)),
                  pl.BlockSpec((1, d), lambda i, j: (0, 0)),
                  pl.BlockSpec((None, mod.shape[1], d), lambda i, j: (row_fn(i), 0, 0)),
                  pl.BlockSpec((d, tn), lambda i, j: (0, j))],
        out_specs=pl.BlockSpec((tm, tn), lambda i, j: (i, j)),
        out_shape=jax.ShapeDtypeStruct((t, n), out_dtype),
        scratch_shapes=[pltpu.VMEM((tm, d), BF16)],
        compiler_params=_params("parallel", "arbitrary"),
        name="norm_in_proj",
    )(x, g.reshape(1, d), mod, w)


def _out_proj_kernel(om_ref, og_ref, of_ref, ob_ref, z_ref, gd_ref, wm_ref, wg_ref, wd_ref, x_ref, mod_ref,
                     o_ref, od_ref, *, gate_idx):
    @pl.when(pl.program_id(1) == 0)
    def _():
        for h in range(of_ref.shape[1] // GDN_DV):
            hl = slice(h * GDN_DV, (h + 1) * GDN_DV)
            z = z_ref[:, hl]
            od_ref[:, hl] = (_head_rms(of_ref[:, hl] + ob_ref[:, hl], gd_ref[...])
                             * (z * jax.nn.sigmoid(z))).astype(BF16)
    acc = jnp.dot(om_ref[...], wm_ref[...], preferred_element_type=F32)
    acc += jnp.dot(og_ref[...], wg_ref[...], preferred_element_type=F32)
    acc += jnp.dot(od_ref[...], wd_ref[...], preferred_element_type=F32)
    o_ref[...] = x_ref[...] + mod_ref[gate_idx:gate_idx + 1, :] * acc


def out_proj_residual(o_m, o_g, o_fwd, o_bwd, u, z_col, g_gdn, w_out, x, mod, *, gate_idx, tm, row_fn):
    t, d = x.shape
    km, kg, kd = o_m.shape[1], o_g.shape[1], o_fwd.shape[1]
    z0, zw = z_col
    assert km == kg and (km + kg) % kd == 0 and zw == kd
    tn = _pick(d, (1024, 512, 256, 128))
    return pl.pallas_call(
        functools.partial(_out_proj_kernel, gate_idx=gate_idx),
        grid=(t // tm, d // tn),
        in_specs=[pl.BlockSpec((tm, km), lambda i, j: (i, 0)),
                  pl.BlockSpec((tm, kg), lambda i, j: (i, 0)),
                  pl.BlockSpec((tm, kd), lambda i, j: (i, 0)),
                  pl.BlockSpec((tm, kd), lambda i, j: (i, 0)),
                  pl.BlockSpec((tm, zw), lambda i, j: (i, z0 // zw)),
                  pl.BlockSpec((1, GDN_DV), lambda i, j: (0, 0)),
                  pl.BlockSpec((km, tn), lambda i, j: (0, j)),
                  pl.BlockSpec((kg, tn), lambda i, j: (1, j)),
                  pl.BlockSpec((kd, tn), lambda i, j: ((km + kg) // kd, j)),
                  pl.BlockSpec((tm, tn), lambda i, j: (i, j)),
                  pl.BlockSpec((None, mod.shape[1], tn), lambda i, j: (row_fn(i), 0, j))],
        out_specs=pl.BlockSpec((tm, tn), lambda i, j: (i, j)),
        out_shape=jax.ShapeDtypeStruct((t, d), F32),
        scratch_shapes=[pltpu.VMEM((tm, kd), BF16)],
        compiler_params=_params("parallel", "arbitrary"),
        name="out_proj",
    )(o_m, o_g, o_fwd, o_bwd, u, g_gdn.reshape(1, GDN_DV), w_out, w_out, w_out, x, mod)


def _pack_bf16_pair(lo, hi):
    lo_bits = lax.bitcast_convert_type(lo.astype(BF16).astype(F32), jnp.uint32)
    hi_bits = lax.bitcast_convert_type(hi.astype(BF16).astype(F32), jnp.uint32)
    return jnp.right_shift(lo_bits, jnp.uint32(16)) | (hi_bits & jnp.uint32(0xFFFF0000))


def _unpack_bf16_pair(word):
    lo = lax.bitcast_convert_type(jnp.left_shift(word, jnp.uint32(16)), F32)
    hi = lax.bitcast_convert_type(word & jnp.uint32(0xFFFF0000), F32)
    return lo, hi


def _ffn_norm_router_kernel(x_ref, g_ref, mod_ref, wr_ref, br_ref, h_ref, gate_ref, exp_ref, *,
                            sh_idx, sc_idx, n_groups, epg):
    h = _modulated_norm(x_ref[...], g_ref[...], mod_ref[sc_idx:sc_idx + 1, :],
                        mod_ref[sh_idx:sh_idx + 1, :])
    half = h.shape[1] // 2
    h_ref[...] = _pack_bf16_pair(h[:, :half], h[:, half:])
    h_hi = h.astype(BF16)
    h_lo = (h - h_hi.astype(F32)).astype(BF16)
    w = wr_ref[...]
    w_hi = w.astype(BF16)
    w_lo = (w - w_hi.astype(F32)).astype(BF16)
    lg = (jnp.dot(h_hi, w_hi, preferred_element_type=F32) + jnp.dot(h_lo, w_hi, preferred_element_type=F32)
          + jnp.dot(h_hi, w_lo, preferred_element_type=F32)) + br_ref[...]
    lane = lax.broadcasted_iota(jnp.int32, lg.shape, 1).astype(F32)
    neg = jnp.float32(-1e30)
    far = jnp.float32(2 * LANE)

    def first_argmax(v):
        top = jnp.max(v, axis=-1, keepdims=True)
        return top, jnp.min(jnp.where(v == top, lane, far), axis=-1, keepdims=True)

    is_group = lane < n_groups
    g_max, g_idx = first_argmax(jnp.where(is_group, lg, neg))
    g_top = 1.0 / jnp.sum(jnp.where(is_group, jnp.exp(lg - g_max), 0.0), axis=-1, keepdims=True)
    lo = n_groups + g_idx * epg
    el = jnp.where(jnp.logical_and(lane >= lo, lane < lo + epg), lg, neg)
    e1, i1 = first_argmax(el)
    e2, i2 = first_argmax(jnp.where(lane == i1, neg, el))
    r = jnp.exp(e2 - e1)
    gate1 = g_top / (1.0 + r)
    gate2 = g_top * r / (1.0 + r)
    gate_ref[...] = jnp.where(lane == 0, gate1, jnp.where(lane == 1, gate2, 0.0))
    exp_ref[...] = jnp.where(lane == 0, i1 - n_groups, jnp.where(lane == 1, i2 - n_groups, 0.0)).astype(jnp.int32)


def ffn_norm_router(x, g, mod, w_router, b_router, *, sh_idx, sc_idx, tm, row_fn, n_groups, epg):
    t, d = x.shape
    nr = w_router.shape[1]
    assert d % (2 * LANE) == 0
    return pl.pallas_call(
        functools.partial(_ffn_norm_router_kernel, sh_idx=sh_idx, sc_idx=sc_idx, n_groups=n_groups, epg=epg),
        grid=(t // tm,),
        in_specs=[pl.BlockSpec((tm, d), lambda i: (i, 0)),
                  pl.BlockSpec((1, d), lambda i: (0, 0)),
                  pl.BlockSpec((None, mod.shape[1], d), lambda i: (row_fn(i), 0, 0)),
                  pl.BlockSpec((d, nr), lambda i: (0, 0)),
                  pl.BlockSpec((1, nr), lambda i: (0, 0))],
        out_specs=[pl.BlockSpec((tm, d // 2), lambda i: (i, 0)),
                   pl.BlockSpec((tm, nr), lambda i: (i, 0)),
                   pl.BlockSpec((tm, nr), lambda i: (i, 0))],
        out_shape=[jax.ShapeDtypeStruct((t, d // 2), jnp.uint32), jax.ShapeDtypeStruct((t, nr), F32),
                   jax.ShapeDtypeStruct((t, nr), jnp.int32)],
        compiler_params=_params("parallel"),
        name="ffn_norm_router",
    )(x, g.reshape(1, d), mod, w_router, b_router)


def _rope_rotate(x, cos, sin, pair):
    lane = lax.broadcasted_iota(jnp.int32, x.shape, 1)
    first = (lane & (2 * pair - 1)) < pair
    partner = jnp.where(first, pltpu.roll(x, LANE - pair, axis=1), pltpu.roll(x, pair, axis=1))
    return x * cos + partner * sin


def _head_rms(x, g):
    return x * lax.rsqrt(jnp.mean(x * x, axis=-1, keepdims=True) + EPS) * g


def _gqa_prep_kernel(gq_ref, gk_ref, gv_ref, gq_gain_ref, gk_gain_ref, cos_ref, sin_ref,
                     q_ref, k_rot_ref, k_ref, v_ref, *, q_scale):
    cos, sin = cos_ref[...], sin_ref[...]
    pair = HEAD_DIM // 4
    for h in range(gq_ref.shape[1] // HEAD_DIM):
        hl = slice(h * HEAD_DIM, (h + 1) * HEAD_DIM)
        q = _rope_rotate(_head_rms(gq_ref[:, hl], gq_gain_ref[...]), cos, sin, pair)
        q_ref[:, hl] = (q * q_scale).astype(BF16)
    for h in range(gk_ref.shape[1] // HEAD_DIM):
        hl = slice(h * HEAD_DIM, (h + 1) * HEAD_DIM)
        k = _head_rms(gk_ref[:, hl], gk_gain_ref[...])
        k_ref[:, hl] = k
        k_rot_ref[:, hl] = _rope_rotate(k, cos, sin, pair).astype(BF16)
    v_ref[...] = gv_ref[...].astype(BF16)


def gqa_prep(u, cols, g_q, g_k, cos, sin, *, tm, q_scale):
    t = u.shape[0]
    (q0, qw), (k0, kw), (v0, vw) = cols['gq'], cols['gk'], cols['gv']
    row = lambda i: (i, 0)
    return pl.pallas_call(
        functools.partial(_gqa_prep_kernel, q_scale=q_scale),
        grid=(t // tm,),
        in_specs=[pl.BlockSpec((tm, qw), lambda i: (i, q0 // qw)),
                  pl.BlockSpec((tm, kw), lambda i: (i, k0 // kw)),
                  pl.BlockSpec((tm, vw), lambda i: (i, v0 // vw)),
                  pl.BlockSpec((1, HEAD_DIM), lambda i: (0, 0)),
                  pl.BlockSpec((1, HEAD_DIM), lambda i: (0, 0)),
                  pl.BlockSpec((tm, LANE), row),
                  pl.BlockSpec((tm, LANE), row)],
        out_specs=[pl.BlockSpec((tm, qw), row), pl.BlockSpec((tm, kw), row),
                   pl.BlockSpec((tm, kw), row), pl.BlockSpec((tm, vw), row)],
        out_shape=[jax.ShapeDtypeStruct((t, qw), BF16), jax.ShapeDtypeStruct((t, kw), BF16),
                   jax.ShapeDtypeStruct((t, kw), F32), jax.ShapeDtypeStruct((t, vw), BF16)],
        compiler_params=_params("parallel"),
        name="gqa_prep",
    )(u, u, u, g_q.reshape(1, HEAD_DIM), g_k.reshape(1, HEAD_DIM), cos, sin)


def _mla_q_kernel(cq_ref, g_cq_ref, wq_ref, g_q_ref, cos_ref, sin_ref, q_ref, *, q_scale):
    cqn = _head_rms(cq_ref[...], g_cq_ref[...]).astype(BF16)
    q = jnp.dot(cqn, wq_ref[...], preferred_element_type=F32)
    cos, sin = cos_ref[...], sin_ref[...]
    g_nope, g_rope = g_q_ref[:, :MLA_NOPE], g_q_ref[:, MLA_NOPE:]
    for h in range(q.shape[1] // MLA_PAD):
        nope = q[:, h * MLA_PAD:h * MLA_PAD + MLA_NOPE]
        rope = q[:, h * MLA_PAD + MLA_NOPE:(h + 1) * MLA_PAD]
        ss = jnp.sum(nope * nope, axis=-1, keepdims=True) + jnp.sum(rope * rope, axis=-1, keepdims=True)
        r = lax.rsqrt(ss / MLA_QK + EPS) * q_scale
        q_ref[:, h * MLA_PAD:h * MLA_PAD + MLA_NOPE] = (nope * r * g_nope).astype(BF16)
        q_ref[:, h * MLA_PAD + MLA_NOPE:(h + 1) * MLA_PAD] = (
            _rope_rotate(rope * g_rope, cos, sin, MLA_ROPE // 4) * r).astype(BF16)


def mla_queries(u, cols, g_cq, wq, g_q_pad, cos, sin, *, tm, q_scale):
    t = u.shape[0]
    c0, cw = cols['cq']
    n = wq.shape[1]
    return pl.pallas_call(
        functools.partial(_mla_q_kernel, q_scale=q_scale),
        grid=(t // tm,),
        in_specs=[pl.BlockSpec((tm, cw), lambda i: (i, c0 // cw)),
                  pl.BlockSpec((1, cw), lambda i: (0, 0)),
                  pl.BlockSpec((cw, n), lambda i: (0, 0)),
                  pl.BlockSpec((1, MLA_PAD), lambda i: (0, 0)),
                  pl.BlockSpec((tm, LANE), lambda i: (i, 0)),
                  pl.BlockSpec((tm, LANE), lambda i: (i, 0))],
        out_specs=pl.BlockSpec((tm, n), lambda i: (i, 0)),
        out_shape=jax.ShapeDtypeStruct((t, n), BF16),
        compiler_params=_params("parallel"),
        name="mla_queries",
    )(u, g_cq.reshape(1, cw), wq, g_q_pad.reshape(1, MLA_PAD), cos, sin)


def _mla_kv_kernel(ckv_ref, kr_ref, g_ckv_ref, wkv_ref, g_k_ref, cos_ref, sin_ref, ckvn_ref, k_ref, v_ref, *,
                   heads, normalise):
    ckv = ckv_ref[...]
    if normalise:
        ckv = _head_rms(ckv, g_ckv_ref[...])
    ckvn_ref[...] = ckv
    kv = jnp.dot(ckv.astype(BF16), wkv_ref[...], preferred_element_type=F32)
    lane = lax.broadcasted_iota(jnp.int32, (ckv.shape[0], LANE), 1)
    krope = jnp.where(lane < MLA_ROPE, kr_ref[:, :LANE], 0.0)
    ss_rope = jnp.sum(krope * krope, axis=-1, keepdims=True)
    g_nope, g_rope = g_k_ref[:, :MLA_NOPE], g_k_ref[:, MLA_NOPE:]
    rope = _rope_rotate(krope * g_rope, cos_ref[...], sin_ref[...], MLA_ROPE // 4)
    for h in range(heads):
        nope = kv[:, h * MLA_NOPE:(h + 1) * MLA_NOPE]
        r = lax.rsqrt((jnp.sum(nope * nope, axis=-1, keepdims=True) + ss_rope) / MLA_QK + EPS)
        k_ref[:, h * MLA_PAD:h * MLA_PAD + MLA_NOPE] = (nope * r * g_nope).astype(BF16)
        k_ref[:, h * MLA_PAD + MLA_NOPE:(h + 1) * MLA_PAD] = (rope * r).astype(BF16)
    v_ref[...] = kv[:, heads * MLA_NOPE:].astype(BF16)


def mla_keys_values(ckv_src, ckv_col, kr_src, kr_col, g_ckv, wkv, g_k_pad, cos, sin, *, heads, tm, normalise):
    t = ckv_src.shape[0]
    (c0, cw), (r0, rw) = ckv_col, kr_col
    return pl.pallas_call(
        functools.partial(_mla_kv_kernel, heads=heads, normalise=normalise),
        grid=(t // tm,),
        in_specs=[pl.BlockSpec((tm, cw), lambda i: (i, c0 // cw)),
                  pl.BlockSpec((tm, rw), lambda i: (i, r0 // rw)),
                  pl.BlockSpec((1, cw), lambda i: (0, 0)),
                  pl.BlockSpec((cw, wkv.shape[1]), lambda i: (0, 0)),
                  pl.BlockSpec((1, MLA_PAD), lambda i: (0, 0)),
                  pl.BlockSpec((tm, LANE), lambda i: (i, 0)),
                  pl.BlockSpec((tm, LANE), lambda i: (i, 0))],
        out_specs=[pl.BlockSpec((tm, cw), lambda i: (i, 0)),
                   pl.BlockSpec((tm, heads * MLA_PAD), lambda i: (i, 0)),
                   pl.BlockSpec((tm, heads * MLA_V), lambda i: (i, 0))],
        out_shape=[jax.ShapeDtypeStruct((t, cw), F32), jax.ShapeDtypeStruct((t, heads * MLA_PAD), BF16),
                   jax.ShapeDtypeStruct((t, heads * MLA_V), BF16)],
        compiler_params=_params("parallel"),
        name="mla_keys_values",
    )(ckv_src, kr_src, g_ckv.reshape(1, cw), wkv, g_k_pad.reshape(1, MLA_PAD), cos, sin)


def _gdn_prep_kernel(x_ref, prev_ref, next_ref, w_ref, o_ref, *, heads, tm, n_prompt_tiles, seq_lens):
    i = pl.program_id(0)
    x = x_ref[...]
    row = lax.broadcasted_iota(jnp.int32, (tm, 1), 0)
    in_prompt = i < n_prompt_tiles
    seq_len = jnp.where(in_prompt, seq_lens[0], seq_lens[1])
    first_row = jnp.where(in_prompt, i, i - n_prompt_tiles) * tm
    span = [min(s, tm) for s in seq_lens]
    local = jnp.where(in_prompt, row & (span[0] - 1), row & (span[1] - 1))
    span_t = jnp.where(in_prompt, span[0], span[1])
    tile_starts_seq = lax.rem(first_row, seq_len) == 0
    tile_ends_seq = lax.rem(first_row + tm, seq_len) == 0
    is_start = jnp.logical_and(local == 0, jnp.logical_or(row != 0, tile_starts_seq))
    is_end = jnp.logical_and(local == span_t - 1, jnp.logical_or(row != tm - 1, tile_ends_seq))
    prev = jnp.where(row == 0, prev_ref[7:8, :], pltpu.roll(x, 1, axis=0))
    nxt = jnp.where(row == tm - 1, next_ref[0:1, :], pltpu.roll(x, tm - 1, axis=0))
    prev = jnp.where(is_start, 0.0, prev)
    nxt = jnp.where(is_end, 0.0, nxt)
    conv = prev * w_ref[0:1, :] + x * w_ref[1:2, :] + nxt * w_ref[2:3, :]
    act = conv * jax.nn.sigmoid(conv)
    for h in range(3 * heads):
        hl = slice(h * GDN_DK, (h + 1) * GDN_DK)
        slab = act[:, hl]
        if h < 2 * heads:
            slab = slab * lax.rsqrt(jnp.sum(slab * slab, axis=-1, keepdims=True) + EPS)
            if h < heads:
                slab = slab * GDN_DK ** -0.5
        o_ref[:, hl] = slab


def gdn_prep(u, col, w_conv, *, heads, tm, t_prompt, seq_lens):
    t = u.shape[0]
    c0, cw = col
    assert all(s & (s - 1) == 0 for s in (tm,) + tuple(min(s, tm) for s in seq_lens))
    eight = 8
    cb = c0 // cw
    last8 = t // eight - 1
    return pl.pallas_call(
        functools.partial(_gdn_prep_kernel, heads=heads, tm=tm, n_prompt_tiles=t_prompt // tm, seq_lens=seq_lens),
        grid=(t // tm,),
        in_specs=[pl.BlockSpec((tm, cw), lambda i: (i, cb)),
                  pl.BlockSpec((eight, cw), lambda i: (jnp.maximum(i * (tm // eight) - 1, 0), cb)),
                  pl.BlockSpec((eight, cw), lambda i: (jnp.minimum((i + 1) * (tm // eight), last8), cb)),
                  pl.BlockSpec((3, cw), lambda i: (0, 0))],
        out_specs=pl.BlockSpec((tm, cw), lambda i: (i, 0)),
        out_shape=jax.ShapeDtypeStruct((t, cw), F32),
        compiler_params=_params("parallel"),
        name="gdn_prep",
    )(u, u, u, w_conv)


def _attn_kernel(*refs, n_seg, aliased, group, dq, dv, tk):
    q_ref = refs[0]
    kv_refs = [(refs[1 + 2 * s], refs[2 + 2 * s]) for s in range(n_seg)]
    o_ref, s_ref, mx_ref, l_ref, acc_ref = refs[1 + 2 * n_seg + (1 if aliased else 0):]
    tq = q_ref.shape[0]
    q = jnp.concatenate([q_ref[:, g * dq:(g + 1) * dq] for g in range(group)], axis=0)
    slabs = tk // LANE

    def for_chunks(body):
        base = 0
        for k_ref, v_ref in kv_refs:
            nk = k_ref.shape[0] // tk
            main = nk // ATTN_UNROLL

            def group_body(i, carry, k_ref=k_ref, v_ref=v_ref, base=base):
                for u in range(ATTN_UNROLL):
                    body(k_ref, v_ref, i * ATTN_UNROLL + u, base)
                return carry
            if main:
                lax.fori_loop(0, main, group_body, 0)
            for c in range(main * ATTN_UNROLL, nk):
                body(k_ref, v_ref, c, base)
            base += nk

    mx_ref[...] = jnp.full(mx_ref.shape, -jnp.inf, F32)

    def rows_of(c):
        return pl.ds(c * tk if isinstance(c, int) else pl.multiple_of(c * tk, tk), tk)

    def scores(k_ref, v_ref, c, base):
        s = lax.dot_general(q, k_ref[rows_of(c), :], (((1,), (1,)), ((), ())),
                            preferred_element_type=F32)
        s_ref[base + c] = s
        mx = mx_ref[...]
        for j in range(slabs):
            mx = jnp.maximum(mx, s[:, j * LANE:(j + 1) * LANE])
        mx_ref[...] = mx

    for_chunks(scores)
    m = jnp.broadcast_to(jnp.max(mx_ref[...], axis=-1, keepdims=True), mx_ref.shape)
    l_ref[...] = jnp.zeros_like(l_ref)
    acc_ref[...] = jnp.zeros_like(acc_ref)

    def weighted(k_ref, v_ref, c, base):
        s = s_ref[base + c]
        p = jnp.concatenate([jnp.exp2(s[:, j * LANE:(j + 1) * LANE] - m) for j in range(slabs)], axis=1)
        lsum = l_ref[...]
        for j in range(slabs):
            lsum = lsum + p[:, j * LANE:(j + 1) * LANE]
        l_ref[...] = lsum
        acc_ref[...] += jnp.dot(p.astype(BF16), v_ref[rows_of(c), :], preferred_element_type=F32)

    for_chunks(weighted)
    o = acc_ref[...] / jnp.sum(l_ref[...], axis=-1, keepdims=True)
    for g in range(group):
        o_ref[:, g * dv:(g + 1) * dv] = o[g * tq:(g + 1) * tq].astype(o_ref.dtype)


def attention(q, q_row0, kv_segments, *, batch, lq, kv_heads, group, dq, dv, out_rows, out_row0, prev_out, name):
    tq = _pick(lq, (512, 256, 128)) // (2 if group > 1 else 1)
    tk = _pick(math.gcd(*[length for _, _, _, length in kv_segments]), (256, 128))
    nq = lq // tq
    m_rows = group * tq
    assert q_row0 % tq == 0 and out_row0 % tq == 0
    in_specs = [pl.BlockSpec((tq, group * dq), lambda b, h, i: (q_row0 // tq + b * nq + i, h))]
    operands = [q]
    for k, v, first_row, length in kv_segments:
        assert first_row % length == 0
        in_specs += [pl.BlockSpec((length, dq), lambda b, h, i, f=first_row // length: (f + b, h)),
                     pl.BlockSpec((length, dv), lambda b, h, i, f=first_row // length: (f + b, h))]
        operands += [k, v]
    aliases = {}
    if prev_out is not None:
        in_specs.append(pl.BlockSpec(memory_space=pl.ANY))
        aliases = {len(operands): 0}
        operands.append(prev_out)
    n_chunks = sum(length // tk for _, _, _, length in kv_segments)
    return pl.pallas_call(
        functools.partial(_attn_kernel, n_seg=len(kv_segments), aliased=prev_out is not None, group=group,
                          dq=dq, dv=dv, tk=tk),
        grid=(batch, kv_heads, nq),
        in_specs=in_specs,
        out_specs=pl.BlockSpec((tq, group * dv), lambda b, h, i: (out_row0 // tq + b * nq + i, h)),
        out_shape=jax.ShapeDtypeStruct((out_rows, kv_heads * group * dv), BF16),
        input_output_aliases=aliases,
        scratch_shapes=[pltpu.VMEM((n_chunks, m_rows, tk), F32), pltpu.VMEM((m_rows, LANE), F32),
                        pltpu.VMEM((m_rows, LANE), F32), pltpu.VMEM((m_rows, dv), F32)],
        compiler_params=_params("parallel", "parallel", "parallel"),
        name=name,
    )(*operands)


def _split_hi_lo(a):
    hi = a.astype(BF16).astype(F32)
    return hi, (a - hi).astype(BF16).astype(F32)


def _dot_split(a_parts, b_parts):
    ah, al = a_parts
    bh, bl = b_parts
    lhs = jnp.concatenate([ah, al, ah, al], axis=2).astype(BF16)
    rhs = jnp.concatenate([bh, bh, bl, bl], axis=1).astype(BF16)
    return lax.dot_general(lhs, rhs, (((2,), (1,)), ((0,), (0,))), preferred_element_type=F32)


def _dot_nt(a, b):
    return lax.dot_general(a.astype(BF16), b.astype(BF16), (((2,), (2,)), ((0,), (0,))), preferred_element_type=F32)


def _chunk_masks(backward):
    r = lax.broadcasted_iota(jnp.int32, (GDN_CHUNK, GDN_CHUNK), 0)
    c = lax.broadcasted_iota(jnp.int32, (GDN_CHUNK, GDN_CHUNK), 1)
    incl = (r <= c) if backward else (r >= c)
    strict = (r < c) if backward else (r > c)
    return r, c, incl, strict


def _decay_mask(dcol, r, c, incl):
    drow = jnp.sum(jnp.where(r == c, dcol, 0.0), axis=1, keepdims=True)
    return jnp.where(incl, jnp.exp(jnp.where(incl, dcol - drow, 0.0)), 0.0)


def _unit_tri_inverse(a_mat, r, c):
    eye = (r == c).astype(F32)
    same16 = jnp.right_shift(r, 4) == jnp.right_shift(c, 4)
    same32 = jnp.right_shift(r, 5) == jnp.right_shift(c, 5)
    x = jnp.where(same16, -a_mat, 0.0)
    inv = eye + x
    x_parts = _split_hi_lo(x)
    for _ in range(3):
        x_parts = _split_hi_lo(_dot_split(x_parts, x_parts))
        inv = inv + _dot_split(_split_hi_lo(inv), x_parts)
    for off_blocks in (jnp.logical_and(same32, jnp.logical_not(same16)), jnp.logical_not(same32)):
        off = jnp.where(off_blocks, a_mat, 0.0)
        inv_parts = _split_hi_lo(inv)
        inv = inv - _dot_split(inv_parts, _split_hi_lo(_dot_split(_split_hi_lo(off), inv_parts)))
    return inv


def _head_column(ref, lane, index, n_c):
    col = jnp.sum(jnp.where(lane == index, ref[...], 0.0), axis=1, keepdims=True)
    return col.reshape(n_c, GDN_CHUNK, 1)


def _gdn_solve_kernel(k_ref, v_ref, d_ref, b_ref, uw_ref, *, head_lanes):
    h = pl.program_id(1)
    n_c = k_ref.shape[0] // GDN_CHUNK
    lane = lax.broadcasted_iota(jnp.int32, (1, d_ref.shape[1]), 1)
    kc = k_ref[...].reshape(n_c, GDN_CHUNK, GDN_DK)
    vc = v_ref[...].reshape(n_c, GDN_CHUNK, GDN_DV)
    for direction in range(2):
        r, c, incl, strict = _chunk_masks(direction == 1)
        dcol = _head_column(d_ref, lane, direction * head_lanes + h, n_c)
        bcol = _head_column(b_ref, lane, direction * head_lanes + h, n_c)
        kb = kc * bcol
        a_mat = jnp.where(strict, _dot_nt(kb, kc) * _decay_mask(dcol, r, c, incl), 0.0)
        rhs = jnp.concatenate([vc * bcol, kb * jnp.exp(dcol)], axis=2)
        uw = _dot_split(_split_hi_lo(_unit_tri_inverse(a_mat, r, c)), _split_hi_lo(rhs))
        uw_ref[direction] = uw.reshape(n_c * GDN_CHUNK, GDN_DV + GDN_DK)


def gdn_solve(qkv, dcs, beta, *, heads):
    t = qkv.shape[0]
    rows = _pick(t, (512, 256, 128, 64))
    width = GDN_DV + GDN_DK
    return pl.pallas_call(
        functools.partial(_gdn_solve_kernel, head_lanes=heads),
        grid=(t // rows, heads),
        in_specs=[pl.BlockSpec((rows, GDN_DK), lambda i, h: (i, heads + h)),
                  pl.BlockSpec((rows, GDN_DV), lambda i, h: (i, 2 * heads + h)),
                  pl.BlockSpec((rows, 2 * heads), lambda i, h: (i, 0)),
                  pl.BlockSpec((rows, 2 * heads), lambda i, h: (i, 0))],
        out_specs=pl.BlockSpec((2, rows, width), lambda i, h: (0, i, h)),
        out_shape=jax.ShapeDtypeStruct((2, t, heads * width), F32),
        compiler_params=_params("parallel", "parallel"),
        name="gdn_solve",
    )(qkv, qkv, dcs, beta)


def _gdn_scan_kernel(fwd_blk_ref, bwd_blk_ref, seq_ref, edge_ref, qf_ref, kf_ref, uwf_ref, df_ref,
                     qb_ref, kb_ref, uwb_ref, db_ref, s0_ref, of_ref, ob_ref, sfin_ref, s_ref, *,
                     heads_per_step, head_lanes):
    step = pl.program_id(1)
    hg = pl.program_id(0)
    rows = qf_ref.shape[0]
    n_c = rows // GDN_CHUNK
    lane = lax.broadcasted_iota(jnp.int32, (1, df_ref.shape[1]), 1)

    @pl.when((edge_ref[step] & 1) != 0)
    def _():
        s_ref[...] = s0_ref[...].astype(F32)

    def bmm(a, b):
        return lax.dot_general(a, b, (((2,), (1,)), ((0,), (0,))), preferred_element_type=F32)

    width = GDN_DV + GDN_DK
    streams = ((qf_ref, kf_ref, uwf_ref, df_ref, of_ref), (qb_ref, kb_ref, uwb_ref, db_ref, ob_ref))
    for ci in range(n_c):
        for direction, (q_ref, k_ref, uw_ref, d_ref, o_ref) in enumerate(streams):
            backward = direction == 1
            r, c, incl, _ = _chunk_masks(backward)
            pos = (n_c - 1 - ci) if backward else ci
            sl = slice(pos * GDN_CHUNK, (pos + 1) * GDN_CHUNK)
            heads = range(heads_per_step)
            dcol = jnp.stack([jnp.sum(jnp.where(lane == direction * head_lanes + hg * heads_per_step + hh,
                                                d_ref[sl, :], 0.0), axis=1, keepdims=True) for hh in heads])
            qc = jnp.stack([q_ref[sl, hh * GDN_DK:(hh + 1) * GDN_DK] for hh in heads])
            kc = jnp.stack([k_ref[sl, hh * GDN_DK:(hh + 1) * GDN_DK] for hh in heads])
            u = jnp.stack([uw_ref[sl, hh * width:hh * width + GDN_DV] for hh in heads])
            w = jnp.stack([uw_ref[sl, hh * width + GDN_DV:(hh + 1) * width] for hh in heads])
            attn = _dot_nt(qc, kc) * _decay_mask(dcol, r, c, incl)
            d_last = dcol[:, 0:1, :] if backward else dcol[:, GDN_CHUNK - 1:GDN_CHUNK, :]
            q_dec = (qc * jnp.exp(dcol)).astype(BF16)
            k_dec = (kc * jnp.exp(d_last - dcol)).astype(BF16)
            s = s_ref[direction]
            s_b = s.astype(BF16)
            v_new = u - bmm(w.astype(BF16), s_b)
            v_b = v_new.astype(BF16)
            o = bmm(q_dec, s_b) + bmm(attn.astype(BF16), v_b)
            for hh in heads:
                o_ref[sl, hh * GDN_DV:(hh + 1) * GDN_DV] = o[hh]
            s_ref[direction] = s * jnp.exp(d_last) + lax.dot_general(
                k_dec, v_b, (((1,), (1,)), ((0,), (0,))), preferred_element_type=F32)

    @pl.when((edge_ref[step] & 2) != 0)
    def _():
        sfin_ref[...] = s_ref[...]


def gdn_scan(qkv, uw, dcs, s0, *, sequences, heads):
    t = qkv.shape[0]
    rows = _pick(math.gcd(*[length for _, length in sequences]), (256, 128, 64))
    hps = _pick(heads, (4, 2, 1))
    hgroups = heads // hps
    width = GDN_DV + GDN_DK
    fwd_blk, bwd_blk, seq_id, edge = [], [], [], []
    for s, (first_row, length) in enumerate(sequences):
        assert first_row % rows == 0 and length % rows == 0
        n_l = length // rows
        for i in range(n_l):
            fwd_blk.append(first_row // rows + i)
            bwd_blk.append(first_row // rows + n_l - 1 - i)
            seq_id.append(s)
            edge.append((1 if i == 0 else 0) | (2 if i == n_l - 1 else 0))
    assert sorted(fwd_blk) == list(range(t // rows))
    tables = [jnp.asarray(np.asarray(v, np.int32)) for v in (fwd_blk, bwd_blk, seq_id, edge)]

    def stream_specs(table, direction):
        return [pl.BlockSpec((rows, hps * GDN_DK), lambda g, s, fb, bb, sq, ed: ((fb, bb)[table][s], g)),
                pl.BlockSpec((rows, hps * GDN_DK), lambda g, s, fb, bb, sq, ed: ((fb, bb)[table][s], hgroups + g)),
                pl.BlockSpec((None, rows, hps * width), lambda g, s, fb, bb, sq, ed: (direction, (fb, bb)[table][s], g)),
                pl.BlockSpec((rows, 2 * heads), lambda g, s, fb, bb, sq, ed: ((fb, bb)[table][s], 0))]

    state_spec = pl.BlockSpec((None, 2, hps, GDN_DK, GDN_DV), lambda g, s, fb, bb, sq, ed: (sq[s], 0, g, 0, 0))
    grid_spec = pltpu.PrefetchScalarGridSpec(
        num_scalar_prefetch=4,
        grid=(hgroups, len(fwd_blk)),
        in_specs=stream_specs(0, 0) + stream_specs(1, 1) + [state_spec],
        out_specs=[pl.BlockSpec((rows, hps * GDN_DV), lambda g, s, fb, bb, sq, ed: (fb[s], g)),
                   pl.BlockSpec((rows, hps * GDN_DV), lambda g, s, fb, bb, sq, ed: (bb[s], g)),
                   state_spec],
        scratch_shapes=[pltpu.VMEM((2, hps, GDN_DK, GDN_DV), F32)],
    )
    return pl.pallas_call(
        functools.partial(_gdn_scan_kernel, heads_per_step=hps, head_lanes=heads),
        grid_spec=grid_spec,
        out_shape=[jax.ShapeDtypeStruct((t, heads * GDN_DV), F32),
                   jax.ShapeDtypeStruct((t, heads * GDN_DV), F32),
                   jax.ShapeDtypeStruct((len(sequences), 2, heads, GDN_DK, GDN_DV), F32)],
        compiler_params=_params("parallel", "arbitrary"),
        name="gdn_scan",
    )(*tables, qkv, qkv, uw, dcs, qkv, qkv, uw, dcs, s0)


def _row_gather(idx_ref, src_hbm, dst_vmem, sem, n_rows):
    def start():
        def body(r, carry):
            pltpu.make_async_copy(src_hbm.at[pl.ds(idx_ref[0, r], 1), :], dst_vmem.at[pl.ds(r, 1), :], sem).start()
            return carry
        lax.fori_loop(0, n_rows, body, 0, unroll=GATHER_UNROLL)

    def wait():
        def body(r, carry):
            pltpu.make_async_copy(src_hbm.at[pl.ds(0, 1), :], dst_vmem.at[pl.ds(0, 1), :], sem).wait()
            return carry
        lax.fori_loop(0, n_rows, body, 0, unroll=GATHER_UNROLL)
    return start, wait


def _moe_kernel(be_ref, nu_ref, tok_ref, tok_next_ref, h_hbm, wg_ref, wu_ref, wd_ref, o_ref,
                xbuf_ref, xlo_ref, xhi_ref, acc_ref, sem):
    i = pl.program_id(0)
    f = pl.program_id(1)
    n_used = nu_ref[0]
    tm, half = xlo_ref.shape

    @pl.when(i < n_used)
    def _():
        @pl.when(f == 0)
        def _():
            for slot in range(2):
                start_this, wait_this = _row_gather(tok_ref, h_hbm, xbuf_ref.at[slot], sem.at[slot], tm)
                start_next, _ = _row_gather(tok_next_ref, h_hbm, xbuf_ref.at[1 - slot], sem.at[1 - slot], tm)

                @pl.when(lax.rem(i, 2) == slot)
                def _():
                    @pl.when(i == 0)
                    def _():
                        start_this()
                    wait_this()

                    @pl.when(i + 1 < n_used)
                    def _():
                        start_next()
                    lo, hi = _unpack_bf16_pair(xbuf_ref[slot])
                    xlo_ref[...] = lo.astype(BF16)
                    xhi_ref[...] = hi.astype(BF16)
            acc_ref[...] = jnp.zeros_like(acc_ref)

        def x_dot(w_ref):
            return (jnp.dot(xlo_ref[...], w_ref[:half, :].astype(BF16), preferred_element_type=F32)
                    + jnp.dot(xhi_ref[...], w_ref[half:, :].astype(BF16), preferred_element_type=F32))

        gate = x_dot(wg_ref)
        up = x_dot(wu_ref)
        hmid = (gate * jax.nn.sigmoid(gate) * up).astype(BF16)
        acc_ref[...] += jnp.dot(hmid, wd_ref[...].astype(BF16), preferred_element_type=F32)

        @pl.when(f == pl.num_programs(1) - 1)
        def _():
            o_ref[...] = _pack_bf16_pair(acc_ref[:, :half], acc_ref[:, half:])

    @pl.when(jnp.logical_and(i >= n_used, f == 0))
    def _():
        o_ref[...] = jnp.zeros_like(o_ref)


def moe_experts(h_packed, slot_tok, block_e, n_used, w_gate, w_up, w_down, *, tm, layer):
    d = 2 * h_packed.shape[1]
    p = slot_tok.shape[0]
    f_dim = w_gate.shape[3]
    tf = _pick(f_dim, (256, 128))
    n_blocks = p // tm
    last_f = f_dim // tf - 1
    tok_blocks = slot_tok.reshape(n_blocks, 1, tm)

    def blk(i, nu):
        return jnp.minimum(i, nu[0] - 1)

    def f_blk(i, f, nu):
        return jnp.where(i < nu[0], f, last_f)

    grid_spec = pltpu.PrefetchScalarGridSpec(
        num_scalar_prefetch=2,
        grid=(n_blocks, f_dim // tf),
        in_specs=[pl.BlockSpec((None, 1, tm), lambda i, f, be, nu: (blk(i, nu), 0, 0), memory_space=pltpu.SMEM),
                  pl.BlockSpec((None, 1, tm), lambda i, f, be, nu: (blk(i + 1, nu), 0, 0), memory_space=pltpu.SMEM),
                  pl.BlockSpec(memory_space=pl.ANY),
                  pl.BlockSpec((None, None, d, tf), lambda i, f, be, nu: (layer, be[blk(i, nu)], 0, f_blk(i, f, nu))),
                  pl.BlockSpec((None, None, d, tf), lambda i, f, be, nu: (layer, be[blk(i, nu)], 0, f_blk(i, f, nu))),
                  pl.BlockSpec((None, None, tf, d), lambda i, f, be, nu: (layer, be[blk(i, nu)], f_blk(i, f, nu), 0))],
        out_specs=pl.BlockSpec((tm, d // 2), lambda i, f, be, nu: (i, 0)),
        scratch_shapes=[pltpu.VMEM((2, tm, d // 2), jnp.uint32), pltpu.VMEM((tm, d // 2), BF16),
                        pltpu.VMEM((tm, d // 2), BF16), pltpu.VMEM((tm, d), F32),
                        pltpu.SemaphoreType.DMA((2,))],
    )
    return pl.pallas_call(
        _moe_kernel,
        grid_spec=grid_spec,
        out_shape=jax.ShapeDtypeStruct((p, d // 2), jnp.uint32),
        compiler_params=_params("arbitrary", "arbitrary"),
        name="moe_experts",
    )(block_e, n_used, tok_blocks, tok_blocks, h_packed, w_gate, w_up, w_down)


def _moe_combine_kernel(dest_ref, dest_next_ref, y_hbm, x_ref, gate_ref, mod_ref, o_ref, rows_ref, sem, *, gate_idx):
    i = pl.program_id(0)
    tc, d = x_ref.shape
    half = d // 2
    for slot in range(2):
        start_this, wait_this = _row_gather(dest_ref, y_hbm, rows_ref.at[slot], sem.at[slot], 2 * tc)
        start_next, _ = _row_gather(dest_next_ref, y_hbm, rows_ref.at[1 - slot], sem.at[1 - slot], 2 * tc)

        @pl.when(lax.rem(i, 2) == slot)
        def _():
            @pl.when(i == 0)
            def _():
                start_this()
            wait_this()

            @pl.when(i + 1 < pl.num_programs(0))
            def _():
                start_next()
            lo0, hi0 = _unpack_bf16_pair(rows_ref[slot, :tc, :])
            lo1, hi1 = _unpack_bf16_pair(rows_ref[slot, tc:, :])
            g0 = gate_ref[:, 0:1]
            g1 = gate_ref[:, 1:2]
            o_ref[:, :half] = x_ref[:, :half] + mod_ref[gate_idx:gate_idx + 1, :half] * (lo0 * g0 + lo1 * g1)
            o_ref[:, half:] = x_ref[:, half:] + mod_ref[gate_idx:gate_idx + 1, half:] * (hi0 * g0 + hi1 * g1)


def moe_combine_residual(y_packed, dest, gates, x, mod, *, gate_idx, tc, row_fn):
    t, d = x.shape
    n_tiles = t // tc
    dest_tiles = dest.reshape(n_tiles, tc, TOP_K).transpose(0, 2, 1).reshape(n_tiles, 1, TOP_K * tc)
    return pl.pallas_call(
        functools.partial(_moe_combine_kernel, gate_idx=gate_idx),
        grid=(n_tiles,),
        in_specs=[pl.BlockSpec((None, 1, TOP_K * tc), lambda i: (i, 0, 0), memory_space=pltpu.SMEM),
                  pl.BlockSpec((None, 1, TOP_K * tc), lambda i: (jnp.minimum(i + 1, n_tiles - 1), 0, 0),
                               memory_space=pltpu.SMEM),
                  pl.BlockSpec(memory_space=pl.ANY),
                  pl.BlockSpec((tc, d), lambda i: (i, 0)),
                  pl.BlockSpec((tc, gates.shape[1]), lambda i: (i, 0)),
                  pl.BlockSpec((None, mod.shape[1], d), lambda i: (row_fn(i), 0, 0))],
        out_specs=pl.BlockSpec((tc, d), lambda i: (i, 0)),
        out_shape=jax.ShapeDtypeStruct((t, d), F32),
        scratch_shapes=[pltpu.VMEM((2, TOP_K * tc, d // 2), jnp.uint32), pltpu.SemaphoreType.DMA((2,))],
        compiler_params=_params("arbitrary"),
        name="moe_combine",
    )(dest_tiles, dest_tiles, y_packed, x, gates, mod)


def _rms(x, g):
    return x * lax.rsqrt(jnp.mean(x * x, axis=-1, keepdims=True) + EPS) * g


def _rope_tables(t_prompt, dec_batch, dec_seq, width):
    a = width // 2
    half = a // 2
    freqs = ROPE_THETA ** (-np.arange(half, dtype=np.float32) / half)
    t = np.arange(dec_seq)
    row, col = t // GRID_W, t % GRID_W
    ang_r = row[:, None].astype(np.float32) * freqs[None, :]
    ang_c = col[:, None].astype(np.float32) * freqs[None, :]
    cos = np.concatenate([np.cos(ang_r), np.cos(ang_r), np.cos(ang_c), np.cos(ang_c)], axis=-1)
    sin = np.concatenate([-np.sin(ang_r), np.sin(ang_r), -np.sin(ang_c), np.sin(ang_c)], axis=-1)
    cos = np.concatenate([np.ones((t_prompt, width), np.float32), np.tile(cos, (dec_batch, 1))], axis=0)
    sin = np.concatenate([np.zeros((t_prompt, width), np.float32), np.tile(sin, (dec_batch, 1))], axis=0)
    return jnp.asarray(cos), jnp.asarray(sin)


def _gdn_gates(a, b, a_log, dt_bias, heads):
    t = a.shape[0]
    g = -jnp.exp(a_log.reshape(2 * heads)) * jax.nn.softplus(a + dt_bias.reshape(2 * heads))
    gc = g.reshape(t // GDN_CHUNK, GDN_CHUNK, 2 * heads)
    prefix = jnp.cumsum(gc[..., :heads], axis=1)
    suffix = jnp.flip(jnp.cumsum(jnp.flip(gc[..., heads:], axis=1), axis=1), axis=1)
    dcs = jnp.concatenate([prefix, suffix], axis=-1).reshape(t, 2 * heads)
    return dcs, jax.nn.sigmoid(b)


def _dispatch_plan(experts, n_experts, tm):
    t, k = experts.shape
    a = t * k
    n_blocks = a // tm + n_experts
    flat_e = experts.reshape(-1)
    onehot = (flat_e[:, None] == jnp.arange(n_experts, dtype=jnp.int32)[None, :]).astype(jnp.int32)
    csum = jnp.cumsum(onehot, axis=0)
    pos = jnp.sum((csum - onehot) * onehot, axis=1)
    counts = csum[-1]
    padded = (counts + tm - 1) // tm * tm
    pend = jnp.cumsum(padded)
    pstart = pend - padded
    dest = pstart[flat_e] + pos
    flat_tok = jnp.arange(a, dtype=jnp.int32) // k
    slot_tok = jnp.zeros((n_blocks * tm,), jnp.int32).at[dest].set(flat_tok)
    block_e = jnp.minimum(jnp.searchsorted(pend, jnp.arange(n_blocks, dtype=jnp.int32) * tm, side='right'),
                          n_experts - 1).astype(jnp.int32)
    n_used = (pend[-1] // tm).astype(jnp.int32).reshape(1)
    return dest.reshape(t, k), slot_tok, block_e, n_used


def kernel(x_prompt, x_sample, cache_mla_ckv, cache_mla_krope, cache_gqa_k, cache_gqa_v, state_gdn, c, c_ctx, w_mod, b_mod, g_norm_mix, g_norm_ffn, w_in, g_cq, w_q_up, g_ckv, w_kv_up, g_q_mla, g_k_mla, g_q_gqa, g_k_gqa, w_conv, a_log, dt_bias, g_gdn_out, w_out, w_router_group, b_router_group, w_router_expert, b_router_expert, w_gate, w_up, w_down):
    batch, seq, d = x_prompt.shape
    dec_batch, dec_seq, _ = x_sample.shape
    depth = w_in.shape[0]
    past = cache_mla_ckv.shape[2]
    q_lora = w_q_up.shape[1]
    kv_lora = w_kv_up.shape[1]
    mla_heads = w_q_up.shape[2] // MLA_QK
    kv_heads = cache_gqa_k.shape[3]
    gdn_heads = state_gdn.shape[3]
    gqa_heads = (w_out.shape[1] - mla_heads * MLA_V - gdn_heads * GDN_DV) // HEAD_DIM
    group = gqa_heads // kv_heads
    n_groups = w_router_group.shape[2]
    n_experts = w_gate.shape[1]
    gdn_qkv = gdn_heads * (2 * GDN_DK + GDN_DV)
    gdn_width = gdn_heads * GDN_DV

    t_p = batch * seq
    t_s = dec_batch * dec_seq
    t_all = t_p + t_s
    tm = _pick(math.gcd(t_p, dec_seq), (512, 256, 128))
    row_fn = _mod_row_fn(tm, t_p, dec_seq)
    moe_tm = _pick(t_all * TOP_K, (512, 256, 128))
    combine_tc = _pick(math.gcd(t_p, dec_seq), (256, 128))

    src_sizes = (q_lora, kv_lora, MLA_ROPE, gqa_heads * HEAD_DIM, kv_heads * HEAD_DIM, kv_heads * HEAD_DIM,
                 gdn_qkv, gdn_width, 2 * gdn_heads, 2 * gdn_heads)
    src_off = np.concatenate([[0], np.cumsum(src_sizes)])
    src = dict(zip(('cq', 'ckv', 'krope', 'gq', 'gk', 'gv', 'qkv', 'z', 'a', 'b'),
                   [(int(src_off[j]), int(src_off[j + 1])) for j in range(len(src_sizes))]))
    tail_w = -(-(MLA_ROPE + 4 * gdn_heads) // LANE) * LANE
    cols, pieces, pos = {}, [], 0
    for name, width in (('qkv', gdn_qkv), ('z', gdn_width), ('ckv', kv_lora), ('gq', gqa_heads * HEAD_DIM),
                        ('cq', q_lora), ('tail', tail_w), ('gk', kv_heads * HEAD_DIM), ('gv', kv_heads * HEAD_DIM)):
        start = -(-pos // width) * width
        assert width % LANE == 0
        if start > pos:
            pieces.append(('pad', start - pos))
        pieces.append((name, width))
        cols[name] = (start, width)
        pos = start + width
    n_in = pos
    tail0 = cols['tail'][0]

    def build_w_in(w):
        parts = []
        for name, width in pieces:
            if name == 'pad':
                parts.append(jnp.zeros((d, width), F32))
            elif name == 'tail':
                parts += [w[:, src[s][0]:src[s][1]] for s in ('krope', 'a', 'b')]
                parts.append(jnp.zeros((d, width - MLA_ROPE - 4 * gdn_heads), F32))
            else:
                parts.append(w[:, src[name][0]:src[name][1]])
        return jnp.concatenate(parts, axis=1).astype(BF16)

    def padded_rope_tables(width):
        cos, sin = _rope_tables(t_p, dec_batch, dec_seq, width)
        return (jnp.pad(cos, ((0, 0), (0, LANE - width)), constant_values=1.0),
                jnp.pad(sin, ((0, 0), (0, LANE - width))))

    cos_g, sin_g = padded_rope_tables(HEAD_DIM)
    cos_m, sin_m = padded_rope_tables(MLA_ROPE)
    n_ctx = dec_batch * past
    cos_id, sin_id = jnp.ones((n_ctx, LANE), F32), jnp.zeros((n_ctx, LANE), F32)
    sequences = ([(b * seq, seq) for b in range(batch)]
                 + [(t_p + b * dec_seq, dec_seq) for b in range(dec_batch)])

    cond8 = jnp.zeros((8, d), F32).at[0].set(c_ctx).at[1:1 + dec_batch].set(c)

    x = jnp.concatenate([x_prompt.reshape(t_p, d), x_sample.reshape(t_s, d)], axis=0)
    new_ckv, new_krope, new_k, new_v, new_s = [], [], [], [], []
    q_scale_m = MLA_QK ** -0.5 * LOG2E
    q_scale_g = HEAD_DIM ** -0.5 * LOG2E

    for l in range(depth):
        mod = modulation(cond8, w_mod, b_mod, l).reshape(8, 6, d)

        u = norm_matmul(x, g_norm_mix[l], mod, build_w_in(w_in[l]), sh_idx=0, sc_idx=1, tm=tm, row_fn=row_fn)

        wq = w_q_up[l].reshape(q_lora, mla_heads, MLA_QK)
        wq = jnp.pad(wq, ((0, 0), (0, 0), (0, MLA_PAD - MLA_QK))).reshape(q_lora, mla_heads * MLA_PAD).astype(BF16)
        q_m = mla_queries(u, cols, g_cq[l], wq, jnp.pad(g_q_mla[l], (0, MLA_PAD - MLA_QK)), cos_m, sin_m,
                          tm=tm, q_scale=q_scale_m)
        wkv = w_kv_up[l].reshape(kv_lora, mla_heads, MLA_NOPE + MLA_V)
        wkv = jnp.concatenate([wkv[..., :MLA_NOPE].reshape(kv_lora, -1), wkv[..., MLA_NOPE:].reshape(kv_lora, -1)],
                              axis=1).astype(BF16)
        g_k_pad = jnp.pad(g_k_mla[l], (0, MLA_PAD - MLA_QK))
        ckv_n, k_m, v_m = mla_keys_values(u, cols['ckv'], u, cols['tail'], g_ckv[l], wkv, g_k_pad, cos_m, sin_m,
                                          heads=mla_heads, tm=tm, normalise=True)
        ctx_krope = jnp.pad(cache_mla_krope[:, l].reshape(n_ctx, MLA_ROPE), ((0, 0), (0, LANE - MLA_ROPE)))
        _, k_m_ctx, v_m_ctx = mla_keys_values(cache_mla_ckv[:, l].reshape(n_ctx, kv_lora), (0, kv_lora), ctx_krope,
                                              (0, LANE), g_ckv[l], wkv, g_k_pad, cos_id, sin_id, heads=mla_heads,
                                              tm=_pick(n_ctx, (512, 256, 128)), normalise=False)

        mla_common = dict(kv_heads=mla_heads, group=1, dq=MLA_PAD, dv=MLA_V, out_rows=t_all)
        o_m = attention(q_m, 0, [(k_m, v_m, 0, seq)], batch=batch, lq=seq, out_row0=0,
                        prev_out=jnp.zeros((t_all, mla_heads * MLA_V), BF16), name="mla_attn_ctx", **mla_common)
        o_m = attention(q_m, t_p, [(k_m_ctx, v_m_ctx, 0, past), (k_m, v_m, t_p, dec_seq)], batch=dec_batch,
                        lq=dec_seq, out_row0=t_p, prev_out=o_m, name="mla_attn_lat", **mla_common)

        q_g, k_g_rot, k_g, v_g_b = gqa_prep(u, cols, g_q_gqa[l], g_k_gqa[l], cos_g, sin_g, tm=tm, q_scale=q_scale_g)
        ctx_k = cache_gqa_k[:, l].reshape(n_ctx, kv_heads * HEAD_DIM).astype(BF16)
        ctx_v = cache_gqa_v[:, l].reshape(n_ctx, kv_heads * HEAD_DIM).astype(BF16)
        gqa_common = dict(kv_heads=kv_heads, group=group, dq=HEAD_DIM, dv=HEAD_DIM, out_rows=t_all)
        o_g = attention(q_g, 0, [(k_g_rot, v_g_b, 0, seq)], batch=batch, lq=seq, out_row0=0,
                        prev_out=jnp.zeros((t_all, gqa_heads * HEAD_DIM), BF16), name="gqa_attn_ctx", **gqa_common)
        o_g = attention(q_g, t_p, [(ctx_k, ctx_v, 0, past), (k_g_rot, v_g_b, t_p, dec_seq)], batch=dec_batch,
                        lq=dec_seq, out_row0=t_p, prev_out=o_g, name="gqa_attn_lat", **gqa_common)

        qkv_c = gdn_prep(u, cols['qkv'], w_conv[l], heads=gdn_heads, tm=tm, t_prompt=t_p, seq_lens=(seq, dec_seq))
        tail = u[:, tail0:tail0 + MLA_ROPE + 4 * gdn_heads]
        dcs, beta = _gdn_gates(tail[:, MLA_ROPE:MLA_ROPE + 2 * gdn_heads], tail[:, MLA_ROPE + 2 * gdn_heads:],
                               a_log[l], dt_bias[l], gdn_heads)
        uw = gdn_solve(qkv_c, dcs, beta, heads=gdn_heads)
        s0 = jnp.concatenate([jnp.zeros((batch, 2, gdn_heads, GDN_DK, GDN_DV), F32), state_gdn[:, l]], axis=0)
        o_fwd, o_bwd, s_fin = gdn_scan(qkv_c, uw, dcs, s0, sequences=sequences, heads=gdn_heads)

        new_ckv.append(ckv_n[:t_p].reshape(batch, seq, kv_lora))
        new_krope.append(tail[:t_p, :MLA_ROPE].reshape(batch, seq, MLA_ROPE))
        new_k.append(k_g[:t_p].reshape(batch, seq, kv_heads, HEAD_DIM))
        new_v.append(u[:t_p, cols['gv'][0]:cols['gv'][0] + cols['gv'][1]].reshape(batch, seq, kv_heads, HEAD_DIM))
        new_s.append(s_fin[:batch])

        x = out_proj_residual(o_m, o_g, o_fwd, o_bwd, u, cols['z'], g_gdn_out[l], w_out[l].astype(BF16), x, mod,
                              gate_idx=2, tm=tm, row_fn=row_fn)

        w_router = jnp.concatenate([w_router_group[l], w_router_expert[l],
                                    jnp.zeros((d, LANE - n_groups - n_experts), F32)], axis=1)
        b_router = jnp.concatenate([b_router_group[l], b_router_expert[l],
                                    jnp.zeros((LANE - n_groups - n_experts,), F32)]).reshape(1, LANE)
        h2, gates, experts = ffn_norm_router(x, g_norm_ffn[l], mod, w_router, b_router, sh_idx=3, sc_idx=4, tm=tm,
                                             row_fn=row_fn, n_groups=n_groups, epg=n_experts // n_groups)
        dest, slot_tok, block_e, n_used = _dispatch_plan(experts[:, :TOP_K], n_experts, moe_tm)
        yb = moe_experts(h2, slot_tok, block_e, n_used, w_gate, w_up, w_down, tm=moe_tm, layer=l)
        x = moe_combine_residual(yb, dest, gates, x, mod, gate_idx=5, tc=combine_tc,
                                 row_fn=_mod_row_fn(combine_tc, t_p, dec_seq))

    return (x[:t_p].reshape(batch, seq, d), x[t_p:].reshape(dec_batch, dec_seq, d),
            jnp.stack(new_ckv, axis=1), jnp.stack(new_krope, axis=1), jnp.stack(new_k, axis=1),
            jnp.stack(new_v, axis=1), jnp.stack(new_s, axis=1))
```

```python
import functools
import math

import numpy as np
import jax
import jax.numpy as jnp
from jax import lax
from jax.experimental import pallas as pl
from jax.experimental.pallas import tpu as pltpu

F32 = jnp.float32
BF16 = jnp.bfloat16

EPS = 1e-6
ROPE_THETA = 10000.0
GRID_W = 64
HEAD_DIM = 128
MLA_NOPE = 128
MLA_ROPE = 64
MLA_V = 128
MLA_QK = MLA_NOPE + MLA_ROPE
MLA_PAD = 256
GDN_DK = 128
GDN_DV = 128
GDN_CHUNK = 64
TOP_K = 2
LANE = 128
VMEM_LIMIT_BYTES = 56 * 1024 * 1024
LOG2E = 1.4426950408889634
ATTN_UNROLL = 8
GATHER_UNROLL = 8


def _params(*sem):
    return pltpu.CompilerParams(dimension_semantics=sem, vmem_limit_bytes=VMEM_LIMIT_BYTES)


def _pick(n, prefs):
    for p in prefs:
        if n % p == 0:
            return p
    return n


def _mod_kernel(c_ref, w_ref, b_ref, o_ref):
    c = c_ref[...]
    s = (c * jax.nn.sigmoid(c)).astype(BF16)
    o_ref[...] = jnp.dot(s, w_ref[...].astype(BF16), preferred_element_type=F32) + b_ref[...]


def modulation(cond8, w_mod, b_mod, layer):
    rows, d = cond8.shape
    depth, _, n = w_mod.shape
    tn = _pick(n, (512, 256, 128))
    return pl.pallas_call(
        _mod_kernel,
        grid=(n // tn,),
        in_specs=[pl.BlockSpec((rows, d), lambda j: (0, 0)),
                  pl.BlockSpec((None, d, tn), lambda j: (layer, 0, j)),
                  pl.BlockSpec((None, 1, tn), lambda j: (layer, 0, j))],
        out_specs=pl.BlockSpec((rows, tn), lambda j: (0, j)),
        out_shape=jax.ShapeDtypeStruct((rows, n), F32),
        compiler_params=_params("parallel"),
        name="modulation",
    )(cond8, w_mod, b_mod.reshape(depth, 1, n))


def _mod_row_fn(tm, t_prompt, dec_seq):
    n_p = t_prompt // tm
    per_b = dec_seq // tm

    def row(i):
        return jnp.where(i < n_p, 0, 1 + (i - n_p) // per_b)
    return row


def _modulated_norm(x, g, sc, sh):
    ms = jnp.mean(x * x, axis=-1, keepdims=True)
    return (x * lax.rsqrt(ms + EPS) * g) * (1.0 + sc) + sh


def _norm_matmul_kernel(x_ref, g_ref, mod_ref, w_ref, o_ref, h_ref, *, sh_idx, sc_idx):
    @pl.when(pl.program_id(1) == 0)
    def _():
        h = _modulated_norm(x_ref[...], g_ref[...], mod_ref[sc_idx:sc_idx + 1, :],
                            mod_ref[sh_idx:sh_idx + 1, :])
        h_ref[...] = h.astype(BF16)
    o_ref[...] = jnp.dot(h_ref[...], w_ref[...], preferred_element_type=F32).astype(o_ref.dtype)


def norm_matmul(x, g, mod, w, *, sh_idx, sc_idx, tm, row_fn, out_dtype=F32):
    t, d = x.shape
    n = w.shape[1]
    tn = _pick(n, (1152, 1024, 512, 384, 256, 128))
    return pl.pallas_call(
        functools.partial(_norm_matmul_kernel, sh_idx=sh_idx, sc_idx=sc_idx),
        grid=(t // tm, n // tn),
        in_specs=[pl.BlockSpec((tm, d), lambda i, j: (i, 0)),
                  pl.BlockSpec((1, d), lambda i, j: (0, 0)),
                  pl.BlockSpec((None, mod.shape[1], d), lambda i, j: (row_fn(i), 0, 0)),
                  pl.BlockSpec((d, tn), lambda i, j: (0, j))],
        out_specs=pl.BlockSpec((tm, tn), lambda i, j: (i, j)),
        out_shape=jax.ShapeDtypeStruct((t, n), out_dtype),
        scratch_shapes=[pltpu.VMEM((tm, d), BF16)],
        compiler_params=_params("parallel", "arbitrary"),
        name="norm_in_proj",
    )(x, g.reshape(1, d), mod, w)


def _out_proj_kernel(om_ref, og_ref, of_ref, ob_ref, z_ref, gd_ref, wm_ref, wg_ref, wd_ref, x_ref, mod_ref,
                     o_ref, od_ref, *, gate_idx):
    @pl.when(pl.program_id(1) == 0)
    def _():
        for h in range(of_ref.shape[1] // GDN_DV):
            hl = slice(h * GDN_DV, (h + 1) * GDN_DV)
            z = z_ref[:, hl]
            od_ref[:, hl] = (_head_rms(of_ref[:, hl] + ob_ref[:, hl], gd_ref[...])
                             * (z * jax.nn.sigmoid(z))).astype(BF16)
    acc = jnp.dot(om_ref[...], wm_ref[...], preferred_element_type=F32)
    acc += jnp.dot(og_ref[...], wg_ref[...], preferred_element_type=F32)
    acc += jnp.dot(od_ref[...], wd_ref[...], preferred_element_type=F32)
    o_ref[...] = x_ref[...] + mod_ref[gate_idx:gate_idx + 1, :] * acc


def out_proj_residual(o_m, o_g, o_fwd, o_bwd, u, z_col, g_gdn, w_out, x, mod, *, gate_idx, tm, row_fn):
    t, d = x.shape
    km, kg, kd = o_m.shape[1], o_g.shape[1], o_fwd.shape[1]
    z0, zw = z_col
    assert km == kg and (km + kg) % kd == 0 and zw == kd
    tn = _pick(d, (1024, 512, 256, 128))
    return pl.pallas_call(
        functools.partial(_out_proj_kernel, gate_idx=gate_idx),
        grid=(t // tm, d // tn),
        in_specs=[pl.BlockSpec((tm, km), lambda i, j: (i, 0)),
                  pl.BlockSpec((tm, kg), lambda i, j: (i, 0)),
                  pl.BlockSpec((tm, kd), lambda i, j: (i, 0)),
                  pl.BlockSpec((tm, kd), lambda i, j: (i, 0)),
                  pl.BlockSpec((tm, zw), lambda i, j: (i, z0 // zw)),
                  pl.BlockSpec((1, GDN_DV), lambda i, j: (0, 0)),
                  pl.BlockSpec((km, tn), lambda i, j: (0, j)),
                  pl.BlockSpec((kg, tn), lambda i, j: (1, j)),
                  pl.BlockSpec((kd, tn), lambda i, j: ((km + kg) // kd, j)),
                  pl.BlockSpec((tm, tn), lambda i, j: (i, j)),
                  pl.BlockSpec((None, mod.shape[1], tn), lambda i, j: (row_fn(i), 0, j))],
        out_specs=pl.BlockSpec((tm, tn), lambda i, j: (i, j)),
        out_shape=jax.ShapeDtypeStruct((t, d), F32),
        scratch_shapes=[pltpu.VMEM((tm, kd), BF16)],
        compiler_params=_params("parallel", "arbitrary"),
        name="out_proj",
    )(o_m, o_g, o_fwd, o_bwd, u, g_gdn.reshape(1, GDN_DV), w_out, w_out, w_out, x, mod)


def _pack_bf16_pair(lo, hi):
    lo_bits = lax.bitcast_convert_type(lo.astype(BF16).astype(F32), jnp.uint32)
    hi_bits = lax.bitcast_convert_type(hi.astype(BF16).astype(F32), jnp.uint32)
    return jnp.right_shift(lo_bits, jnp.uint32(16)) | (hi_bits & jnp.uint32(0xFFFF0000))


def _unpack_bf16_pair(word):
    lo = lax.bitcast_convert_type(jnp.left_shift(word, jnp.uint32(16)), F32)
    hi = lax.bitcast_convert_type(word & jnp.uint32(0xFFFF0000), F32)
    return lo, hi


def _ffn_norm_router_kernel(x_ref, g_ref, mod_ref, wr_ref, br_ref, h_ref, gate_ref, exp_ref, *,
                            sh_idx, sc_idx, n_groups, epg):
    h = _modulated_norm(x_ref[...], g_ref[...], mod_ref[sc_idx:sc_idx + 1, :],
                        mod_ref[sh_idx:sh_idx + 1, :])
    half = h.shape[1] // 2
    h_ref[...] = _pack_bf16_pair(h[:, :half], h[:, half:])
    h_hi = h.astype(BF16)
    h_lo = (h - h_hi.astype(F32)).astype(BF16)
    w = wr_ref[...]
    w_hi = w.astype(BF16)
    w_lo = (w - w_hi.astype(F32)).astype(BF16)
    lg = (jnp.dot(h_hi, w_hi, preferred_element_type=F32) + jnp.dot(h_lo, w_hi, preferred_element_type=F32)
          + jnp.dot(h_hi, w_lo, preferred_element_type=F32)) + br_ref[...]
    lane = lax.broadcasted_iota(jnp.int32, lg.shape, 1).astype(F32)
    neg = jnp.float32(-1e30)
    far = jnp.float32(2 * LANE)

    def first_argmax(v):
        top = jnp.max(v, axis=-1, keepdims=True)
        return top, jnp.min(jnp.where(v == top, lane, far), axis=-1, keepdims=True)

    is_group = lane < n_groups
    g_max, g_idx = first_argmax(jnp.where(is_group, lg, neg))
    g_top = 1.0 / jnp.sum(jnp.where(is_group, jnp.exp(lg - g_max), 0.0), axis=-1, keepdims=True)
    lo = n_groups + g_idx * epg
    el = jnp.where(jnp.logical_and(lane >= lo, lane < lo + epg), lg, neg)
    e1, i1 = first_argmax(el)
    e2, i2 = first_argmax(jnp.where(lane == i1, neg, el))
    r = jnp.exp(e2 - e1)
    gate1 = g_top / (1.0 + r)
    gate2 = g_top * r / (1.0 + r)
    gate_ref[...] = jnp.where(lane == 0, gate1, jnp.where(lane == 1, gate2, 0.0))
    exp_ref[...] = jnp.where(lane == 0, i1 - n_groups, jnp.where(lane == 1, i2 - n_groups, 0.0)).astype(jnp.int32)


def ffn_norm_router(x, g, mod, w_router, b_router, *, sh_idx, sc_idx, tm, row_fn, n_groups, epg):
    t, d = x.shape
    nr = w_router.shape[1]
    assert d % (2 * LANE) == 0
    return pl.pallas_call(
        functools.partial(_ffn_norm_router_kernel, sh_idx=sh_idx, sc_idx=sc_idx, n_groups=n_groups, epg=epg),
        grid=(t // tm,),
        in_specs=[pl.BlockSpec((tm, d), lambda i: (i, 0)),
                  pl.BlockSpec((1, d), lambda i: (0, 0)),
                  pl.BlockSpec((None, mod.shape[1], d), lambda i: (row_fn(i), 0, 0)),
                  pl.BlockSpec((d, nr), lambda i: (0, 0)),
                  pl.BlockSpec((1, nr), lambda i: (0, 0))],
        out_specs=[pl.BlockSpec((tm, d // 2), lambda i: (i, 0)),
                   pl.BlockSpec((tm, nr), lambda i: (i, 0)),
                   pl.BlockSpec((tm, nr), lambda i: (i, 0))],
        out_shape=[jax.ShapeDtypeStruct((t, d // 2), jnp.uint32), jax.ShapeDtypeStruct((t, nr), F32),
                   jax.ShapeDtypeStruct((t, nr), jnp.int32)],
        compiler_params=_params("parallel"),
        name="ffn_norm_router",
    )(x, g.reshape(1, d), mod, w_router, b_router)


def _rope_rotate(x, cos, sin, pair):
    lane = lax.broadcasted_iota(jnp.int32, x.shape, 1)
    first = (lane & (2 * pair - 1)) < pair
    partner = jnp.where(first, pltpu.roll(x, LANE - pair, axis=1), pltpu.roll(x, pair, axis=1))
    return x * cos + partner * sin


def _head_rms(x, g):
    return x * lax.rsqrt(jnp.mean(x * x, axis=-1, keepdims=True) + EPS) * g


def _gqa_prep_kernel(gq_ref, gk_ref, gv_ref, gq_gain_ref, gk_gain_ref, cos_ref, sin_ref,
                     q_ref, k_rot_ref, k_ref, v_ref, *, q_scale):
    cos, sin = cos_ref[...], sin_ref[...]
    pair = HEAD_DIM // 4
    for h in range(gq_ref.shape[1] // HEAD_DIM):
        hl = slice(h * HEAD_DIM, (h + 1) * HEAD_DIM)
        q = _rope_rotate(_head_rms(gq_ref[:, hl], gq_gain_ref[...]), cos, sin, pair)
        q_ref[:, hl] = (q * q_scale).astype(BF16)
    for h in range(gk_ref.shape[1] // HEAD_DIM):
        hl = slice(h * HEAD_DIM, (h + 1) * HEAD_DIM)
        k = _head_rms(gk_ref[:, hl], gk_gain_ref[...])
        k_ref[:, hl] = k
        k_rot_ref[:, hl] = _rope_rotate(k, cos, sin, pair).astype(BF16)
    v_ref[...] = gv_ref[...].astype(BF16)


def gqa_prep(u, cols, g_q, g_k, cos, sin, *, tm, q_scale):
    t = u.shape[0]
    (q0, qw), (k0, kw), (v0, vw) = cols['gq'], cols['gk'], cols['gv']
    row = lambda i: (i, 0)
    return pl.pallas_call(
        functools.partial(_gqa_prep_kernel, q_scale=q_scale),
        grid=(t // tm,),
        in_specs=[pl.BlockSpec((tm, qw), lambda i: (i, q0 // qw)),
                  pl.BlockSpec((tm, kw), lambda i: (i, k0 // kw)),
                  pl.BlockSpec((tm, vw), lambda i: (i, v0 // vw)),
                  pl.BlockSpec((1, HEAD_DIM), lambda i: (0, 0)),
                  pl.BlockSpec((1, HEAD_DIM), lambda i: (0, 0)),
                  pl.BlockSpec((tm, LANE), row),
                  pl.BlockSpec((tm, LANE), row)],
        out_specs=[pl.BlockSpec((tm, qw), row), pl.BlockSpec((tm, kw), row),
                   pl.BlockSpec((tm, kw), row), pl.BlockSpec((tm, vw), row)],
        out_shape=[jax.ShapeDtypeStruct((t, qw), BF16), jax.ShapeDtypeStruct((t, kw), BF16),
                   jax.ShapeDtypeStruct((t, kw), F32), jax.ShapeDtypeStruct((t, vw), BF16)],
        compiler_params=_params("parallel"),
        name="gqa_prep",
    )(u, u, u, g_q.reshape(1, HEAD_DIM), g_k.reshape(1, HEAD_DIM), cos, sin)


def _mla_q_kernel(cq_ref, g_cq_ref, wq_ref, g_q_ref, cos_ref, sin_ref, q_ref, *, q_scale):
    cqn = _head_rms(cq_ref[...], g_cq_ref[...]).astype(BF16)
    q = jnp.dot(cqn, wq_ref[...], preferred_element_type=F32)
    cos, sin = cos_ref[...], sin_ref[...]
    g_nope, g_rope = g_q_ref[:, :MLA_NOPE], g_q_ref[:, MLA_NOPE:]
    for h in range(q.shape[1] // MLA_PAD):
        nope = q[:, h * MLA_PAD:h * MLA_PAD + MLA_NOPE]
        rope = q[:, h * MLA_PAD + MLA_NOPE:(h + 1) * MLA_PAD]
        ss = jnp.sum(nope * nope, axis=-1, keepdims=True) + jnp.sum(rope * rope, axis=-1, keepdims=True)
        r = lax.rsqrt(ss / MLA_QK + EPS) * q_scale
        q_ref[:, h * MLA_PAD:h * MLA_PAD + MLA_NOPE] = (nope * r * g_nope).astype(BF16)
        q_ref[:, h * MLA_PAD + MLA_NOPE:(h + 1) * MLA_PAD] = (
            _rope_rotate(rope * g_rope, cos, sin, MLA_ROPE // 4) * r).astype(BF16)


def mla_queries(u, cols, g_cq, wq, g_q_pad, cos, sin, *, tm, q_scale):
    t = u.shape[0]
    c0, cw = cols['cq']
    n = wq.shape[1]
    return pl.pallas_call(
        functools.partial(_mla_q_kernel, q_scale=q_scale),
        grid=(t // tm,),
        in_specs=[pl.BlockSpec((tm, cw), lambda i: (i, c0 // cw)),
                  pl.BlockSpec((1, cw), lambda i: (0, 0)),
                  pl.BlockSpec((cw, n), lambda i: (0, 0)),
                  pl.BlockSpec((1, MLA_PAD), lambda i: (0, 0)),
                  pl.BlockSpec((tm, LANE), lambda i: (i, 0)),
                  pl.BlockSpec((tm, LANE), lambda i: (i, 0))],
        out_specs=pl.BlockSpec((tm, n), lambda i: (i, 0)),
        out_shape=jax.ShapeDtypeStruct((t, n), BF16),
        compiler_params=_params("parallel"),
        name="mla_queries",
    )(u, g_cq.reshape(1, cw), wq, g_q_pad.reshape(1, MLA_PAD), cos, sin)


def _mla_kv_kernel(ckv_ref, kr_ref, g_ckv_ref, wkv_ref, g_k_ref, cos_ref, sin_ref, ckvn_ref, k_ref, v_ref, *,
                   heads, normalise):
    ckv = ckv_ref[...]
    if normalise:
        ckv = _head_rms(ckv, g_ckv_ref[...])
    ckvn_ref[...] = ckv
    kv = jnp.dot(ckv.astype(BF16), wkv_ref[...], preferred_element_type=F32)
    lane = lax.broadcasted_iota(jnp.int32, (ckv.shape[0], LANE), 1)
    krope = jnp.where(lane < MLA_ROPE, kr_ref[:, :LANE], 0.0)
    ss_rope = jnp.sum(krope * krope, axis=-1, keepdims=True)
    g_nope, g_rope = g_k_ref[:, :MLA_NOPE], g_k_ref[:, MLA_NOPE:]
    rope = _rope_rotate(krope * g_rope, cos_ref[...], sin_ref[...], MLA_ROPE // 4)
    for h in range(heads):
        nope = kv[:, h * MLA_NOPE:(h + 1) * MLA_NOPE]
        r = lax.rsqrt((jnp.sum(nope * nope, axis=-1, keepdims=True) + ss_rope) / MLA_QK + EPS)
        k_ref[:, h * MLA_PAD:h * MLA_PAD + MLA_NOPE] = (nope * r * g_nope).astype(BF16)
        k_ref[:, h * MLA_PAD + MLA_NOPE:(h + 1) * MLA_PAD] = (rope * r).astype(BF16)
    v_ref[...] = kv[:, heads * MLA_NOPE:].astype(BF16)


def mla_keys_values(ckv_src, ckv_col, kr_src, kr_col, g_ckv, wkv, g_k_pad, cos, sin, *, heads, tm, normalise):
    t = ckv_src.shape[0]
    (c0, cw), (r0, rw) = ckv_col, kr_col
    return pl.pallas_call(
        functools.partial(_mla_kv_kernel, heads=heads, normalise=normalise),
        grid=(t // tm,),
        in_specs=[pl.BlockSpec((tm, cw), lambda i: (i, c0 // cw)),
                  pl.BlockSpec((tm, rw), lambda i: (i, r0 // rw)),
                  pl.BlockSpec((1, cw), lambda i: (0, 0)),
                  pl.BlockSpec((cw, wkv.shape[1]), lambda i: (0, 0)),
                  pl.BlockSpec((1, MLA_PAD), lambda i: (0, 0)),
                  pl.BlockSpec((tm, LANE), lambda i: (i, 0)),
                  pl.BlockSpec((tm, LANE), lambda i: (i, 0))],
        out_specs=[pl.BlockSpec((tm, cw), lambda i: (i, 0)),
                   pl.BlockSpec((tm, heads * MLA_PAD), lambda i: (i, 0)),
                   pl.BlockSpec((tm, heads * MLA_V), lambda i: (i, 0))],
        out_shape=[jax.ShapeDtypeStruct((t, cw), F32), jax.ShapeDtypeStruct((t, heads * MLA_PAD), BF16),
                   jax.ShapeDtypeStruct((t, heads * MLA_V), BF16)],
        compiler_params=_params("parallel"),
        name="mla_keys_values",
    )(ckv_src, kr_src, g_ckv.reshape(1, cw), wkv, g_k_pad.reshape(1, MLA_PAD), cos, sin)


def _gdn_prep_kernel(x_ref, prev_ref, next_ref, w_ref, o_ref, *, heads, tm, n_prompt_tiles, seq_lens):
    i = pl.program_id(0)
    x = x_ref[...]
    row = lax.broadcasted_iota(jnp.int32, (tm, 1), 0)
    in_prompt = i < n_prompt_tiles
    seq_len = jnp.where(in_prompt, seq_lens[0], seq_lens[1])
    first_row = jnp.where(in_prompt, i, i - n_prompt_tiles) * tm
    span = [min(s, tm) for s in seq_lens]
    local = jnp.where(in_prompt, row & (span[0] - 1), row & (span[1] - 1))
    span_t = jnp.where(in_prompt, span[0], span[1])
    tile_starts_seq = lax.rem(first_row, seq_len) == 0
    tile_ends_seq = lax.rem(first_row + tm, seq_len) == 0
    is_start = jnp.logical_and(local == 0, jnp.logical_or(row != 0, tile_starts_seq))
    is_end = jnp.logical_and(local == span_t - 1, jnp.logical_or(row != tm - 1, tile_ends_seq))
    prev = jnp.where(row == 0, prev_ref[7:8, :], pltpu.roll(x, 1, axis=0))
    nxt = jnp.where(row == tm - 1, next_ref[0:1, :], pltpu.roll(x, tm - 1, axis=0))
    prev = jnp.where(is_start, 0.0, prev)
    nxt = jnp.where(is_end, 0.0, nxt)
    conv = prev * w_ref[0:1, :] + x * w_ref[1:2, :] + nxt * w_ref[2:3, :]
    act = conv * jax.nn.sigmoid(conv)
    for h in range(3 * heads):
        hl = slice(h * GDN_DK, (h + 1) * GDN_DK)
        slab = act[:, hl]
        if h < 2 * heads:
            slab = slab * lax.rsqrt(jnp.sum(slab * slab, axis=-1, keepdims=True) + EPS)
            if h < heads:
                slab = slab * GDN_DK ** -0.5
        o_ref[:, hl] = slab


def gdn_prep(u, col, w_conv, *, heads, tm, t_prompt, seq_lens):
    t = u.shape[0]
    c0, cw = col
    assert all(s & (s - 1) == 0 for s in (tm,) + tuple(min(s, tm) for s in seq_lens))
    eight = 8
    cb = c0 // cw
    last8 = t // eight - 1
    return pl.pallas_call(
        functools.partial(_gdn_prep_kernel, heads=heads, tm=tm, n_prompt_tiles=t_prompt // tm, seq_lens=seq_lens),
        grid=(t // tm,),
        in_specs=[pl.BlockSpec((tm, cw), lambda i: (i, cb)),
                  pl.BlockSpec((eight, cw), lambda i: (jnp.maximum(i * (tm // eight) - 1, 0), cb)),
                  pl.BlockSpec((eight, cw), lambda i: (jnp.minimum((i + 1) * (tm // eight), last8), cb)),
                  pl.BlockSpec((3, cw), lambda i: (0, 0))],
        out_specs=pl.BlockSpec((tm, cw), lambda i: (i, 0)),
        out_shape=jax.ShapeDtypeStruct((t, cw), F32),
        compiler_params=_params("parallel"),
        name="gdn_prep",
    )(u, u, u, w_conv)


def _attn_kernel(*refs, n_seg, aliased, group, dq, dv, tk):
    q_ref = refs[0]
    kv_refs = [(refs[1 + 2 * s], refs[2 + 2 * s]) for s in range(n_seg)]
    o_ref, s_ref, mx_ref, l_ref, acc_ref = refs[1 + 2 * n_seg + (1 if aliased else 0):]
    tq = q_ref.shape[0]
    q = jnp.concatenate([q_ref[:, g * dq:(g + 1) * dq] for g in range(group)], axis=0)
    slabs = tk // LANE

    def for_chunks(body):
        base = 0
        for k_ref, v_ref in kv_refs:
            nk = k_ref.shape[0] // tk
            main = nk // ATTN_UNROLL

            def group_body(i, carry, k_ref=k_ref, v_ref=v_ref, base=base):
                for u in range(ATTN_UNROLL):
                    body(k_ref, v_ref, i * ATTN_UNROLL + u, base)
                return carry
            if main:
                lax.fori_loop(0, main, group_body, 0)
            for c in range(main * ATTN_UNROLL, nk):
                body(k_ref, v_ref, c, base)
            base += nk

    mx_ref[...] = jnp.full(mx_ref.shape, -jnp.inf, F32)

    def rows_of(c):
        return pl.ds(c * tk if isinstance(c, int) else pl.multiple_of(c * tk, tk), tk)

    def scores(k_ref, v_ref, c, base):
        s = lax.dot_general(q, k_ref[rows_of(c), :], (((1,), (1,)), ((), ())),
                            preferred_element_type=F32)
        s_ref[base + c] = s
        mx = mx_ref[...]
        for j in range(slabs):
            mx = jnp.maximum(mx, s[:, j * LANE:(j + 1) * LANE])
        mx_ref[...] = mx

    for_chunks(scores)
    m = jnp.broadcast_to(jnp.max(mx_ref[...], axis=-1, keepdims=True), mx_ref.shape)
    l_ref[...] = jnp.zeros_like(l_ref)
    acc_ref[...] = jnp.zeros_like(acc_ref)

    def weighted(k_ref, v_ref, c, base):
        s = s_ref[base + c]
        p = jnp.concatenate([jnp.exp2(s[:, j * LANE:(j + 1) * LANE] - m) for j in range(slabs)], axis=1)
        lsum = l_ref[...]
        for j in range(slabs):
            lsum = lsum + p[:, j * LANE:(j + 1) * LANE]
        l_ref[...] = lsum
        acc_ref[...] += jnp.dot(p.astype(BF16), v_ref[rows_of(c), :], preferred_element_type=F32)

    for_chunks(weighted)
    o = acc_ref[...] / jnp.sum(l_ref[...], axis=-1, keepdims=True)
    for g in range(group):
        o_ref[:, g * dv:(g + 1) * dv] = o[g * tq:(g + 1) * tq].astype(o_ref.dtype)


def attention(q, q_row0, kv_segments, *, batch, lq, kv_heads, group, dq, dv, out_rows, out_row0, prev_out, name):
    tq = _pick(lq, (512, 256, 128)) // (2 if group > 1 else 1)
    tk = _pick(math.gcd(*[length for _, _, _, length in kv_segments]), (256, 128))
    nq = lq // tq
    m_rows = group * tq
    assert q_row0 % tq == 0 and out_row0 % tq == 0
    in_specs = [pl.BlockSpec((tq, group * dq), lambda b, h, i: (q_row0 // tq + b * nq + i, h))]
    operands = [q]
    for k, v, first_row, length in kv_segments:
        assert first_row % length == 0
        in_specs += [pl.BlockSpec((length, dq), lambda b, h, i, f=first_row // length: (f + b, h)),
                     pl.BlockSpec((length, dv), lambda b, h, i, f=first_row // length: (f + b, h))]
        operands += [k, v]
    aliases = {}
    if prev_out is not None:
        in_specs.append(pl.BlockSpec(memory_space=pl.ANY))
        aliases = {len(operands): 0}
        operands.append(prev_out)
    n_chunks = sum(length // tk for _, _, _, length in kv_segments)
    return pl.pallas_call(
        functools.partial(_attn_kernel, n_seg=len(kv_segments), aliased=prev_out is not None, group=group,
                          dq=dq, dv=dv, tk=tk),
        grid=(batch, kv_heads, nq),
        in_specs=in_specs,
        out_specs=pl.BlockSpec((tq, group * dv), lambda b, h, i: (out_row0 // tq + b * nq + i, h)),
        out_shape=jax.ShapeDtypeStruct((out_rows, kv_heads * group * dv), BF16),
        input_output_aliases=aliases,
        scratch_shapes=[pltpu.VMEM((n_chunks, m_rows, tk), F32), pltpu.VMEM((m_rows, LANE), F32),
                        pltpu.VMEM((m_rows, LANE), F32), pltpu.VMEM((m_rows, dv), F32)],
        compiler_params=_params("parallel", "parallel", "parallel"),
        name=name,
    )(*operands)


def _split_hi_lo(a):
    hi = a.astype(BF16).astype(F32)
    return hi, (a - hi).astype(BF16).astype(F32)


def _dot_split(a_parts, b_parts):
    ah, al = a_parts
    bh, bl = b_parts
    lhs = jnp.concatenate([ah, al, ah, al], axis=2).astype(BF16)
    rhs = jnp.concatenate([bh, bh, bl, bl], axis=1).astype(BF16)
    return lax.dot_general(lhs, rhs, (((2,), (1,)), ((0,), (0,))), preferred_element_type=F32)


def _dot_nt(a, b):
    return lax.dot_general(a.astype(BF16), b.astype(BF16), (((2,), (2,)), ((0,), (0,))), preferred_element_type=F32)


def _chunk_masks(backward):
    r = lax.broadcasted_iota(jnp.int32, (GDN_CHUNK, GDN_CHUNK), 0)
    c = lax.broadcasted_iota(jnp.int32, (GDN_CHUNK, GDN_CHUNK), 1)
    incl = (r <= c) if backward else (r >= c)
    strict = (r < c) if backward else (r > c)
    return r, c, incl, strict


def _decay_mask(dcol, r, c, incl):
    drow = jnp.sum(jnp.where(r == c, dcol, 0.0), axis=1, keepdims=True)
    return jnp.where(incl, jnp.exp(jnp.where(incl, dcol - drow, 0.0)), 0.0)


def _unit_tri_inverse(a_mat, r, c):
    eye = (r == c).astype(F32)
    same16 = jnp.right_shift(r, 4) == jnp.right_shift(c, 4)
    same32 = jnp.right_shift(r, 5) == jnp.right_shift(c, 5)
    x = jnp.where(same16, -a_mat, 0.0)
    inv = eye + x
    x_parts = _split_hi_lo(x)
    for _ in range(3):
        x_parts = _split_hi_lo(_dot_split(x_parts, x_parts))
        inv = inv + _dot_split(_split_hi_lo(inv), x_parts)
    for off_blocks in (jnp.logical_and(same32, jnp.logical_not(same16)), jnp.logical_not(same32)):
        off = jnp.where(off_blocks, a_mat, 0.0)
        inv_parts = _split_hi_lo(inv)
        inv = inv - _dot_split(inv_parts, _split_hi_lo(_dot_split(_split_hi_lo(off), inv_parts)))
    return inv


def _head_column(ref, lane, index, n_c):
    col = jnp.sum(jnp.where(lane == index, ref[...], 0.0), axis=1, keepdims=True)
    return col.reshape(n_c, GDN_CHUNK, 1)


def _gdn_solve_kernel(k_ref, v_ref, d_ref, b_ref, uw_ref, *, head_lanes):
    h = pl.program_id(1)
    n_c = k_ref.shape[0] // GDN_CHUNK
    lane = lax.broadcasted_iota(jnp.int32, (1, d_ref.shape[1]), 1)
    kc = k_ref[...].reshape(n_c, GDN_CHUNK, GDN_DK)
    vc = v_ref[...].reshape(n_c, GDN_CHUNK, GDN_DV)
    for direction in range(2):
        r, c, incl, strict = _chunk_masks(direction == 1)
        dcol = _head_column(d_ref, lane, direction * head_lanes + h, n_c)
        bcol = _head_column(b_ref, lane, direction * head_lanes + h, n_c)
        kb = kc * bcol
        a_mat = jnp.where(strict, _dot_nt(kb, kc) * _decay_mask(dcol, r, c, incl), 0.0)
        rhs = jnp.concatenate([vc * bcol, kb * jnp.exp(dcol)], axis=2)
        uw = _dot_split(_split_hi_lo(_unit_tri_inverse(a_mat, r, c)), _split_hi_lo(rhs))
        uw_ref[direction] = uw.reshape(n_c * GDN_CHUNK, GDN_DV + GDN_DK)


def gdn_solve(qkv, dcs, beta, *, heads):
    t = qkv.shape[0]
    rows = _pick(t, (512, 256, 128, 64))
    width = GDN_DV + GDN_DK
    return pl.pallas_call(
        functools.partial(_gdn_solve_kernel, head_lanes=heads),
        grid=(t // rows, heads),
        in_specs=[pl.BlockSpec((rows, GDN_DK), lambda i, h: (i, heads + h)),
                  pl.BlockSpec((rows, GDN_DV), lambda i, h: (i, 2 * heads + h)),
                  pl.BlockSpec((rows, 2 * heads), lambda i, h: (i, 0)),
                  pl.BlockSpec((rows, 2 * heads), lambda i, h: (i, 0))],
        out_specs=pl.BlockSpec((2, rows, width), lambda i, h: (0, i, h)),
        out_shape=jax.ShapeDtypeStruct((2, t, heads * width), F32),
        compiler_params=_params("parallel", "parallel"),
        name="gdn_solve",
    )(qkv, qkv, dcs, beta)


def _gdn_scan_kernel(fwd_blk_ref, bwd_blk_ref, seq_ref, edge_ref, qf_ref, kf_ref, uwf_ref, df_ref,
                     qb_ref, kb_ref, uwb_ref, db_ref, s0_ref, of_ref, ob_ref, sfin_ref, s_ref, *,
                     heads_per_step, head_lanes):
    step = pl.program_id(1)
    hg = pl.program_id(0)
    rows = qf_ref.shape[0]
    n_c = rows // GDN_CHUNK
    lane = lax.broadcasted_iota(jnp.int32, (1, df_ref.shape[1]), 1)

    @pl.when((edge_ref[step] & 1) != 0)
    def _():
        s_ref[...] = s0_ref[...].astype(F32)

    def bmm(a, b):
        return lax.dot_general(a, b, (((2,), (1,)), ((0,), (0,))), preferred_element_type=F32)

    width = GDN_DV + GDN_DK
    streams = ((qf_ref, kf_ref, uwf_ref, df_ref, of_ref), (qb_ref, kb_ref, uwb_ref, db_ref, ob_ref))
    for ci in range(n_c):
        for direction, (q_ref, k_ref, uw_ref, d_ref, o_ref) in enumerate(streams):
            backward = direction == 1
            r, c, incl, _ = _chunk_masks(backward)
            pos = (n_c - 1 - ci) if backward else ci
            sl = slice(pos * GDN_CHUNK, (pos + 1) * GDN_CHUNK)
            heads = range(heads_per_step)
            dcol = jnp.stack([jnp.sum(jnp.where(lane == direction * head_lanes + hg * heads_per_step + hh,
                                                d_ref[sl, :], 0.0), axis=1, keepdims=True) for hh in heads])
            qc = jnp.stack([q_ref[sl, hh * GDN_DK:(hh + 1) * GDN_DK] for hh in heads])
            kc = jnp.stack([k_ref[sl, hh * GDN_DK:(hh + 1) * GDN_DK] for hh in heads])
            u = jnp.stack([uw_ref[sl, hh * width:hh * width + GDN_DV] for hh in heads])
            w = jnp.stack([uw_ref[sl, hh * width + GDN_DV:(hh + 1) * width] for hh in heads])
            attn = _dot_nt(qc, kc) * _decay_mask(dcol, r, c, incl)
            d_last = dcol[:, 0:1, :] if backward else dcol[:, GDN_CHUNK - 1:GDN_CHUNK, :]
            q_dec = (qc * jnp.exp(dcol)).astype(BF16)
            k_dec = (kc * jnp.exp(d_last - dcol)).astype(BF16)
            s = s_ref[direction]
            s_b = s.astype(BF16)
            v_new = u - bmm(w.astype(BF16), s_b)
            v_b = v_new.astype(BF16)
            o = bmm(q_dec, s_b) + bmm(attn.astype(BF16), v_b)
            for hh in heads:
                o_ref[sl, hh * GDN_DV:(hh + 1) * GDN_DV] = o[hh]
            s_ref[direction] = s * jnp.exp(d_last) + lax.dot_general(
                k_dec, v_b, (((1,), (1,)), ((0,), (0,))), preferred_element_type=F32)

    @pl.when((edge_ref[step] & 2) != 0)
    def _():
        sfin_ref[...] = s_ref[...]


def gdn_scan(qkv, uw, dcs, s0, *, sequences, heads):
    t = qkv.shape[0]
    rows = _pick(math.gcd(*[length for _, length in sequences]), (256, 128, 64))
    hps = _pick(heads, (4, 2, 1))
    hgroups = heads // hps
    width = GDN_DV + GDN_DK
    fwd_blk, bwd_blk, seq_id, edge = [], [], [], []
    for s, (first_row, length) in enumerate(sequences):
        assert first_row % rows == 0 and length % rows == 0
        n_l = length // rows
        for i in range(n_l):
            fwd_blk.append(first_row // rows + i)
            bwd_blk.append(first_row // rows + n_l - 1 - i)
            seq_id.append(s)
            edge.append((1 if i == 0 else 0) | (2 if i == n_l - 1 else 0))
    assert sorted(fwd_blk) == list(range(t // rows))
    tables = [jnp.asarray(np.asarray(v, np.int32)) for v in (fwd_blk, bwd_blk, seq_id, edge)]

    def stream_specs(table, direction):
        return [pl.BlockSpec((rows, hps * GDN_DK), lambda g, s, fb, bb, sq, ed: ((fb, bb)[table][s], g)),
                pl.BlockSpec((rows, hps * GDN_DK), lambda g, s, fb, bb, sq, ed: ((fb, bb)[table][s], hgroups + g)),
                pl.BlockSpec((None, rows, hps * width), lambda g, s, fb, bb, sq, ed: (direction, (fb, bb)[table][s], g)),
                pl.BlockSpec((rows, 2 * heads), lambda g, s, fb, bb, sq, ed: ((fb, bb)[table][s], 0))]

    state_spec = pl.BlockSpec((None, 2, hps, GDN_DK, GDN_DV), lambda g, s, fb, bb, sq, ed: (sq[s], 0, g, 0, 0))
    grid_spec = pltpu.PrefetchScalarGridSpec(
        num_scalar_prefetch=4,
        grid=(hgroups, len(fwd_blk)),
        in_specs=stream_specs(0, 0) + stream_specs(1, 1) + [state_spec],
        out_specs=[pl.BlockSpec((rows, hps * GDN_DV), lambda g, s, fb, bb, sq, ed: (fb[s], g)),
                   pl.BlockSpec((rows, hps * GDN_DV), lambda g, s, fb, bb, sq, ed: (bb[s], g)),
                   state_spec],
        scratch_shapes=[pltpu.VMEM((2, hps, GDN_DK, GDN_DV), F32)],
    )
    return pl.pallas_call(
        functools.partial(_gdn_scan_kernel, heads_per_step=hps, head_lanes=heads),
        grid_spec=grid_spec,
        out_shape=[jax.ShapeDtypeStruct((t, heads * GDN_DV), F32),
                   jax.ShapeDtypeStruct((t, heads * GDN_DV), F32),
                   jax.ShapeDtypeStruct((len(sequences), 2, heads, GDN_DK, GDN_DV), F32)],
        compiler_params=_params("parallel", "arbitrary"),
        name="gdn_scan",
    )(*tables, qkv, qkv, uw, dcs, qkv, qkv, uw, dcs, s0)


def _row_gather(idx_ref, src_hbm, dst_vmem, sem, n_rows):
    def start():
        def body(r, carry):
            pltpu.make_async_copy(src_hbm.at[pl.ds(idx_ref[0, r], 1), :], dst_vmem.at[pl.ds(r, 1), :], sem).start()
            return carry
        lax.fori_loop(0, n_rows, body, 0, unroll=GATHER_UNROLL)

    def wait():
        def body(r, carry):
            pltpu.make_async_copy(src_hbm.at[pl.ds(0, 1), :], dst_vmem.at[pl.ds(0, 1), :], sem).wait()
            return carry
        lax.fori_loop(0, n_rows, body, 0, unroll=GATHER_UNROLL)
    return start, wait


def _moe_kernel(be_ref, nu_ref, tok_ref, tok_next_ref, h_hbm, wg_ref, wu_ref, wd_ref, o_ref,
                xbuf_ref, xlo_ref, xhi_ref, acc_ref, sem):
    i = pl.program_id(0)
    f = pl.program_id(1)
    n_used = nu_ref[0]
    tm, half = xlo_ref.shape

    @pl.when(i < n_used)
    def _():
        @pl.when(f == 0)
        def _():
            for slot in range(2):
                start_this, wait_this = _row_gather(tok_ref, h_hbm, xbuf_ref.at[slot], sem.at[slot], tm)
                start_next, _ = _row_gather(tok_next_ref, h_hbm, xbuf_ref.at[1 - slot], sem.at[1 - slot], tm)

                @pl.when(lax.rem(i, 2) == slot)
                def _():
                    @pl.when(i == 0)
                    def _():
                        start_this()
                    wait_this()

                    @pl.when(i + 1 < n_used)
                    def _():
                        start_next()
                    lo, hi = _unpack_bf16_pair(xbuf_ref[slot])
                    xlo_ref[...] = lo.astype(BF16)
                    xhi_ref[...] = hi.astype(BF16)
            acc_ref[...] = jnp.zeros_like(acc_ref)

        def x_dot(w_ref):
            return (jnp.dot(xlo_ref[...], w_ref[:half, :].astype(BF16), preferred_element_type=F32)
                    + jnp.dot(xhi_ref[...], w_ref[half:, :].astype(BF16), preferred_element_type=F32))

        gate = x_dot(wg_ref)
        up = x_dot(wu_ref)
        hmid = (gate * jax.nn.sigmoid(gate) * up).astype(BF16)
        acc_ref[...] += jnp.dot(hmid, wd_ref[...].astype(BF16), preferred_element_type=F32)

        @pl.when(f == pl.num_programs(1) - 1)
        def _():
            o_ref[...] = _pack_bf16_pair(acc_ref[:, :half], acc_ref[:, half:])

    @pl.when(jnp.logical_and(i >= n_used, f == 0))
    def _():
        o_ref[...] = jnp.zeros_like(o_ref)


def moe_experts(h_packed, slot_tok, block_e, n_used, w_gate, w_up, w_down, *, tm, layer):
    d = 2 * h_packed.shape[1]
    p = slot_tok.shape[0]
    f_dim = w_gate.shape[3]
    tf = _pick(f_dim, (256, 128))
    n_blocks = p // tm
    last_f = f_dim // tf - 1
    tok_blocks = slot_tok.reshape(n_blocks, 1, tm)

    def blk(i, nu):
        return jnp.minimum(i, nu[0] - 1)

    def f_blk(i, f, nu):
        return jnp.where(i < nu[0], f, last_f)

    grid_spec = pltpu.PrefetchScalarGridSpec(
        num_scalar_prefetch=2,
        grid=(n_blocks, f_dim // tf),
        in_specs=[pl.BlockSpec((None, 1, tm), lambda i, f, be, nu: (blk(i, nu), 0, 0), memory_space=pltpu.SMEM),
                  pl.BlockSpec((None, 1, tm), lambda i, f, be, nu: (blk(i + 1, nu), 0, 0), memory_space=pltpu.SMEM),
                  pl.BlockSpec(memory_space=pl.ANY),
                  pl.BlockSpec((None, None, d, tf), lambda i, f, be, nu: (layer, be[blk(i, nu)], 0, f_blk(i, f, nu))),
                  pl.BlockSpec((None, None, d, tf), lambda i, f, be, nu: (layer, be[blk(i, nu)], 0, f_blk(i, f, nu))),
                  pl.BlockSpec((None, None, tf, d), lambda i, f, be, nu: (layer, be[blk(i, nu)], f_blk(i, f, nu), 0))],
        out_specs=pl.BlockSpec((tm, d // 2), lambda i, f, be, nu: (i, 0)),
        scratch_shapes=[pltpu.VMEM((2, tm, d // 2), jnp.uint32), pltpu.VMEM((tm, d // 2), BF16),
                        pltpu.VMEM((tm, d // 2), BF16), pltpu.VMEM((tm, d), F32),
                        pltpu.SemaphoreType.DMA((2,))],
    )
    return pl.pallas_call(
        _moe_kernel,
        grid_spec=grid_spec,
        out_shape=jax.ShapeDtypeStruct((p, d // 2), jnp.uint32),
        compiler_params=_params("arbitrary", "arbitrary"),
        name="moe_experts",
    )(block_e, n_used, tok_blocks, tok_blocks, h_packed, w_gate, w_up, w_down)


def _moe_combine_kernel(dest_ref, dest_next_ref, y_hbm, x_ref, gate_ref, mod_ref, o_ref, rows_ref, sem, *, gate_idx):
    i = pl.program_id(0)
    tc, d = x_ref.shape
    half = d // 2
    for slot in range(2):
        start_this, wait_this = _row_gather(dest_ref, y_hbm, rows_ref.at[slot], sem.at[slot], 2 * tc)
        start_next, _ = _row_gather(dest_next_ref, y_hbm, rows_ref.at[1 - slot], sem.at[1 - slot], 2 * tc)

        @pl.when(lax.rem(i, 2) == slot)
        def _():
            @pl.when(i == 0)
            def _():
                start_this()
            wait_this()

            @pl.when(i + 1 < pl.num_programs(0))
            def _():
                start_next()
            lo0, hi0 = _unpack_bf16_pair(rows_ref[slot, :tc, :])
            lo1, hi1 = _unpack_bf16_pair(rows_ref[slot, tc:, :])
            g0 = gate_ref[:, 0:1]
            g1 = gate_ref[:, 1:2]
            o_ref[:, :half] = x_ref[:, :half] + mod_ref[gate_idx:gate_idx + 1, :half] * (lo0 * g0 + lo1 * g1)
            o_ref[:, half:] = x_ref[:, half:] + mod_ref[gate_idx:gate_idx + 1, half:] * (hi0 * g0 + hi1 * g1)


def moe_combine_residual(y_packed, dest, gates, x, mod, *, gate_idx, tc, row_fn):
    t, d = x.shape
    n_tiles = t // tc
    dest_tiles = dest.reshape(n_tiles, tc, TOP_K).transpose(0, 2, 1).reshape(n_tiles, 1, TOP_K * tc)
    return pl.pallas_call(
        functools.partial(_moe_combine_kernel, gate_idx=gate_idx),
        grid=(n_tiles,),
        in_specs=[pl.BlockSpec((None, 1, TOP_K * tc), lambda i: (i, 0, 0), memory_space=pltpu.SMEM),
                  pl.BlockSpec((None, 1, TOP_K * tc), lambda i: (jnp.minimum(i + 1, n_tiles - 1), 0, 0),
                               memory_space=pltpu.SMEM),
                  pl.BlockSpec(memory_space=pl.ANY),
                  pl.BlockSpec((tc, d), lambda i: (i, 0)),
                  pl.BlockSpec((tc, gates.shape[1]), lambda i: (i, 0)),
                  pl.BlockSpec((None, mod.shape[1], d), lambda i: (row_fn(i), 0, 0))],
        out_specs=pl.BlockSpec((tc, d), lambda i: (i, 0)),
        out_shape=jax.ShapeDtypeStruct((t, d), F32),
        scratch_shapes=[pltpu.VMEM((2, TOP_K * tc, d // 2), jnp.uint32), pltpu.SemaphoreType.DMA((2,))],
        compiler_params=_params("arbitrary"),
        name="moe_combine",
    )(dest_tiles, dest_tiles, y_packed, x, gates, mod)


def _rms(x, g):
    return x * lax.rsqrt(jnp.mean(x * x, axis=-1, keepdims=True) + EPS) * g


def _rope_tables(t_prompt, dec_batch, dec_seq, width):
    a = width // 2
    half = a // 2
    freqs = ROPE_THETA ** (-np.arange(half, dtype=np.float32) / half)
    t = np.arange(dec_seq)
    row, col = t // GRID_W, t % GRID_W
    ang_r = row[:, None].astype(np.float32) * freqs[None, :]
    ang_c = col[:, None].astype(np.float32) * freqs[None, :]
    cos = np.concatenate([np.cos(ang_r), np.cos(ang_r), np.cos(ang_c), np.cos(ang_c)], axis=-1)
    sin = np.concatenate([-np.sin(ang_r), np.sin(ang_r), -np.sin(ang_c), np.sin(ang_c)], axis=-1)
    cos = np.concatenate([np.ones((t_prompt, width), np.float32), np.tile(cos, (dec_batch, 1))], axis=0)
    sin = np.concatenate([np.zeros((t_prompt, width), np.float32), np.tile(sin, (dec_batch, 1))], axis=0)
    return jnp.asarray(cos), jnp.asarray(sin)


def _gdn_gates(a, b, a_log, dt_bias, heads):
    t = a.shape[0]
    g = -jnp.exp(a_log.reshape(2 * heads)) * jax.nn.softplus(a + dt_bias.reshape(2 * heads))
    gc = g.reshape(t // GDN_CHUNK, GDN_CHUNK, 2 * heads)
    prefix = jnp.cumsum(gc[..., :heads], axis=1)
    suffix = jnp.flip(jnp.cumsum(jnp.flip(gc[..., heads:], axis=1), axis=1), axis=1)
    dcs = jnp.concatenate([prefix, suffix], axis=-1).reshape(t, 2 * heads)
    return dcs, jax.nn.sigmoid(b)


def _dispatch_plan(experts, n_experts, tm):
    t, k = experts.shape
    a = t * k
    n_blocks = a // tm + n_experts
    flat_e = experts.reshape(-1)
    onehot = (flat_e[:, None] == jnp.arange(n_experts, dtype=jnp.int32)[None, :]).astype(jnp.int32)
    csum = jnp.cumsum(onehot, axis=0)
    pos = jnp.sum((csum - onehot) * onehot, axis=1)
    counts = csum[-1]
    padded = (counts + tm - 1) // tm * tm
    pend = jnp.cumsum(padded)
    pstart = pend - padded
    dest = pstart[flat_e] + pos
    flat_tok = jnp.arange(a, dtype=jnp.int32) // k
    slot_tok = jnp.zeros((n_blocks * tm,), jnp.int32).at[dest].set(flat_tok)
    block_e = jnp.minimum(jnp.searchsorted(pend, jnp.arange(n_blocks, dtype=jnp.int32) * tm, side='right'),
                          n_experts - 1).astype(jnp.int32)
    n_used = (pend[-1] // tm).astype(jnp.int32).reshape(1)
    return dest.reshape(t, k), slot_tok, block_e, n_used


def kernel(x_prompt, x_sample, cache_mla_ckv, cache_mla_krope, cache_gqa_k, cache_gqa_v, state_gdn, c, c_ctx, w_mod, b_mod, g_norm_mix, g_norm_ffn, w_in, g_cq, w_q_up, g_ckv, w_kv_up, g_q_mla, g_k_mla, g_q_gqa, g_k_gqa, w_conv, a_log, dt_bias, g_gdn_out, w_out, w_router_group, b_router_group, w_router_expert, b_router_expert, w_gate, w_up, w_down):
    batch, seq, d = x_prompt.shape
    dec_batch, dec_seq, _ = x_sample.shape
    depth = w_in.shape[0]
    past = cache_mla_ckv.shape[2]
    q_lora = w_q_up.shape[1]
    kv_lora = w_kv_up.shape[1]
    mla_heads = w_q_up.shape[2] // MLA_QK
    kv_heads = cache_gqa_k.shape[3]
    gdn_heads = state_gdn.shape[3]
    gqa_heads = (w_out.shape[1] - mla_heads * MLA_V - gdn_heads * GDN_DV) // HEAD_DIM
    group = gqa_heads // kv_heads
    n_groups = w_router_group.shape[2]
    n_experts = w_gate.shape[1]
    gdn_qkv = gdn_heads * (2 * GDN_DK + GDN_DV)
    gdn_width = gdn_heads * GDN_DV

    t_p = batch * seq
    t_s = dec_batch * dec_seq
    t_all = t_p + t_s
    tm = _pick(math.gcd(t_p, dec_seq), (512, 256, 128))
    row_fn = _mod_row_fn(tm, t_p, dec_seq)
    moe_tm = _pick(t_all * TOP_K, (512, 256, 128))
    combine_tc = _pick(math.gcd(t_p, dec_seq), (256, 128))

    src_sizes = (q_lora, kv_lora, MLA_ROPE, gqa_heads * HEAD_DIM, kv_heads * HEAD_DIM, kv_heads * HEAD_DIM,
                 gdn_qkv, gdn_width, 2 * gdn_heads, 2 * gdn_heads)
    src_off = np.concatenate([[0], np.cumsum(src_sizes)])
    src = dict(zip(('cq', 'ckv', 'krope', 'gq', 'gk', 'gv', 'qkv', 'z', 'a', 'b'),
                   [(int(src_off[j]), int(src_off[j + 1])) for j in range(len(src_sizes))]))
    tail_w = -(-(MLA_ROPE + 4 * gdn_heads) // LANE) * LANE
    cols, pieces, pos = {}, [], 0
    for name, width in (('qkv', gdn_qkv), ('z', gdn_width), ('ckv', kv_lora), ('gq', gqa_heads * HEAD_DIM),
                        ('cq', q_lora), ('tail', tail_w), ('gk', kv_heads * HEAD_DIM), ('gv', kv_heads * HEAD_DIM)):
        start = -(-pos // width) * width
        assert width % LANE == 0
        if start > pos:
            pieces.append(('pad', start - pos))
        pieces.append((name, width))
        cols[name] = (start, width)
        pos = start + width
    n_in = pos
    tail0 = cols['tail'][0]

    def build_w_in(w):
        parts = []
        for name, width in pieces:
            if name == 'pad':
                parts.append(jnp.zeros((d, width), F32))
            elif name == 'tail':
                parts += [w[:, src[s][0]:src[s][1]] for s in ('krope', 'a', 'b')]
                parts.append(jnp.zeros((d, width - MLA_ROPE - 4 * gdn_heads), F32))
            else:
                parts.append(w[:, src[name][0]:src[name][1]])
        return jnp.concatenate(parts, axis=1).astype(BF16)

    def padded_rope_tables(width):
        cos, sin = _rope_tables(t_p, dec_batch, dec_seq, width)
        return (jnp.pad(cos, ((0, 0), (0, LANE - width)), constant_values=1.0),
                jnp.pad(sin, ((0, 0), (0, LANE - width))))

    cos_g, sin_g = padded_rope_tables(HEAD_DIM)
    cos_m, sin_m = padded_rope_tables(MLA_ROPE)
    n_ctx = dec_batch * past
    cos_id, sin_id = jnp.ones((n_ctx, LANE), F32), jnp.zeros((n_ctx, LANE), F32)
    sequences = ([(b * seq, seq) for b in range(batch)]
                 + [(t_p + b * dec_seq, dec_seq) for b in range(dec_batch)])

    cond8 = jnp.zeros((8, d), F32).at[0].set(c_ctx).at[1:1 + dec_batch].set(c)

    x = jnp.concatenate([x_prompt.reshape(t_p, d), x_sample.reshape(t_s, d)], axis=0)
    new_ckv, new_krope, new_k, new_v, new_s = [], [], [], [], []
    q_scale_m = MLA_QK ** -0.5 * LOG2E
    q_scale_g = HEAD_DIM ** -0.5 * LOG2E

    for l in range(depth):
        mod = modulation(cond8, w_mod, b_mod, l).reshape(8, 6, d)

        u = norm_matmul(x, g_norm_mix[l], mod, build_w_in(w_in[l]), sh_idx=0, sc_idx=1, tm=tm, row_fn=row_fn)

        wq = w_q_up[l].reshape(q_lora, mla_heads, MLA_QK)
        wq = jnp.pad(wq, ((0, 0), (0, 0), (0, MLA_PAD - MLA_QK))).reshape(q_lora, mla_heads * MLA_PAD).astype(BF16)
        q_m = mla_queries(u, cols, g_cq[l], wq, jnp.pad(g_q_mla[l], (0, MLA_PAD - MLA_QK)), cos_m, sin_m,
                          tm=tm, q_scale=q_scale_m)
        wkv = w_kv_up[l].reshape(kv_lora, mla_heads, MLA_NOPE + MLA_V)
        wkv = jnp.concatenate([wkv[..., :MLA_NOPE].reshape(kv_lora, -1), wkv[..., MLA_NOPE:].reshape(kv_lora, -1)],
                              axis=1).astype(BF16)
        g_k_pad = jnp.pad(g_k_mla[l], (0, MLA_PAD - MLA_QK))
        ckv_n, k_m, v_m = mla_keys_values(u, cols['ckv'], u, cols['tail'], g_ckv[l], wkv, g_k_pad, cos_m, sin_m,
                                          heads=mla_heads, tm=tm, normalise=True)
        ctx_krope = jnp.pad(cache_mla_krope[:, l].reshape(n_ctx, MLA_ROPE), ((0, 0), (0, LANE - MLA_ROPE)))
        _, k_m_ctx, v_m_ctx = mla_keys_values(cache_mla_ckv[:, l].reshape(n_ctx, kv_lora), (0, kv_lora), ctx_krope,
                                              (0, LANE), g_ckv[l], wkv, g_k_pad, cos_id, sin_id, heads=mla_heads,
                                              tm=_pick(n_ctx, (512, 256, 128)), normalise=False)

        mla_common = dict(kv_heads=mla_heads, group=1, dq=MLA_PAD, dv=MLA_V, out_rows=t_all)
        o_m = attention(q_m, 0, [(k_m, v_m, 0, seq)], batch=batch, lq=seq, out_row0=0,
                        prev_out=jnp.zeros((t_all, mla_heads * MLA_V), BF16), name="mla_attn_ctx", **mla_common)
        o_m = attention(q_m, t_p, [(k_m_ctx, v_m_ctx, 0, past), (k_m, v_m, t_p, dec_seq)], batch=dec_batch,
                        lq=dec_seq, out_row0=t_p, prev_out=o_m, name="mla_attn_lat", **mla_common)

        q_g, k_g_rot, k_g, v_g_b = gqa_prep(u, cols, g_q_gqa[l], g_k_gqa[l], cos_g, sin_g, tm=tm, q_scale=q_scale_g)
        ctx_k = cache_gqa_k[:, l].reshape(n_ctx, kv_heads * HEAD_DIM).astype(BF16)
        ctx_v = cache_gqa_v[:, l].reshape(n_ctx, kv_heads * HEAD_DIM).astype(BF16)
        gqa_common = dict(kv_heads=kv_heads, group=group, dq=HEAD_DIM, dv=HEAD_DIM, out_rows=t_all)
        o_g = attention(q_g, 0, [(k_g_rot, v_g_b, 0, seq)], batch=batch, lq=seq, out_row0=0,
                        prev_out=jnp.zeros((t_all, gqa_heads * HEAD_DIM), BF16), name="gqa_attn_ctx", **gqa_common)
        o_g = attention(q_g, t_p, [(ctx_k, ctx_v, 0, past), (k_g_rot, v_g_b, t_p, dec_seq)], batch=dec_batch,
                        lq=dec_seq, out_row0=t_p, prev_out=o_g, name="gqa_attn_lat", **gqa_common)

        qkv_c = gdn_prep(u, cols['qkv'], w_conv[l], heads=gdn_heads, tm=tm, t_prompt=t_p, seq_lens=(seq, dec_seq))
        tail = u[:, tail0:tail0 + MLA_ROPE + 4 * gdn_heads]
        dcs, beta = _gdn_gates(tail[:, MLA_ROPE:MLA_ROPE + 2 * gdn_heads], tail[:, MLA_ROPE + 2 * gdn_heads:],
                               a_log[l], dt_bias[l], gdn_heads)
        uw = gdn_solve(qkv_c, dcs, beta, heads=gdn_heads)
        s0 = jnp.concatenate([jnp.zeros((batch, 2, gdn_heads, GDN_DK, GDN_DV), F32), state_gdn[:, l]], axis=0)
        o_fwd, o_bwd, s_fin = gdn_scan(qkv_c, uw, dcs, s0, sequences=sequences, heads=gdn_heads)

        new_ckv.append(ckv_n[:t_p].reshape(batch, seq, kv_lora))
        new_krope.append(tail[:t_p, :MLA_ROPE].reshape(batch, seq, MLA_ROPE))
        new_k.append(k_g[:t_p].reshape(batch, seq, kv_heads, HEAD_DIM))
        new_v.append(u[:t_p, cols['gv'][0]:cols['gv'][0] + cols['gv'][1]].reshape(batch, seq, kv_heads, HEAD_DIM))
        new_s.append(s_fin[:batch])

        x = out_proj_residual(o_m, o_g, o_fwd, o_bwd, u, cols['z'], g_gdn_out[l], w_out[l].astype(BF16), x, mod,
                              gate_idx=2, tm=tm, row_fn=row_fn)

        w_router = jnp.concatenate([w_router_group[l], w_router_expert[l],
                                    jnp.zeros((d, LANE - n_groups - n_experts), F32)], axis=1)
        b_router = jnp.concatenate([b_router_group[l], b_router_expert[l],
                                    jnp.zeros((LANE - n_groups - n_experts,), F32)]).reshape(1, LANE)
        h2, gates, experts = ffn_norm_router(x, g_norm_ffn[l], mod, w_router, b_router, sh_idx=3, sc_idx=4, tm=tm,
                                             row_fn=row_fn, n_groups=n_groups, epg=n_experts // n_groups)
        dest, slot_tok, block_e, n_used = _dispatch_plan(experts[:, :TOP_K], n_experts, moe_tm)
        yb = moe_experts(h2, slot_tok, block_e, n_used, w_gate, w_up, w_down, tm=moe_tm, layer=l)
        x = moe_combine_residual(yb, dest, gates, x, mod, gate_idx=5, tc=combine_tc,
                                 row_fn=_mod_row_fn(combine_tc, t_p, dec_seq))

    return (x[:t_p].reshape(batch, seq, d), x[t_p:].reshape(dec_batch, dec_seq, d),
            jnp.stack(new_ckv, axis=1), jnp.stack(new_krope, axis=1), jnp.stack(new_k, axis=1),
            jnp.stack(new_v, axis=1), jnp.stack(new_s, axis=1))
```

```python
import functools
import math

import numpy as np
import jax
import jax.numpy as jnp
from jax import lax
from jax.experimental import pallas as pl
from jax.experimental.pallas import tpu as pltpu

F32 = jnp.float32
BF16 = jnp.bfloat16

EPS = 1e-6
ROPE_THETA = 10000.0
GRID_W = 64
HEAD_DIM = 128
MLA_NOPE = 128
MLA_ROPE = 64
MLA_V = 128
MLA_QK = MLA_NOPE + MLA_ROPE
MLA_PAD = 256
GDN_DK = 128
GDN_DV = 128
GDN_CHUNK = 64
TOP_K = 2
LANE = 128
VMEM_LIMIT_BYTES = 56 * 1024 * 1024
LOG2E = 1.4426950408889634
ATTN_UNROLL = 8
GATHER_UNROLL = 8


def _params(*sem):
    return pltpu.CompilerParams(dimension_semantics=sem, vmem_limit_bytes=VMEM_LIMIT_BYTES)


def _pick(n, prefs):
    for p in prefs:
        if n % p == 0:
            return p
    return n


def _mod_kernel(c_ref, w_ref, b_ref, o_ref):
    c = c_ref[...]
    s = (c * jax.nn.sigmoid(c)).astype(BF16)
    o_ref[...] = jnp.dot(s, w_ref[...].astype(BF16), preferred_element_type=F32) + b_ref[...]


def modulation(cond8, w_mod, b_mod, layer):
    rows, d = cond8.shape
    depth, _, n = w_mod.shape
    tn = _pick(n, (512, 256, 128))
    return pl.pallas_call(
        _mod_kernel,
        grid=(n // tn,),
        in_specs=[pl.BlockSpec((rows, d), lambda j: (0, 0)),
                  pl.BlockSpec((None, d, tn), lambda j: (layer, 0, j)),
                  pl.BlockSpec((None, 1, tn), lambda j: (layer, 0, j))],
        out_specs=pl.BlockSpec((rows, tn), lambda j: (0, j)),
        out_shape=jax.ShapeDtypeStruct((rows, n), F32),
        compiler_params=_params("parallel"),
        name="modulation",
    )(cond8, w_mod, b_mod.reshape(depth, 1, n))


def _mod_row_fn(tm, t_prompt, dec_seq):
    n_p = t_prompt // tm
    per_b = dec_seq // tm

    def row(i):
        return jnp.where(i < n_p, 0, 1 + (i - n_p) // per_b)
    return row


def _modulated_norm(x, g, sc, sh):
    ms = jnp.mean(x * x, axis=-1, keepdims=True)
    return (x * lax.rsqrt(ms + EPS) * g) * (1.0 + sc) + sh


def _norm_matmul_kernel(x_ref, g_ref, mod_ref, w_ref, o_ref, h_ref, *, sh_idx, sc_idx):
    @pl.when(pl.program_id(1) == 0)
    def _():
        h = _modulated_norm(x_ref[...], g_ref[...], mod_ref[sc_idx:sc_idx + 1, :],
                            mod_ref[sh_idx:sh_idx + 1, :])
        h_ref[...] = h.astype(BF16)
    o_ref[...] = jnp.dot(h_ref[...], w_ref[...], preferred_element_type=F32).astype(o_ref.dtype)


def norm_matmul(x, g, mod, w, *, sh_idx, sc_idx, tm, row_fn, out_dtype=F32):
    t, d = x.shape
    n = w.shape[1]
    tn = _pick(n, (1152, 1024, 512, 384, 256, 128))
    return pl.pallas_call(
        functools.partial(_norm_matmul_kernel, sh_idx=sh_idx, sc_idx=sc_idx),
        grid=(t // tm, n // tn),
        in_specs=[pl.BlockSpec((tm, d), lambda i, j: (i, 0)),
                  pl.BlockSpec((1, d), lambda i, j: (0, 0)),
                  pl.BlockSpec((None, mod.shape[1], d), lambda i, j: (row_fn(i), 0, 0)),
                  pl.BlockSpec((d, tn), lambda i, j: (0, j))],
        out_specs=pl.BlockSpec((tm, tn), lambda i, j: (i, j)),
        out_shape=jax.ShapeDtypeStruct((t, n), out_dtype),
        scratch_shapes=[pltpu.VMEM((tm, d), BF16)],
        compiler_params=_params("parallel", "arbitrary"),
        name="norm_in_proj",
    )(x, g.reshape(1, d), mod, w)


def _out_proj_kernel(om_ref, og_ref, of_ref, ob_ref, z_ref, gd_ref, wm_ref, wg_ref, wd_ref, x_ref, mod_ref,
                     o_ref, od_ref, *, gate_idx):
    @pl.when(pl.program_id(1) == 0)
    def _():
        for h in range(of_ref.shape[1] // GDN_DV):
            hl = slice(h * GDN_DV, (h + 1) * GDN_DV)
            z = z_ref[:, hl]
            od_ref[:, hl] = (_head_rms(of_ref[:, hl] + ob_ref[:, hl], gd_ref[...])
                             * (z * jax.nn.sigmoid(z))).astype(BF16)
    acc = jnp.dot(om_ref[...], wm_ref[...], preferred_element_type=F32)
    acc += jnp.dot(og_ref[...], wg_ref[...], preferred_element_type=F32)
    acc += jnp.dot(od_ref[...], wd_ref[...], preferred_element_type=F32)
    o_ref[...] = x_ref[...] + mod_ref[gate_idx:gate_idx + 1, :] * acc


def out_proj_residual(o_m, o_g, o_fwd, o_bwd, u, z_col, g_gdn, w_out, x, mod, *, gate_idx, tm, row_fn):
    t, d = x.shape
    km, kg, kd = o_m.shape[1], o_g.shape[1], o_fwd.shape[1]
    z0, zw = z_col
    assert km == kg and (km + kg) % kd == 0 and zw == kd
    tn = _pick(d, (1024, 512, 256, 128))
    return pl.pallas_call(
        functools.partial(_out_proj_kernel, gate_idx=gate_idx),
        grid=(t // tm, d // tn),
        in_specs=[pl.BlockSpec((tm, km), lambda i, j: (i, 0)),
                  pl.BlockSpec((tm, kg), lambda i, j: (i, 0)),
                  pl.BlockSpec((tm, kd), lambda i, j: (i, 0)),
                  pl.BlockSpec((tm, kd), lambda i, j: (i, 0)),
                  pl.BlockSpec((tm, zw), lambda i, j: (i, z0 // zw)),
                  pl.BlockSpec((1, GDN_DV), lambda i, j: (0, 0)),
                  pl.BlockSpec((km, tn), lambda i, j: (0, j)),
                  pl.BlockSpec((kg, tn), lambda i, j: (1, j)),
                  pl.BlockSpec((kd, tn), lambda i, j: ((km + kg) // kd, j)),
                  pl.BlockSpec((tm, tn), lambda i, j: (i, j)),
                  pl.BlockSpec((None, mod.shape[1], tn), lambda i, j: (row_fn(i), 0, j))],
        out_specs=pl.BlockSpec((tm, tn), lambda i, j: (i, j)),
        out_shape=jax.ShapeDtypeStruct((t, d), F32),
        scratch_shapes=[pltpu.VMEM((tm, kd), BF16)],
        compiler_params=_params("parallel", "arbitrary"),
        name="out_proj",
    )(o_m, o_g, o_fwd, o_bwd, u, g_gdn.reshape(1, GDN_DV), w_out, w_out, w_out, x, mod)


def _pack_bf16_pair(lo, hi):
    lo_bits = lax.bitcast_convert_type(lo.astype(BF16).astype(F32), jnp.uint32)
    hi_bits = lax.bitcast_convert_type(hi.astype(BF16).astype(F32), jnp.uint32)
    return jnp.right_shift(lo_bits, jnp.uint32(16)) | (hi_bits & jnp.uint32(0xFFFF0000))


def _unpack_bf16_pair(word):
    lo = lax.bitcast_convert_type(jnp.left_shift(word, jnp.uint32(16)), F32)
    hi = lax.bitcast_convert_type(word & jnp.uint32(0xFFFF0000), F32)
    return lo, hi


def _ffn_norm_router_kernel(x_ref, g_ref, mod_ref, wr_ref, br_ref, h_ref, gate_ref, exp_ref, *,
                            sh_idx, sc_idx, n_groups, epg):
    h = _modulated_norm(x_ref[...], g_ref[...], mod_ref[sc_idx:sc_idx + 1, :],
                        mod_ref[sh_idx:sh_idx + 1, :])
    half = h.shape[1] // 2
    h_ref[...] = _pack_bf16_pair(h[:, :half], h[:, half:])
    h_hi = h.astype(BF16)
    h_lo = (h - h_hi.astype(F32)).astype(BF16)
    w = wr_ref[...]
    w_hi = w.astype(BF16)
    w_lo = (w - w_hi.astype(F32)).astype(BF16)
    lg = (jnp.dot(h_hi, w_hi, preferred_element_type=F32) + jnp.dot(h_lo, w_hi, preferred_element_type=F32)
          + jnp.dot(h_hi, w_lo, preferred_element_type=F32)) + br_ref[...]
    lane = lax.broadcasted_iota(jnp.int32, lg.shape, 1).astype(F32)
    neg = jnp.float32(-1e30)
    far = jnp.float32(2 * LANE)

    def first_argmax(v):
        top = jnp.max(v, axis=-1, keepdims=True)
        return top, jnp.min(jnp.where(v == top, lane, far), axis=-1, keepdims=True)

    is_group = lane < n_groups
    g_max, g_idx = first_argmax(jnp.where(is_group, lg, neg))
    g_top = 1.0 / jnp.sum(jnp.where(is_group, jnp.exp(lg - g_max), 0.0), axis=-1, keepdims=True)
    lo = n_groups + g_idx * epg
    el = jnp.where(jnp.logical_and(lane >= lo, lane < lo + epg), lg, neg)
    e1, i1 = first_argmax(el)
    e2, i2 = first_argmax(jnp.where(lane == i1, neg, el))
    r = jnp.exp(e2 - e1)
    gate1 = g_top / (1.0 + r)
    gate2 = g_top * r / (1.0 + r)
    gate_ref[...] = jnp.where(lane == 0, gate1, jnp.where(lane == 1, gate2, 0.0))
    exp_ref[...] = jnp.where(lane == 0, i1 - n_groups, jnp.where(lane == 1, i2 - n_groups, 0.0)).astype(jnp.int32)


def ffn_norm_router(x, g, mod, w_router, b_router, *, sh_idx, sc_idx, tm, row_fn, n_groups, epg):
    t, d = x.shape
    nr = w_router.shape[1]
    assert d % (2 * LANE) == 0
    return pl.pallas_call(
        functools.partial(_ffn_norm_router_kernel, sh_idx=sh_idx, sc_idx=sc_idx, n_groups=n_groups, epg=epg),
        grid=(t // tm,),
        in_specs=[pl.BlockSpec((tm, d), lambda i: (i, 0)),
                  pl.BlockSpec((1, d), lambda i: (0, 0)),
                  pl.BlockSpec((None, mod.shape[1], d), lambda i: (row_fn(i), 0, 0)),
                  pl.BlockSpec((d, nr), lambda i: (0, 0)),
                  pl.BlockSpec((1, nr), lambda i: (0, 0))],
        out_specs=[pl.BlockSpec((tm, d // 2), lambda i: (i, 0)),
                   pl.BlockSpec((tm, nr), lambda i: (i, 0)),
                   pl.BlockSpec((tm, nr), lambda i: (i, 0))],
        out_shape=[jax.ShapeDtypeStruct((t, d // 2), jnp.uint32), jax.ShapeDtypeStruct((t, nr), F32),
                   jax.ShapeDtypeStruct((t, nr), jnp.int32)],
        compiler_params=_params("parallel"),
        name="ffn_norm_router",
    )(x, g.reshape(1, d), mod, w_router, b_router)


def _rope_rotate(x, cos, sin, pair):
    lane = lax.broadcasted_iota(jnp.int32, x.shape, 1)
    first = (lane & (2 * pair - 1)) < pair
    partner = jnp.where(first, pltpu.roll(x, LANE - pair, axis=1), pltpu.roll(x, pair, axis=1))
    return x * cos + partner * sin


def _head_rms(x, g):
    return x * lax.rsqrt(jnp.mean(x * x, axis=-1, keepdims=True) + EPS) * g


def _gqa_prep_kernel(gq_ref, gk_ref, gv_ref, gq_gain_ref, gk_gain_ref, cos_ref, sin_ref,
                     q_ref, k_rot_ref, k_ref, v_ref, *, q_scale):
    cos, sin = cos_ref[...], sin_ref[...]
    pair = HEAD_DIM // 4
    for h in range(gq_ref.shape[1] // HEAD_DIM):
        hl = slice(h * HEAD_DIM, (h + 1) * HEAD_DIM)
        q = _rope_rotate(_head_rms(gq_ref[:, hl], gq_gain_ref[...]), cos, sin, pair)
        q_ref[:, hl] = (q * q_scale).astype(BF16)
    for h in range(gk_ref.shape[1] // HEAD_DIM):
        hl = slice(h * HEAD_DIM, (h + 1) * HEAD_DIM)
        k = _head_rms(gk_ref[:, hl], gk_gain_ref[...])
        k_ref[:, hl] = k
        k_rot_ref[:, hl] = _rope_rotate(k, cos, sin, pair).astype(BF16)
    v_ref[...] = gv_ref[...].astype(BF16)


def gqa_prep(u, cols, g_q, g_k, cos, sin, *, tm, q_scale):
    t = u.shape[0]
    (q0, qw), (k0, kw), (v0, vw) = cols['gq'], cols['gk'], cols['gv']
    row = lambda i: (i, 0)
    return pl.pallas_call(
        functools.partial(_gqa_prep_kernel, q_scale=q_scale),
        grid=(t // tm,),
        in_specs=[pl.BlockSpec((tm, qw), lambda i: (i, q0 // qw)),
                  pl.BlockSpec((tm, kw), lambda i: (i, k0 // kw)),
                  pl.BlockSpec((tm, vw), lambda i: (i, v0 // vw)),
                  pl.BlockSpec((1, HEAD_DIM), lambda i: (0, 0)),
                  pl.BlockSpec((1, HEAD_DIM), lambda i: (0, 0)),
                  pl.BlockSpec((tm, LANE), row),
                  pl.BlockSpec((tm, LANE), row)],
        out_specs=[pl.BlockSpec((tm, qw), row), pl.BlockSpec((tm, kw), row),
                   pl.BlockSpec((tm, kw), row), pl.BlockSpec((tm, vw), row)],
        out_shape=[jax.ShapeDtypeStruct((t, qw), BF16), jax.ShapeDtypeStruct((t, kw), BF16),
                   jax.ShapeDtypeStruct((t, kw), F32), jax.ShapeDtypeStruct((t, vw), BF16)],
        compiler_params=_params("parallel"),
        name="gqa_prep",
    )(u, u, u, g_q.reshape(1, HEAD_DIM), g_k.reshape(1, HEAD_DIM), cos, sin)


def _mla_q_kernel(cq_ref, g_cq_ref, wq_ref, g_q_ref, cos_ref, sin_ref, q_ref, *, q_scale):
    cqn = _head_rms(cq_ref[...], g_cq_ref[...]).astype(BF16)
    q = jnp.dot(cqn, wq_ref[...], preferred_element_type=F32)
    cos, sin = cos_ref[...], sin_ref[...]
    g_nope, g_rope = g_q_ref[:, :MLA_NOPE], g_q_ref[:, MLA_NOPE:]
    for h in range(q.shape[1] // MLA_PAD):
        nope = q[:, h * MLA_PAD:h * MLA_PAD + MLA_NOPE]
        rope = q[:, h * MLA_PAD + MLA_NOPE:(h + 1) * MLA_PAD]
        ss = jnp.sum(nope * nope, axis=-1, keepdims=True) + jnp.sum(rope * rope, axis=-1, keepdims=True)
        r = lax.rsqrt(ss / MLA_QK + EPS) * q_scale
        q_ref[:, h * MLA_PAD:h * MLA_PAD + MLA_NOPE] = (nope * r * g_nope).astype(BF16)
        q_ref[:, h * MLA_PAD + MLA_NOPE:(h + 1) * MLA_PAD] = (
            _rope_rotate(rope * g_rope, cos, sin, MLA_ROPE // 4) * r).astype(BF16)


def mla_queries(u, cols, g_cq, wq, g_q_pad, cos, sin, *, tm, q_scale):
    t = u.shape[0]
    c0, cw = cols['cq']
    n = wq.shape[1]
    return pl.pallas_call(
        functools.partial(_mla_q_kernel, q_scale=q_scale),
        grid=(t // tm,),
        in_specs=[pl.BlockSpec((tm, cw), lambda i: (i, c0 // cw)),
                  pl.BlockSpec((1, cw), lambda i: (0, 0)),
                  pl.BlockSpec((cw, n), lambda i: (0, 0)),
                  pl.BlockSpec((1, MLA_PAD), lambda i: (0, 0)),
                  pl.BlockSpec((tm, LANE), lambda i: (i, 0)),
                  pl.BlockSpec((tm, LANE), lambda i: (i, 0))],
        out_specs=pl.BlockSpec((tm, n), lambda i: (i, 0)),
        out_shape=jax.ShapeDtypeStruct((t, n), BF16),
        compiler_params=_params("parallel"),
        name="mla_queries",
    )(u, g_cq.reshape(1, cw), wq, g_q_pad.reshape(1, MLA_PAD), cos, sin)


def _mla_kv_kernel(ckv_ref, kr_ref, g_ckv_ref, wkv_ref, g_k_ref, cos_ref, sin_ref, ckvn_ref, k_ref, v_ref, *,
                   heads, normalise):
    ckv = ckv_ref[...]
    if normalise:
        ckv = _head_rms(ckv, g_ckv_ref[...])
    ckvn_ref[...] = ckv
    kv = jnp.dot(ckv.astype(BF16), wkv_ref[...], preferred_element_type=F32)
    lane = lax.broadcasted_iota(jnp.int32, (ckv.shape[0], LANE), 1)
    krope = jnp.where(lane < MLA_ROPE, kr_ref[:, :LANE], 0.0)
    ss_rope = jnp.sum(krope * krope, axis=-1, keepdims=True)
    g_nope, g_rope = g_k_ref[:, :MLA_NOPE], g_k_ref[:, MLA_NOPE:]
    rope = _rope_rotate(krope * g_rope, cos_ref[...], sin_ref[...], MLA_ROPE // 4)
    for h in range(heads):
        nope = kv[:, h * MLA_NOPE:(h + 1) * MLA_NOPE]
        r = lax.rsqrt((jnp.sum(nope * nope, axis=-1, keepdims=True) + ss_rope) / MLA_QK + EPS)
        k_ref[:, h * MLA_PAD:h * MLA_PAD + MLA_NOPE] = (nope * r * g_nope).astype(BF16)
        k_ref[:, h * MLA_PAD + MLA_NOPE:(h + 1) * MLA_PAD] = (rope * r).astype(BF16)
    v_ref[...] = kv[:, heads * MLA_NOPE:].astype(BF16)


def mla_keys_values(ckv_src, ckv_col, kr_src, kr_col, g_ckv, wkv, g_k_pad, cos, sin, *, heads, tm, normalise):
    t = ckv_src.shape[0]
    (c0, cw), (r0, rw) = ckv_col, kr_col
    return pl.pallas_call(
        functools.partial(_mla_kv_kernel, heads=heads, normalise=normalise),
        grid=(t // tm,),
        in_specs=[pl.BlockSpec((tm, cw), lambda i: (i, c0 // cw)),
                  pl.BlockSpec((tm, rw), lambda i: (i, r0 // rw)),
                  pl.BlockSpec((1, cw), lambda i: (0, 0)),
                  pl.BlockSpec((cw, wkv.shape[1]), lambda i: (0, 0)),
                  pl.BlockSpec((1, MLA_PAD), lambda i: (0, 0)),
                  pl.BlockSpec((tm, LANE), lambda i: (i, 0)),
                  pl.BlockSpec((tm, LANE), lambda i: (i, 0))],
        out_specs=[pl.BlockSpec((tm, cw), lambda i: (i, 0)),
                   pl.BlockSpec((tm, heads * MLA_PAD), lambda i: (i, 0)),
                   pl.BlockSpec((tm, heads * MLA_V), lambda i: (i, 0))],
        out_shape=[jax.ShapeDtypeStruct((t, cw), F32), jax.ShapeDtypeStruct((t, heads * MLA_PAD), BF16),
                   jax.ShapeDtypeStruct((t, heads * MLA_V), BF16)],
        compiler_params=_params("parallel"),
        name="mla_keys_values",
    )(ckv_src, kr_src, g_ckv.reshape(1, cw), wkv, g_k_pad.reshape(1, MLA_PAD), cos, sin)


def _gdn_prep_kernel(x_ref, prev_ref, next_ref, w_ref, o_ref, *, heads, tm, n_prompt_tiles, seq_lens):
    i = pl.program_id(0)
    x = x_ref[...]
    row = lax.broadcasted_iota(jnp.int32, (tm, 1), 0)
    in_prompt = i < n_prompt_tiles
    seq_len = jnp.where(in_prompt, seq_lens[0], seq_lens[1])
    first_row = jnp.where(in_prompt, i, i - n_prompt_tiles) * tm
    span = [min(s, tm) for s in seq_lens]
    local = jnp.where(in_prompt, row & (span[0] - 1), row & (span[1] - 1))
    span_t = jnp.where(in_prompt, span[0], span[1])
    tile_starts_seq = lax.rem(first_row, seq_len) == 0
    tile_ends_seq = lax.rem(first_row + tm, seq_len) == 0
    is_start = jnp.logical_and(local == 0, jnp.logical_or(row != 0, tile_starts_seq))
    is_end = jnp.logical_and(local == span_t - 1, jnp.logical_or(row != tm - 1, tile_ends_seq))
    prev = jnp.where(row == 0, prev_ref[7:8, :], pltpu.roll(x, 1, axis=0))
    nxt = jnp.where(row == tm - 1, next_ref[0:1, :], pltpu.roll(x, tm - 1, axis=0))
    prev = jnp.where(is_start, 0.0, prev)
    nxt = jnp.where(is_end, 0.0, nxt)
    conv = prev * w_ref[0:1, :] + x * w_ref[1:2, :] + nxt * w_ref[2:3, :]
    act = conv * jax.nn.sigmoid(conv)
    for h in range(3 * heads):
        hl = slice(h * GDN_DK, (h + 1) * GDN_DK)
        slab = act[:, hl]
        if h < 2 * heads:
            slab = slab * lax.rsqrt(jnp.sum(slab * slab, axis=-1, keepdims=True) + EPS)
            if h < heads:
                slab = slab * GDN_DK ** -0.5
        o_ref[:, hl] = slab


def gdn_prep(u, col, w_conv, *, heads, tm, t_prompt, seq_lens):
    t = u.shape[0]
    c0, cw = col
    assert all(s & (s - 1) == 0 for s in (tm,) + tuple(min(s, tm) for s in seq_lens))
    eight = 8
    cb = c0 // cw
    last8 = t // eight - 1
    return pl.pallas_call(
        functools.partial(_gdn_prep_kernel, heads=heads, tm=tm, n_prompt_tiles=t_prompt // tm, seq_lens=seq_lens),
        grid=(t // tm,),
        in_specs=[pl.BlockSpec((tm, cw), lambda i: (i, cb)),
                  pl.BlockSpec((eight, cw), lambda i: (jnp.maximum(i * (tm // eight) - 1, 0), cb)),
                  pl.BlockSpec((eight, cw), lambda i: (jnp.minimum((i + 1) * (tm // eight), last8), cb)),
                  pl.BlockSpec((3, cw), lambda i: (0, 0))],
        out_specs=pl.BlockSpec((tm, cw), lambda i: (i, 0)),
        out_shape=jax.ShapeDtypeStruct((t, cw), F32),
        compiler_params=_params("parallel"),
        name="gdn_prep",
    )(u, u, u, w_conv)


def _attn_kernel(*refs, n_seg, aliased, group, dq, dv, tk):
    q_ref = refs[0]
    kv_refs = [(refs[1 + 2 * s], refs[2 + 2 * s]) for s in range(n_seg)]
    o_ref, s_ref, mx_ref, l_ref, acc_ref = refs[1 + 2 * n_seg + (1 if aliased else 0):]
    tq = q_ref.shape[0]
    q = jnp.concatenate([q_ref[:, g * dq:(g + 1) * dq] for g in range(group)], axis=0)
    slabs = tk // LANE

    def for_chunks(body):
        base = 0
        for k_ref, v_ref in kv_refs:
            nk = k_ref.shape[0] // tk
            main = nk // ATTN_UNROLL

            def group_body(i, carry, k_ref=k_ref, v_ref=v_ref, base=base):
                for u in range(ATTN_UNROLL):
                    body(k_ref, v_ref, i * ATTN_UNROLL + u, base)
                return carry
            if main:
                lax.fori_loop(0, main, group_body, 0)
            for c in range(main * ATTN_UNROLL, nk):
                body(k_ref, v_ref, c, base)
            base += nk

    mx_ref[...] = jnp.full(mx_ref.shape, -jnp.inf, F32)

    def rows_of(c):
        return pl.ds(c * tk if isinstance(c, int) else pl.multiple_of(c * tk, tk), tk)

    def scores(k_ref, v_ref, c, base):
        s = lax.dot_general(q, k_ref[rows_of(c), :], (((1,), (1,)), ((), ())),
                            preferred_element_type=F32)
        s_ref[base + c] = s
        mx = mx_ref[...]
        for j in range(slabs):
            mx = jnp.maximum(mx, s[:, j * LANE:(j + 1) * LANE])
        mx_ref[...] = mx

    for_chunks(scores)
    m = jnp.broadcast_to(jnp.max(mx_ref[...], axis=-1, keepdims=True), mx_ref.shape)
    l_ref[...] = jnp.zeros_like(l_ref)
    acc_ref[...] = jnp.zeros_like(acc_ref)

    def weighted(k_ref, v_ref, c, base):
        s = s_ref[base + c]
        p = jnp.concatenate([jnp.exp2(s[:, j * LANE:(j + 1) * LANE] - m) for j in range(slabs)], axis=1)
        lsum = l_ref[...]
        for j in range(slabs):
            lsum = lsum + p[:, j * LANE:(j + 1) * LANE]
        l_ref[...] = lsum
        acc_ref[...] += jnp.dot(p.astype(BF16), v_ref[rows_of(c), :], preferred_element_type=F32)

    for_chunks(weighted)
    o = acc_ref[...] / jnp.sum(l_ref[...], axis=-1, keepdims=True)
    for g in range(group):
        o_ref[:, g * dv:(g + 1) * dv] = o[g * tq:(g + 1) * tq].astype(o_ref.dtype)


def attention(q, q_row0, kv_segments, *, batch, lq, kv_heads, group, dq, dv, out_rows, out_row0, prev_out, name):
    tq = _pick(lq, (512, 256, 128)) // (2 if group > 1 else 1)
    tk = _pick(math.gcd(*[length for _, _, _, length in kv_segments]), (256, 128))
    nq = lq // tq
    m_rows = group * tq
    assert q_row0 % tq == 0 and out_row0 % tq == 0
    in_specs = [pl.BlockSpec((tq, group * dq), lambda b, h, i: (q_row0 // tq + b * nq + i, h))]
    operands = [q]
    for k, v, first_row, length in kv_segments:
        assert first_row % length == 0
        in_specs += [pl.BlockSpec((length, dq), lambda b, h, i, f=first_row // length: (f + b, h)),
                     pl.BlockSpec((length, dv), lambda b, h, i, f=first_row // length: (f + b, h))]
        operands += [k, v]
    aliases = {}
    if prev_out is not None:
        in_specs.append(pl.BlockSpec(memory_space=pl.ANY))
        aliases = {len(operands): 0}
        operands.append(prev_out)
    n_chunks = sum(length // tk for _, _, _, length in kv_segments)
    return pl.pallas_call(
        functools.partial(_attn_kernel, n_seg=len(kv_segments), aliased=prev_out is not None, group=group,
                          dq=dq, dv=dv, tk=tk),
        grid=(batch, kv_heads, nq),
        in_specs=in_specs,
        out_specs=pl.BlockSpec((tq, group * dv), lambda b, h, i: (out_row0 // tq + b * nq + i, h)),
        out_shape=jax.ShapeDtypeStruct((out_rows, kv_heads * group * dv), BF16),
        input_output_aliases=aliases,
        scratch_shapes=[pltpu.VMEM((n_chunks, m_rows, tk), F32), pltpu.VMEM((m_rows, LANE), F32),
                        pltpu.VMEM((m_rows, LANE), F32), pltpu.VMEM((m_rows, dv), F32)],
        compiler_params=_params("parallel", "parallel", "parallel"),
        name=name,
    )(*operands)


def _split_hi_lo(a):
    hi = a.astype(BF16).astype(F32)
    return hi, (a - hi).astype(BF16).astype(F32)


def _dot_split(a_parts, b_parts):
    ah, al = a_parts
    bh, bl = b_parts
    lhs = jnp.concatenate([ah, al, ah, al], axis=2).astype(BF16)
    rhs = jnp.concatenate([bh, bh, bl, bl], axis=1).astype(BF16)
    return lax.dot_general(lhs, rhs, (((2,), (1,)), ((0,), (0,))), preferred_element_type=F32)


def _dot_nt(a, b):
    return lax.dot_general(a.astype(BF16), b.astype(BF16), (((2,), (2,)), ((0,), (0,))), preferred_element_type=F32)


def _chunk_masks(backward):
    r = lax.broadcasted_iota(jnp.int32, (GDN_CHUNK, GDN_CHUNK), 0)
    c = lax.broadcasted_iota(jnp.int32, (GDN_CHUNK, GDN_CHUNK), 1)
    incl = (r <= c) if backward else (r >= c)
    strict = (r < c) if backward else (r > c)
    return r, c, incl, strict


def _decay_mask(dcol, r, c, incl):
    drow = jnp.sum(jnp.where(r == c, dcol, 0.0), axis=1, keepdims=True)
    return jnp.where(incl, jnp.exp(jnp.where(incl, dcol - drow, 0.0)), 0.0)


def _unit_tri_inverse(a_mat, r, c):
    eye = (r == c).astype(F32)
    same16 = jnp.right_shift(r, 4) == jnp.right_shift(c, 4)
    same32 = jnp.right_shift(r, 5) == jnp.right_shift(c, 5)
    x = jnp.where(same16, -a_mat, 0.0)
    inv = eye + x
    x_parts = _split_hi_lo(x)
    for _ in range(3):
        x_parts = _split_hi_lo(_dot_split(x_parts, x_parts))
        inv = inv + _dot_split(_split_hi_lo(inv), x_parts)
    for off_blocks in (jnp.logical_and(same32, jnp.logical_not(same16)), jnp.logical_not(same32)):
        off = jnp.where(off_blocks, a_mat, 0.0)
        inv_parts = _split_hi_lo(inv)
        inv = inv - _dot_split(inv_parts, _split_hi_lo(_dot_split(_split_hi_lo(off), inv_parts)))
    return inv


def _head_column(ref, lane, index, n_c):
    col = jnp.sum(jnp.where(lane == index, ref[...], 0.0), axis=1, keepdims=True)
    return col.reshape(n_c, GDN_CHUNK, 1)


def _gdn_solve_kernel(k_ref, v_ref, d_ref, b_ref, uw_ref, *, head_lanes):
    h = pl.program_id(1)
    n_c = k_ref.shape[0] // GDN_CHUNK
    lane = lax.broadcasted_iota(jnp.int32, (1, d_ref.shape[1]), 1)
    kc = k_ref[...].reshape(n_c, GDN_CHUNK, GDN_DK)
    vc = v_ref[...].reshape(n_c, GDN_CHUNK, GDN_DV)
    for direction in range(2):
        r, c, incl, strict = _chunk_masks(direction == 1)
        dcol = _head_column(d_ref, lane, direction * head_lanes + h, n_c)
        bcol = _head_column(b_ref, lane, direction * head_lanes + h, n_c)
        kb = kc * bcol
        a_mat = jnp.where(strict, _dot_nt(kb, kc) * _decay_mask(dcol, r, c, incl), 0.0)
        rhs = jnp.concatenate([vc * bcol, kb * jnp.exp(dcol)], axis=2)
        uw = _dot_split(_split_hi_lo(_unit_tri_inverse(a_mat, r, c)), _split_hi_lo(rhs))
        uw_ref[direction] = uw.reshape(n_c * GDN_CHUNK, GDN_DV + GDN_DK)


def gdn_solve(qkv, dcs, beta, *, heads):
    t = qkv.shape[0]
    rows = _pick(t, (1024, 512, 256, 128, 64))
    width = GDN_DV + GDN_DK
    return pl.pallas_call(
        functools.partial(_gdn_solve_kernel, head_lanes=heads),
        grid=(t // rows, heads),
        in_specs=[pl.BlockSpec((rows, GDN_DK), lambda i, h: (i, heads + h)),
                  pl.BlockSpec((rows, GDN_DV), lambda i, h: (i, 2 * heads + h)),
                  pl.BlockSpec((rows, 2 * heads), lambda i, h: (i, 0)),
                  pl.BlockSpec((rows, 2 * heads), lambda i, h: (i, 0))],
        out_specs=pl.BlockSpec((2, rows, width), lambda i, h: (0, i, h)),
        out_shape=jax.ShapeDtypeStruct((2, t, heads * width), F32),
        compiler_params=_params("parallel", "parallel"),
        name="gdn_solve",
    )(qkv, qkv, dcs, beta)


def _gdn_scan_kernel(fwd_blk_ref, bwd_blk_ref, seq_ref, edge_ref, qf_ref, kf_ref, uwf_ref, df_ref,
                     qb_ref, kb_ref, uwb_ref, db_ref, s0_ref, of_ref, ob_ref, sfin_ref, s_ref, *,
                     heads_per_step, head_lanes):
    step = pl.program_id(1)
    hg = pl.program_id(0)
    rows = qf_ref.shape[0]
    n_c = rows // GDN_CHUNK
    lane = lax.broadcasted_iota(jnp.int32, (1, df_ref.shape[1]), 1)

    @pl.when((edge_ref[step] & 1) != 0)
    def _():
        s_ref[...] = s0_ref[...].astype(F32)

    def bmm(a, b):
        return lax.dot_general(a, b, (((2,), (1,)), ((0,), (0,))), preferred_element_type=F32)

    width = GDN_DV + GDN_DK
    streams = ((qf_ref, kf_ref, uwf_ref, df_ref, of_ref), (qb_ref, kb_ref, uwb_ref, db_ref, ob_ref))
    for ci in range(n_c):
        for direction, (q_ref, k_ref, uw_ref, d_ref, o_ref) in enumerate(streams):
            backward = direction == 1
            r, c, incl, _ = _chunk_masks(backward)
            pos = (n_c - 1 - ci) if backward else ci
            sl = slice(pos * GDN_CHUNK, (pos + 1) * GDN_CHUNK)
            heads = range(heads_per_step)
            dcol = jnp.stack([jnp.sum(jnp.where(lane == direction * head_lanes + hg * heads_per_step + hh,
                                                d_ref[sl, :], 0.0), axis=1, keepdims=True) for hh in heads])
            qc = jnp.stack([q_ref[sl, hh * GDN_DK:(hh + 1) * GDN_DK] for hh in heads])
            kc = jnp.stack([k_ref[sl, hh * GDN_DK:(hh + 1) * GDN_DK] for hh in heads])
            u = jnp.stack([uw_ref[sl, hh * width:hh * width + GDN_DV] for hh in heads])
            w = jnp.stack([uw_ref[sl, hh * width + GDN_DV:(hh + 1) * width] for hh in heads])
            attn = _dot_nt(qc, kc) * _decay_mask(dcol, r, c, incl)
            d_last = dcol[:, 0:1, :] if backward else dcol[:, GDN_CHUNK - 1:GDN_CHUNK, :]
            q_dec = (qc * jnp.exp(dcol)).astype(BF16)
            k_dec = (kc * jnp.exp(d_last - dcol)).astype(BF16)
            s = s_ref[direction]
            s_b = s.astype(BF16)
            v_new = u - bmm(w.astype(BF16), s_b)
            v_b = v_new.astype(BF16)
            o = bmm(q_dec, s_b) + bmm(attn.astype(BF16), v_b)
            for hh in heads:
                o_ref[sl, hh * GDN_DV:(hh + 1) * GDN_DV] = o[hh]
            s_ref[direction] = s * jnp.exp(d_last) + lax.dot_general(
                k_dec, v_b, (((1,), (1,)), ((0,), (0,))), preferred_element_type=F32)

    @pl.when((edge_ref[step] & 2) != 0)
    def _():
        sfin_ref[...] = s_ref[...]


def gdn_scan(qkv, uw, dcs, s0, *, sequences, heads):
    t = qkv.shape[0]
    rows = _pick(math.gcd(*[length for _, length in sequences]), (256, 128, 64))
    hps = _pick(heads, (4, 2, 1))
    hgroups = heads // hps
    width = GDN_DV + GDN_DK
    fwd_blk, bwd_blk, seq_id, edge = [], [], [], []
    for s, (first_row, length) in enumerate(sequences):
        assert first_row % rows == 0 and length % rows == 0
        n_l = length // rows
        for i in range(n_l):
            fwd_blk.append(first_row // rows + i)
            bwd_blk.append(first_row // rows + n_l - 1 - i)
            seq_id.append(s)
            edge.append((1 if i == 0 else 0) | (2 if i == n_l - 1 else 0))
    assert sorted(fwd_blk) == list(range(t // rows))
    tables = [jnp.asarray(np.asarray(v, np.int32)) for v in (fwd_blk, bwd_blk, seq_id, edge)]

    def stream_specs(table, direction):
        return [pl.BlockSpec((rows, hps * GDN_DK), lambda g, s, fb, bb, sq, ed: ((fb, bb)[table][s], g)),
                pl.BlockSpec((rows, hps * GDN_DK), lambda g, s, fb, bb, sq, ed: ((fb, bb)[table][s], hgroups + g)),
                pl.BlockSpec((None, rows, hps * width), lambda g, s, fb, bb, sq, ed: (direction, (fb, bb)[table][s], g)),
                pl.BlockSpec((rows, 2 * heads), lambda g, s, fb, bb, sq, ed: ((fb, bb)[table][s], 0))]

    state_spec = pl.BlockSpec((None, 2, hps, GDN_DK, GDN_DV), lambda g, s, fb, bb, sq, ed: (sq[s], 0, g, 0, 0))
    grid_spec = pltpu.PrefetchScalarGridSpec(
        num_scalar_prefetch=4,
        grid=(hgroups, len(fwd_blk)),
        in_specs=stream_specs(0, 0) + stream_specs(1, 1) + [state_spec],
        out_specs=[pl.BlockSpec((rows, hps * GDN_DV), lambda g, s, fb, bb, sq, ed: (fb[s], g)),
                   pl.BlockSpec((rows, hps * GDN_DV), lambda g, s, fb, bb, sq, ed: (bb[s], g)),
                   state_spec],
        scratch_shapes=[pltpu.VMEM((2, hps, GDN_DK, GDN_DV), F32)],
    )
    return pl.pallas_call(
        functools.partial(_gdn_scan_kernel, heads_per_step=hps, head_lanes=heads),
        grid_spec=grid_spec,
        out_shape=[jax.ShapeDtypeStruct((t, heads * GDN_DV), F32),
                   jax.ShapeDtypeStruct((t, heads * GDN_DV), F32),
                   jax.ShapeDtypeStruct((len(sequences), 2, heads, GDN_DK, GDN_DV), F32)],
        compiler_params=_params("parallel", "arbitrary"),
        name="gdn_scan",
    )(*tables, qkv, qkv, uw, dcs, qkv, qkv, uw, dcs, s0)


def _row_gather(idx_ref, src_hbm, dst_vmem, sem, n_rows):
    def start():
        def body(r, carry):
            pltpu.make_async_copy(src_hbm.at[pl.ds(idx_ref[0, r], 1), :], dst_vmem.at[pl.ds(r, 1), :], sem).start()
            return carry
        lax.fori_loop(0, n_rows, body, 0, unroll=GATHER_UNROLL)

    def wait():
        def body(r, carry):
            pltpu.make_async_copy(src_hbm.at[pl.ds(0, 1), :], dst_vmem.at[pl.ds(0, 1), :], sem).wait()
            return carry
        lax.fori_loop(0, n_rows, body, 0, unroll=GATHER_UNROLL)
    return start, wait


def _moe_kernel(be_ref, nu_ref, tok_ref, tok_next_ref, h_hbm, wg_ref, wu_ref, wd_ref, o_ref,
                xbuf_ref, xlo_ref, xhi_ref, acc_ref, sem):
    i = pl.program_id(0)
    f = pl.program_id(1)
    n_used = nu_ref[0]
    tm, half = xlo_ref.shape

    @pl.when(i < n_used)
    def _():
        @pl.when(f == 0)
        def _():
            for slot in range(2):
                start_this, wait_this = _row_gather(tok_ref, h_hbm, xbuf_ref.at[slot], sem.at[slot], tm)
                start_next, _ = _row_gather(tok_next_ref, h_hbm, xbuf_ref.at[1 - slot], sem.at[1 - slot], tm)

                @pl.when(lax.rem(i, 2) == slot)
                def _():
                    @pl.when(i == 0)
                    def _():
                        start_this()
                    wait_this()

                    @pl.when(i + 1 < n_used)
                    def _():
                        start_next()
                    lo, hi = _unpack_bf16_pair(xbuf_ref[slot])
                    xlo_ref[...] = lo.astype(BF16)
                    xhi_ref[...] = hi.astype(BF16)
            acc_ref[...] = jnp.zeros_like(acc_ref)

        def x_dot(w_ref):
            return (jnp.dot(xlo_ref[...], w_ref[:half, :].astype(BF16), preferred_element_type=F32)
                    + jnp.dot(xhi_ref[...], w_ref[half:, :].astype(BF16), preferred_element_type=F32))

        gate = x_dot(wg_ref)
        up = x_dot(wu_ref)
        hmid = (gate * jax.nn.sigmoid(gate) * up).astype(BF16)
        acc_ref[...] += jnp.dot(hmid, wd_ref[...].astype(BF16), preferred_element_type=F32)

        @pl.when(f == pl.num_programs(1) - 1)
        def _():
            o_ref[...] = _pack_bf16_pair(acc_ref[:, :half], acc_ref[:, half:])

    @pl.when(jnp.logical_and(i >= n_used, f == 0))
    def _():
        o_ref[...] = jnp.zeros_like(o_ref)


def moe_experts(h_packed, slot_tok, block_e, n_used, w_gate, w_up, w_down, *, tm, layer):
    d = 2 * h_packed.shape[1]
    p = slot_tok.shape[0]
    f_dim = w_gate.shape[3]
    tf = _pick(f_dim, (256, 128))
    n_blocks = p // tm
    last_f = f_dim // tf - 1
    tok_blocks = slot_tok.reshape(n_blocks, 1, tm)

    def blk(i, nu):
        return jnp.minimum(i, nu[0] - 1)

    def f_blk(i, f, nu):
        return jnp.where(i < nu[0], f, last_f)

    grid_spec = pltpu.PrefetchScalarGridSpec(
        num_scalar_prefetch=2,
        grid=(n_blocks, f_dim // tf),
        in_specs=[pl.BlockSpec((None, 1, tm), lambda i, f, be, nu: (blk(i, nu), 0, 0), memory_space=pltpu.SMEM),
                  pl.BlockSpec((None, 1, tm), lambda i, f, be, nu: (blk(i + 1, nu), 0, 0), memory_space=pltpu.SMEM),
                  pl.BlockSpec(memory_space=pl.ANY),
                  pl.BlockSpec((None, None, d, tf), lambda i, f, be, nu: (layer, be[blk(i, nu)], 0, f_blk(i, f, nu))),
                  pl.BlockSpec((None, None, d, tf), lambda i, f, be, nu: (layer, be[blk(i, nu)], 0, f_blk(i, f, nu))),
                  pl.BlockSpec((None, None, tf, d), lambda i, f, be, nu: (layer, be[blk(i, nu)], f_blk(i, f, nu), 0))],
        out_specs=pl.BlockSpec((tm, d // 2), lambda i, f, be, nu: (i, 0)),
        scratch_shapes=[pltpu.VMEM((2, tm, d // 2), jnp.uint32), pltpu.VMEM((tm, d // 2), BF16),
                        pltpu.VMEM((tm, d // 2), BF16), pltpu.VMEM((tm, d), F32),
                        pltpu.SemaphoreType.DMA((2,))],
    )
    return pl.pallas_call(
        _moe_kernel,
        grid_spec=grid_spec,
        out_shape=jax.ShapeDtypeStruct((p, d // 2), jnp.uint32),
        compiler_params=_params("arbitrary", "arbitrary"),
        name="moe_experts",
    )(block_e, n_used, tok_blocks, tok_blocks, h_packed, w_gate, w_up, w_down)


def _moe_combine_kernel(dest_ref, dest_next_ref, y_hbm, x_ref, gate_ref, mod_ref, o_ref, rows_ref, sem, *, gate_idx):
    i = pl.program_id(0)
    tc, d = x_ref.shape
    half = d // 2
    for slot in range(2):
        start_this, wait_this = _row_gather(dest_ref, y_hbm, rows_ref.at[slot], sem.at[slot], 2 * tc)
        start_next, _ = _row_gather(dest_next_ref, y_hbm, rows_ref.at[1 - slot], sem.at[1 - slot], 2 * tc)

        @pl.when(lax.rem(i, 2) == slot)
        def _():
            @pl.when(i == 0)
            def _():
                start_this()
            wait_this()

            @pl.when(i + 1 < pl.num_programs(0))
            def _():
                start_next()
            lo0, hi0 = _unpack_bf16_pair(rows_ref[slot, :tc, :])
            lo1, hi1 = _unpack_bf16_pair(rows_ref[slot, tc:, :])
            g0 = gate_ref[:, 0:1]
            g1 = gate_ref[:, 1:2]
            o_ref[:, :half] = x_ref[:, :half] + mod_ref[gate_idx:gate_idx + 1, :half] * (lo0 * g0 + lo1 * g1)
            o_ref[:, half:] = x_ref[:, half:] + mod_ref[gate_idx:gate_idx + 1, half:] * (hi0 * g0 + hi1 * g1)


def moe_combine_residual(y_packed, dest, gates, x, mod, *, gate_idx, tc, row_fn):
    t, d = x.shape
    n_tiles = t // tc
    dest_tiles = dest.reshape(n_tiles, tc, TOP_K).transpose(0, 2, 1).reshape(n_tiles, 1, TOP_K * tc)
    return pl.pallas_call(
        functools.partial(_moe_combine_kernel, gate_idx=gate_idx),
        grid=(n_tiles,),
        in_specs=[pl.BlockSpec((None, 1, TOP_K * tc), lambda i: (i, 0, 0), memory_space=pltpu.SMEM),
                  pl.BlockSpec((None, 1, TOP_K * tc), lambda i: (jnp.minimum(i + 1, n_tiles - 1), 0, 0),
                               memory_space=pltpu.SMEM),
                  pl.BlockSpec(memory_space=pl.ANY),
                  pl.BlockSpec((tc, d), lambda i: (i, 0)),
                  pl.BlockSpec((tc, gates.shape[1]), lambda i: (i, 0)),
                  pl.BlockSpec((None, mod.shape[1], d), lambda i: (row_fn(i), 0, 0))],
        out_specs=pl.BlockSpec((tc, d), lambda i: (i, 0)),
        out_shape=jax.ShapeDtypeStruct((t, d), F32),
        scratch_shapes=[pltpu.VMEM((2, TOP_K * tc, d // 2), jnp.uint32), pltpu.SemaphoreType.DMA((2,))],
        compiler_params=_params("arbitrary"),
        name="moe_combine",
    )(dest_tiles, dest_tiles, y_packed, x, gates, mod)


def _rms(x, g):
    return x * lax.rsqrt(jnp.mean(x * x, axis=-1, keepdims=True) + EPS) * g


def _rope_tables(t_prompt, dec_batch, dec_seq, width):
    a = width // 2
    half = a // 2
    freqs = ROPE_THETA ** (-np.arange(half, dtype=np.float32) / half)
    t = np.arange(dec_seq)
    row, col = t // GRID_W, t % GRID_W
    ang_r = row[:, None].astype(np.float32) * freqs[None, :]
    ang_c = col[:, None].astype(np.float32) * freqs[None, :]
    cos = np.concatenate([np.cos(ang_r), np.cos(ang_r), np.cos(ang_c), np.cos(ang_c)], axis=-1)
    sin = np.concatenate([-np.sin(ang_r), np.sin(ang_r), -np.sin(ang_c), np.sin(ang_c)], axis=-1)
    cos = np.concatenate([np.ones((t_prompt, width), np.float32), np.tile(cos, (dec_batch, 1))], axis=0)
    sin = np.concatenate([np.zeros((t_prompt, width), np.float32), np.tile(sin, (dec_batch, 1))], axis=0)
    return jnp.asarray(cos), jnp.asarray(sin)


def _gdn_gates(a, b, a_log, dt_bias, heads):
    t = a.shape[0]
    g = -jnp.exp(a_log.reshape(2 * heads)) * jax.nn.softplus(a + dt_bias.reshape(2 * heads))
    gc = g.reshape(t // GDN_CHUNK, GDN_CHUNK, 2 * heads)
    prefix = jnp.cumsum(gc[..., :heads], axis=1)
    suffix = jnp.flip(jnp.cumsum(jnp.flip(gc[..., heads:], axis=1), axis=1), axis=1)
    dcs = jnp.concatenate([prefix, suffix], axis=-1).reshape(t, 2 * heads)
    return dcs, jax.nn.sigmoid(b)


def _dispatch_plan(experts, n_experts, tm):
    t, k = experts.shape
    a = t * k
    n_blocks = a // tm + n_experts
    flat_e = experts.reshape(-1)
    onehot = (flat_e[:, None] == jnp.arange(n_experts, dtype=jnp.int32)[None, :]).astype(jnp.int32)
    csum = jnp.cumsum(onehot, axis=0)
    pos = jnp.sum((csum - onehot) * onehot, axis=1)
    counts = csum[-1]
    padded = (counts + tm - 1) // tm * tm
    pend = jnp.cumsum(padded)
    pstart = pend - padded
    dest = pstart[flat_e] + pos
    flat_tok = jnp.arange(a, dtype=jnp.int32) // k
    slot_tok = jnp.zeros((n_blocks * tm,), jnp.int32).at[dest].set(flat_tok)
    block_e = jnp.minimum(jnp.searchsorted(pend, jnp.arange(n_blocks, dtype=jnp.int32) * tm, side='right'),
                          n_experts - 1).astype(jnp.int32)
    n_used = (pend[-1] // tm).astype(jnp.int32).reshape(1)
    return dest.reshape(t, k), slot_tok, block_e, n_used


def kernel(x_prompt, x_sample, cache_mla_ckv, cache_mla_krope, cache_gqa_k, cache_gqa_v, state_gdn, c, c_ctx, w_mod, b_mod, g_norm_mix, g_norm_ffn, w_in, g_cq, w_q_up, g_ckv, w_kv_up, g_q_mla, g_k_mla, g_q_gqa, g_k_gqa, w_conv, a_log, dt_bias, g_gdn_out, w_out, w_router_group, b_router_group, w_router_expert, b_router_expert, w_gate, w_up, w_down):
    batch, seq, d = x_prompt.shape
    dec_batch, dec_seq, _ = x_sample.shape
    depth = w_in.shape[0]
    past = cache_mla_ckv.shape[2]
    q_lora = w_q_up.shape[1]
    kv_lora = w_kv_up.shape[1]
    mla_heads = w_q_up.shape[2] // MLA_QK
    kv_heads = cache_gqa_k.shape[3]
    gdn_heads = state_gdn.shape[3]
    gqa_heads = (w_out.shape[1] - mla_heads * MLA_V - gdn_heads * GDN_DV) // HEAD_DIM
    group = gqa_heads // kv_heads
    n_groups = w_router_group.shape[2]
    n_experts = w_gate.shape[1]
    gdn_qkv = gdn_heads * (2 * GDN_DK + GDN_DV)
    gdn_width = gdn_heads * GDN_DV

    t_p = batch * seq
    t_s = dec_batch * dec_seq
    t_all = t_p + t_s
    tm = _pick(math.gcd(t_p, dec_seq), (512, 256, 128))
    row_fn = _mod_row_fn(tm, t_p, dec_seq)
    moe_tm = _pick(t_all * TOP_K, (512, 256, 128))
    combine_tc = _pick(math.gcd(t_p, dec_seq), (256, 128))

    src_sizes = (q_lora, kv_lora, MLA_ROPE, gqa_heads * HEAD_DIM, kv_heads * HEAD_DIM, kv_heads * HEAD_DIM,
                 gdn_qkv, gdn_width, 2 * gdn_heads, 2 * gdn_heads)
    src_off = np.concatenate([[0], np.cumsum(src_sizes)])
    src = dict(zip(('cq', 'ckv', 'krope', 'gq', 'gk', 'gv', 'qkv', 'z', 'a', 'b'),
                   [(int(src_off[j]), int(src_off[j + 1])) for j in range(len(src_sizes))]))
    tail_w = -(-(MLA_ROPE + 4 * gdn_heads) // LANE) * LANE
    cols, pieces, pos = {}, [], 0
    for name, width in (('qkv', gdn_qkv), ('z', gdn_width), ('ckv', kv_lora), ('gq', gqa_heads * HEAD_DIM),
                        ('cq', q_lora), ('tail', tail_w), ('gk', kv_heads * HEAD_DIM), ('gv', kv_heads * HEAD_DIM)):
        start = -(-pos // width) * width
        assert width % LANE == 0
        if start > pos:
            pieces.append(('pad', start - pos))
        pieces.append((name, width))
        cols[name] = (start, width)
        pos = start + width
    n_in = pos
    tail0 = cols['tail'][0]

    def build_w_in(w):
        parts = []
        for name, width in pieces:
            if name == 'pad':
                parts.append(jnp.zeros((d, width), F32))
            elif name == 'tail':
                parts += [w[:, src[s][0]:src[s][1]] for s in ('krope', 'a', 'b')]
                parts.append(jnp.zeros((d, width - MLA_ROPE - 4 * gdn_heads), F32))
            else:
                parts.append(w[:, src[name][0]:src[name][1]])
        return jnp.concatenate(parts, axis=1).astype(BF16)

    def padded_rope_tables(width):
        cos, sin = _rope_tables(t_p, dec_batch, dec_seq, width)
        return (jnp.pad(cos, ((0, 0), (0, LANE - width)), constant_values=1.0),
                jnp.pad(sin, ((0, 0), (0, LANE - width))))

    cos_g, sin_g = padded_rope_tables(HEAD_DIM)
    cos_m, sin_m = padded_rope_tables(MLA_ROPE)
    n_ctx = dec_batch * past
    cos_id, sin_id = jnp.ones((n_ctx, LANE), F32), jnp.zeros((n_ctx, LANE), F32)
    sequences = ([(b * seq, seq) for b in range(batch)]
                 + [(t_p + b * dec_seq, dec_seq) for b in range(dec_batch)])

    cond8 = jnp.zeros((8, d), F32).at[0].set(c_ctx).at[1:1 + dec_batch].set(c)

    x = jnp.concatenate([x_prompt.reshape(t_p, d), x_sample.reshape(t_s, d)], axis=0)
    new_ckv, new_krope, new_k, new_v, new_s = [], [], [], [], []
    q_scale_m = MLA_QK ** -0.5 * LOG2E
    q_scale_g = HEAD_DIM ** -0.5 * LOG2E

    for l in range(depth):
        mod = modulation(cond8, w_mod, b_mod, l).reshape(8, 6, d)

        u = norm_matmul(x, g_norm_mix[l], mod, build_w_in(w_in[l]), sh_idx=0, sc_idx=1, tm=tm, row_fn=row_fn)

        wq = w_q_up[l].reshape(q_lora, mla_heads, MLA_QK)
        wq = jnp.pad(wq, ((0, 0), (0, 0), (0, MLA_PAD - MLA_QK))).reshape(q_lora, mla_heads * MLA_PAD).astype(BF16)
        q_m = mla_queries(u, cols, g_cq[l], wq, jnp.pad(g_q_mla[l], (0, MLA_PAD - MLA_QK)), cos_m, sin_m,
                          tm=tm, q_scale=q_scale_m)
        wkv = w_kv_up[l].reshape(kv_lora, mla_heads, MLA_NOPE + MLA_V)
        wkv = jnp.concatenate([wkv[..., :MLA_NOPE].reshape(kv_lora, -1), wkv[..., MLA_NOPE:].reshape(kv_lora, -1)],
                              axis=1).astype(BF16)
        g_k_pad = jnp.pad(g_k_mla[l], (0, MLA_PAD - MLA_QK))
        ckv_n, k_m, v_m = mla_keys_values(u, cols['ckv'], u, cols['tail'], g_ckv[l], wkv, g_k_pad, cos_m, sin_m,
                                          heads=mla_heads, tm=tm, normalise=True)
        ctx_krope = jnp.pad(cache_mla_krope[:, l].reshape(n_ctx, MLA_ROPE), ((0, 0), (0, LANE - MLA_ROPE)))
        _, k_m_ctx, v_m_ctx = mla_keys_values(cache_mla_ckv[:, l].reshape(n_ctx, kv_lora), (0, kv_lora), ctx_krope,
                                              (0, LANE), g_ckv[l], wkv, g_k_pad, cos_id, sin_id, heads=mla_heads,
                                              tm=_pick(n_ctx, (512, 256, 128)), normalise=False)

        mla_common = dict(kv_heads=mla_heads, group=1, dq=MLA_PAD, dv=MLA_V, out_rows=t_all)
        o_m = attention(q_m, 0, [(k_m, v_m, 0, seq)], batch=batch, lq=seq, out_row0=0,
                        prev_out=jnp.zeros((t_all, mla_heads * MLA_V), BF16), name="mla_attn_ctx", **mla_common)
        o_m = attention(q_m, t_p, [(k_m_ctx, v_m_ctx, 0, past), (k_m, v_m, t_p, dec_seq)], batch=dec_batch,
                        lq=dec_seq, out_row0=t_p, prev_out=o_m, name="mla_attn_lat", **mla_common)

        q_g, k_g_rot, k_g, v_g_b = gqa_prep(u, cols, g_q_gqa[l], g_k_gqa[l], cos_g, sin_g, tm=tm, q_scale=q_scale_g)
        ctx_k = cache_gqa_k[:, l].reshape(n_ctx, kv_heads * HEAD_DIM).astype(BF16)
        ctx_v = cache_gqa_v[:, l].reshape(n_ctx, kv_heads * HEAD_DIM).astype(BF16)
        gqa_common = dict(kv_heads=kv_heads, group=group, dq=HEAD_DIM, dv=HEAD_DIM, out_rows=t_all)
        o_g = attention(q_g, 0, [(k_g_rot, v_g_b, 0, seq)], batch=batch, lq=seq, out_row0=0,
                        prev_out=jnp.zeros((t_all, gqa_heads * HEAD_DIM), BF16), name="gqa_attn_ctx", **gqa_common)
        o_g = attention(q_g, t_p, [(ctx_k, ctx_v, 0, past), (k_g_rot, v_g_b, t_p, dec_seq)], batch=dec_batch,
                        lq=dec_seq, out_row0=t_p, prev_out=o_g, name="gqa_attn_lat", **gqa_common)

        qkv_c = gdn_prep(u, cols['qkv'], w_conv[l], heads=gdn_heads, tm=tm, t_prompt=t_p, seq_lens=(seq, dec_seq))
        tail = u[:, tail0:tail0 + MLA_ROPE + 4 * gdn_heads]
        dcs, beta = _gdn_gates(tail[:, MLA_ROPE:MLA_ROPE + 2 * gdn_heads], tail[:, MLA_ROPE + 2 * gdn_heads:],
                               a_log[l], dt_bias[l], gdn_heads)
        uw = gdn_solve(qkv_c, dcs, beta, heads=gdn_heads)
        s0 = jnp.concatenate([jnp.zeros((batch, 2, gdn_heads, GDN_DK, GDN_DV), F32), state_gdn[:, l]], axis=0)
        o_fwd, o_bwd, s_fin = gdn_scan(qkv_c, uw, dcs, s0, sequences=sequences, heads=gdn_heads)

        new_ckv.append(ckv_n[:t_p].reshape(batch, seq, kv_lora))
        new_krope.append(tail[:t_p, :MLA_ROPE].reshape(batch, seq, MLA_ROPE))
        new_k.append(k_g[:t_p].reshape(batch, seq, kv_heads, HEAD_DIM))
        new_v.append(u[:t_p, cols['gv'][0]:cols['gv'][0] + cols['gv'][1]].reshape(batch, seq, kv_heads, HEAD_DIM))
        new_s.append(s_fin[:batch])

        x = out_proj_residual(o_m, o_g, o_fwd, o_bwd, u, cols['z'], g_gdn_out[l], w_out[l].astype(BF16), x, mod,
                              gate_idx=2, tm=tm, row_fn=row_fn)

        w_router = jnp.concatenate([w_router_group[l], w_router_expert[l],
                                    jnp.zeros((d, LANE - n_groups - n_experts), F32)], axis=1)
        b_router = jnp.concatenate([b_router_group[l], b_router_expert[l],
                                    jnp.zeros((LANE - n_groups - n_experts,), F32)]).reshape(1, LANE)
        h2, gates, experts = ffn_norm_router(x, g_norm_ffn[l], mod, w_router, b_router, sh_idx=3, sc_idx=4, tm=tm,
                                             row_fn=row_fn, n_groups=n_groups, epg=n_experts // n_groups)
        dest, slot_tok, block_e, n_used = _dispatch_plan(experts[:, :TOP_K], n_experts, moe_tm)
        yb = moe_experts(h2, slot_tok, block_e, n_used, w_gate, w_up, w_down, tm=moe_tm, layer=l)
        x = moe_combine_residual(yb, dest, gates, x, mod, gate_idx=5, tc=combine_tc,
                                 row_fn=_mod_row_fn(combine_tc, t_p, dec_seq))

    return (x[:t_p].reshape(batch, seq, d), x[t_p:].reshape(dec_batch, dec_seq, d),
            jnp.stack(new_ckv, axis=1), jnp.stack(new_krope, axis=1), jnp.stack(new_k, axis=1),
            jnp.stack(new_v, axis=1), jnp.stack(new_s, axis=1))
```
